```python
import math
import jax, jax.numpy as jnp
from jax import lax
import numpy as np

D_MODEL = 1024
BATCH = 8
SEQ = 2048
DEPTH = 4
DEC_BATCH = 128
DEC_SEQ = 1
PAST_LEN = 16384
PAGE_SIZE = 128

N_EVEN = (DEPTH + 1) // 2
N_ODD = DEPTH // 2
D_A = D_MODEL // 2
D_B = D_MODEL - D_A
D_C = D_MODEL // 2
D_D = D_MODEL - D_C
D_IN = 2 * D_MODEL
A_HEADS = 4
A_HEAD_DIM = D_A // A_HEADS
CHUNK = 128
B_CONV_WIDTH = 31
C_CONV_WIDTH = 3
FFN_CONV_WIDTH = 3
S5_GROUP = 16
S5_GROUPS = D_D // S5_GROUP
S5_STATE = 64
D_FF = 2816
EPS = 1e-6
DT_MIN = 1e-3
DT_MAX = 1e-1

kernel_name = 'hybrid_gmlp_conformer_shortconv_s5_decoder_step'


def rms_norm(x, g):
    xf = x.astype(jnp.float32)
    y = xf * lax.rsqrt(jnp.mean(xf * xf, axis=-1, keepdims=True) + EPS)
    return (y * g.astype(jnp.float32)).astype(x.dtype)


def layer_norm(x, g, b):
    xf = x.astype(jnp.float32)
    mu = jnp.mean(xf, axis=-1, keepdims=True)
    var = jnp.mean(jnp.square(xf - mu), axis=-1, keepdims=True)
    y = (xf - mu) * lax.rsqrt(var + EPS)
    return (y * g.astype(jnp.float32) + b.astype(jnp.float32)).astype(x.dtype)


def causal_dwconv(x, w, buf):
    k = w.shape[0]
    if buf is None:
        buf = jnp.zeros((x.shape[0], k - 1, x.shape[2]), x.dtype)
    xp = jnp.concatenate([buf.astype(x.dtype), x], axis=1)
    y = lax.conv_general_dilated(xp, w[:, None, :].astype(x.dtype), window_strides=(1,), padding='VALID',
                                 dimension_numbers=('NWC', 'WIO', 'NWC'), feature_group_count=x.shape[2])
    return y, xp[:, -(k - 1):]


def chunk_spatial_mix(v, w_s, b_s):
    bsz, t, h, dh = v.shape
    n_chunks = -(-t // CHUNK)
    pad = n_chunks * CHUNK - t
    vp = jnp.pad(v, ((0, 0), (0, pad), (0, 0), (0, 0))).reshape(bsz, n_chunks, CHUNK, h, dh)
    mask = jnp.tril(jnp.ones((CHUNK, CHUNK), dtype=bool))
    w = jnp.where(mask[None], w_s, 0).astype(v.dtype)
    mixed = jnp.einsum('hts,bnshd->bnthd', w, vp) + b_s.T[None, None, :, :, None].astype(v.dtype)
    return mixed.reshape(bsz, n_chunks * CHUNK, h, dh)[:, :t]


def s5_scan(u, lam_re, lam_im, log_dt, b_re, b_im, c_re, c_im, d_skip, h0_re, h0_im):
    f32 = jnp.float32
    lam = lax.complex(lam_re.astype(f32), lam_im.astype(f32))
    dt = jnp.exp(log_dt.astype(f32))[:, None]
    lam_bar = jnp.exp(lam * dt)
    b = lax.complex(b_re.astype(f32), b_im.astype(f32))
    b_bar = ((lam_bar - 1.0) / lam)[..., None] * b
    c = lax.complex(c_re.astype(f32), c_im.astype(f32))
    uf = u.astype(f32)
    bu = jnp.einsum('gpc,btgc->btgp', b_bar, uf)
    if h0_re is not None:
        h0 = lax.complex(h0_re.astype(f32), h0_im.astype(f32))
        bu = bu.at[:, 0].add(lam_bar * h0)
    a = jnp.broadcast_to(lam_bar, bu.shape)

    def combine(left, right):
        a_l, b_l = left
        a_r, b_r = right
        return a_l * a_r, a_r * b_l + b_r

    _, h = lax.associative_scan(combine, (a, bu), axis=1)
    y = jnp.einsum('gcp,btgp->btgc', c, h).real + d_skip.astype(f32) * uf
    h_last = h[:, -1]
    return y.astype(u.dtype), h_last.real.astype(u.dtype), h_last.imag.astype(u.dtype)


def even_mixer(xn, w_in, w_out, a_ln_g, a_ln_b, a_ws, a_bs, b_conv_w, b_conv_b, b_ln_g, b_ln_b, buf_b):
    bsz, t, _ = xn.shape
    z = xn @ w_in
    za = jax.nn.gelu(z[..., :2 * D_A])
    zb = z[..., 2 * D_A:]
    u = za[..., :D_A]
    v = layer_norm(za[..., D_A:].reshape(bsz, t, A_HEADS, A_HEAD_DIM), a_ln_g, a_ln_b)
    gate = chunk_spatial_mix(v, a_ws, a_bs).reshape(bsz, t, D_A)
    out_a = u * gate
    glu = zb[..., :D_B] * jax.nn.sigmoid(zb[..., D_B:])
    conv, new_buf_b = causal_dwconv(glu, b_conv_w, buf_b)
    out_b = jax.nn.silu(layer_norm(conv + b_conv_b.astype(conv.dtype), b_ln_g, b_ln_b))
    out = jnp.concatenate([out_a, out_b], axis=-1) @ w_out
    return out, new_buf_b, v.reshape(bsz, t, D_A)


def odd_mixer(xn, w_in, w_out, c_conv_w, lam_re, lam_im, log_dt, b_re, b_im, c_re, c_im, d_skip,
              glu_w, glu_b, buf_c, h0_re, h0_im):
    bsz, t, _ = xn.shape
    z = xn @ w_in
    h_in = z[..., :D_C]
    gate_b = z[..., D_C:2 * D_C]
    gate_c = z[..., 2 * D_C:3 * D_C]
    conv, new_buf_c = causal_dwconv(gate_c * h_in, c_conv_w, buf_c)
    out_c = gate_b * conv
    u = z[..., 3 * D_C:].reshape(bsz, t, S5_GROUPS, S5_GROUP)
    y, h_re, h_im = s5_scan(u, lam_re, lam_im, log_dt, b_re, b_im, c_re, c_im, d_skip, h0_re, h0_im)
    g = jax.nn.gelu(y.reshape(bsz, t, D_D))
    out_d = g * jax.nn.sigmoid(g @ glu_w + glu_b)
    out = jnp.concatenate([out_c, out_d], axis=-1) @ w_out
    return out, new_buf_c, h_re, h_im


def conv_ffn(xn, w_in, conv_w, w_down, buf):
    z = xn @ w_in
    g, new_buf = causal_dwconv(z[..., :D_FF], conv_w, buf)
    return (jax.nn.silu(g) * z[..., D_FF:]) @ w_down, new_buf


def pick(s, i):
    return None if s is None else s[i]


def trunk(x, buf_b, buf_c, ssm_re, ssm_im, buf_f,
          norm_mix, norm_ffn, norm_final, w_mix_in, w_mix_out,
          a_ln_g, a_ln_b, a_ws, a_bs, b_conv_w, b_conv_b, b_ln_g, b_ln_b,
          c_conv_w, s5_lam_re, s5_lam_im, s5_log_dt, s5_b_re, s5_b_im, s5_c_re, s5_c_im, s5_d,
          s5_glu_w, s5_glu_b, ffn_w_in, ffn_conv_w, ffn_w_down):
    v_rows, nb, nc, nre, nim, nf = [], [], [], [], [], []
    for l in range(DEPTH):
        xn = rms_norm(x, norm_mix[l])
        if l % 2 == 0:
            e = l // 2
            out, b_new, v = even_mixer(xn, w_mix_in[l], w_mix_out[l], a_ln_g[e], a_ln_b[e], a_ws[e], a_bs[e],
                                       b_conv_w[e], b_conv_b[e], b_ln_g[e], b_ln_b[e], pick(buf_b, e))
            v_rows.append(v)
            nb.append(b_new)
        else:
            o = l // 2
            out, c_new, h_re, h_im = odd_mixer(xn, w_mix_in[l], w_mix_out[l], c_conv_w[o], s5_lam_re[o],
                                               s5_lam_im[o], s5_log_dt[o], s5_b_re[o], s5_b_im[o], s5_c_re[o],
                                               s5_c_im[o], s5_d[o], s5_glu_w[o], s5_glu_b[o], pick(buf_c, o),
                                               pick(ssm_re, o), pick(ssm_im, o))
            nc.append(c_new)
            nre.append(h_re)
            nim.append(h_im)
        x = x + out
        f, f_new = conv_ffn(rms_norm(x, norm_ffn[l]), ffn_w_in[l], ffn_conv_w[l], ffn_w_down[l], pick(buf_f, l))
        x = x + f
        nf.append(f_new)
    return (rms_norm(x, norm_final), jnp.stack(v_rows), jnp.stack(nb), jnp.stack(nc),
            jnp.stack(nre), jnp.stack(nim), jnp.stack(nf))


def setup_inputs(seed: int = 0) -> dict:
    key = jax.random.key(seed)
    ks = jax.random.split(key, 40)
    nrm = jax.random.normal
    f32 = jnp.float32
    d = {}
    d['x_prompt'] = nrm(ks[0], (BATCH, SEQ, D_MODEL), f32)
    d['x_sample'] = nrm(ks[1], (DEC_BATCH, DEC_SEQ, D_MODEL), f32)
    d['state_conv_b'] = 0.5 * nrm(ks[2], (N_EVEN, DEC_BATCH, B_CONV_WIDTH - 1, D_B), f32)
    d['state_conv_c'] = 0.5 * nrm(ks[3], (N_ODD, DEC_BATCH, C_CONV_WIDTH - 1, D_C), f32)
    d['state_ssm_re'] = 0.5 * nrm(ks[4], (N_ODD, DEC_BATCH, S5_GROUPS, S5_STATE), f32)
    d['state_ssm_im'] = 0.5 * nrm(ks[5], (N_ODD, DEC_BATCH, S5_GROUPS, S5_STATE), f32)
    d['state_ffn_conv'] = nrm(ks[6], (DEPTH, DEC_BATCH, FFN_CONV_WIDTH - 1, D_FF), f32)
    d['norm_mix'] = 1.0 + 0.02 * nrm(ks[7], (DEPTH, D_MODEL), f32)
    d['norm_ffn'] = 1.0 + 0.02 * nrm(ks[8], (DEPTH, D_MODEL), f32)
    d['norm_final'] = 1.0 + 0.02 * nrm(ks[9], (D_MODEL,), f32)
    d['w_mix_in'] = nrm(ks[10], (DEPTH, D_MODEL, D_IN), f32) * D_MODEL ** -0.5
    d['w_mix_out'] = nrm(ks[11], (DEPTH, D_MODEL, D_MODEL), f32) * D_MODEL ** -0.5
    d['a_ln_g'] = 1.0 + 0.02 * nrm(ks[12], (N_EVEN, A_HEADS, A_HEAD_DIM), f32)
    d['a_ln_b'] = 0.02 * nrm(ks[13], (N_EVEN, A_HEADS, A_HEAD_DIM), f32)
    d['a_ws'] = nrm(ks[14], (N_EVEN, A_HEADS, CHUNK, CHUNK), f32) * CHUNK ** -0.5
    d['a_bs'] = 1.0 + 0.02 * nrm(ks[15], (N_EVEN, A_HEADS, CHUNK), f32)
    d['b_conv_w'] = nrm(ks[16], (N_EVEN, B_CONV_WIDTH, D_B), f32) * B_CONV_WIDTH ** -0.5
    d['b_conv_b'] = 0.02 * nrm(ks[17], (N_EVEN, D_B), f32)
    d['b_ln_g'] = 1.0 + 0.02 * nrm(ks[18], (N_EVEN, D_B), f32)
    d['b_ln_b'] = 0.02 * nrm(ks[19], (N_EVEN, D_B), f32)
    d['c_conv_w'] = nrm(ks[20], (N_ODD, C_CONV_WIDTH, D_C), f32) * C_CONV_WIDTH ** -0.5
    d['s5_lam_re'] = -0.5 + 0.01 * nrm(ks[21], (N_ODD, S5_GROUPS, S5_STATE), f32)
    d['s5_lam_im'] = (jnp.pi * jnp.arange(S5_STATE, dtype=f32))[None, None, :] + 0.01 * nrm(ks[22], (N_ODD, S5_GROUPS, S5_STATE), f32)
    d['s5_log_dt'] = jax.random.uniform(ks[23], (N_ODD, S5_GROUPS), f32, minval=math.log(DT_MIN), maxval=math.log(DT_MAX))
    d['s5_b_re'] = nrm(ks[24], (N_ODD, S5_GROUPS, S5_STATE, S5_GROUP), f32) * (2 * S5_GROUP) ** -0.5
    d['s5_b_im'] = nrm(ks[25], (N_ODD, S5_GROUPS, S5_STATE, S5_GROUP), f32) * (2 * S5_GROUP) ** -0.5
    d['s5_c_re'] = nrm(ks[26], (N_ODD, S5_GROUPS, S5_GROUP, S5_STATE), f32) * (2 * S5_STATE) ** -0.5
    d['s5_c_im'] = nrm(ks[27], (N_ODD, S5_GROUPS, S5_GROUP, S5_STATE), f32) * (2 * S5_STATE) ** -0.5
    d['s5_d'] = nrm(ks[28], (N_ODD, S5_GROUPS, S5_GROUP), f32)
    d['s5_glu_w'] = nrm(ks[29], (N_ODD, D_D, D_D), f32) * D_D ** -0.5
    d['s5_glu_b'] = 0.02 * nrm(ks[30], (N_ODD, D_D), f32)
    d['ffn_w_in'] = nrm(ks[31], (DEPTH, D_MODEL, 2 * D_FF), f32) * D_MODEL ** -0.5
    d['ffn_conv_w'] = nrm(ks[32], (DEPTH, FFN_CONV_WIDTH, D_FF), f32) * FFN_CONV_WIDTH ** -0.5
    d['ffn_w_down'] = nrm(ks[33], (DEPTH, D_FF, D_MODEL), f32) * D_FF ** -0.5
    return d


def reference(x_prompt, x_sample, state_conv_b, state_conv_c, state_ssm_re, state_ssm_im, state_ffn_conv,
              norm_mix, norm_ffn, norm_final, w_mix_in, w_mix_out,
              a_ln_g, a_ln_b, a_ws, a_bs, b_conv_w, b_conv_b, b_ln_g, b_ln_b,
              c_conv_w, s5_lam_re, s5_lam_im, s5_log_dt, s5_b_re, s5_b_im, s5_c_re, s5_c_im, s5_d,
              s5_glu_w, s5_glu_b, ffn_w_in, ffn_conv_w, ffn_w_down):
    weights = (norm_mix, norm_ffn, norm_final, w_mix_in, w_mix_out,
               a_ln_g, a_ln_b, a_ws, a_bs, b_conv_w, b_conv_b, b_ln_g, b_ln_b,
               c_conv_w, s5_lam_re, s5_lam_im, s5_log_dt, s5_b_re, s5_b_im, s5_c_re, s5_c_im, s5_d,
               s5_glu_w, s5_glu_b, ffn_w_in, ffn_conv_w, ffn_w_down)
    y_prompt, _, conv_b_p, conv_c_p, ssm_re_p, ssm_im_p, ffn_conv_p = trunk(
        x_prompt, None, None, None, None, None, *weights)
    y_sample, v_rows_s, conv_b_s, conv_c_s, ssm_re_s, ssm_im_s, ffn_conv_s = trunk(
        x_sample, state_conv_b, state_conv_c, state_ssm_re, state_ssm_im, state_ffn_conv, *weights)
    return (y_prompt, y_sample, v_rows_s, conv_b_p, conv_b_s, conv_c_p, conv_c_s,
            ssm_re_p, ssm_re_s, ssm_im_p, ssm_im_s, ffn_conv_p, ffn_conv_s)
```

```python
import functools
import math

import jax
import jax.numpy as jnp
from jax import lax
from jax.experimental import pallas as pl
from jax.experimental.pallas import tpu as pltpu

D_MODEL = 1024
DEPTH = 4
D_A = 512
D_B = 512
D_C = 512
D_D = 512
D_IN = 2048
A_HEADS = 4
A_HEAD_DIM = 128
CHUNK = 128
B_CONV_WIDTH = 31
S5_GROUP = 16
S5_GROUPS = 32
S5_STATE = 64
D_FF = 2816
EPS = 1e-6

S5_CHUNK = 16
S5_LEVELS = 7
TM = 512
FF_CW = 256
CONV_RB = 32
VMEM_LIMIT = 56 * 1024 * 1024

F32 = jnp.float32
BF16 = jnp.bfloat16


def _rms(x, g):
    return x * lax.rsqrt(jnp.mean(x * x, axis=-1, keepdims=True) + EPS) * g


def _ln(x, g, b):
    mu = jnp.mean(x, axis=-1, keepdims=True)
    xc = x - mu
    var = jnp.mean(xc * xc, axis=-1, keepdims=True)
    return xc * lax.rsqrt(var + EPS) * g + b


def _dot(a, b):
    return jnp.dot(a, b, preferred_element_type=F32)


def _const_spec(shape):
    nd = len(shape)
    return pl.BlockSpec(shape, lambda *_: (0,) * nd, pipeline_mode=pl.Buffered(1))


def _params(sem):
    return pltpu.CompilerParams(dimension_semantics=sem, vmem_limit_bytes=VMEM_LIMIT)


def _even_prompt_kernel(x_ref, g_ref, win_ref, wout_ref, alng_ref, alnb_ref, ws_ref, bs_ref,
                        bcw_ref, bcb_ref, blng_ref, blnb_ref, xo_ref, cb_ref,
                        z_ref, s_ref, cat_ref):
    b = pl.program_id(0)
    t = pl.program_id(1)
    nt = pl.num_programs(1)
    tm = x_ref.shape[1]

    @pl.when(jnp.logical_and(b == 0, t == 0))
    def _():
        s_ref[...] = jnp.zeros_like(s_ref)

    @pl.when(t == 0)
    def _():
        s_ref[:, 0:32, :] = jnp.zeros((8, 32, D_B), F32)

    xn = _rms(x_ref[0], g_ref[...]).astype(BF16)
    z_ref[...] = _dot(xn, win_ref[...])

    row = lax.broadcasted_iota(jnp.int32, (CHUNK, CHUNK), 0)
    col = lax.broadcasted_iota(jnp.int32, (CHUNK, CHUNK), 1)
    tril = row >= col
    wm = [jnp.where(tril, ws_ref[h], 0.0).astype(BF16) for h in range(A_HEADS)]

    for ch in range(tm // CHUNK):
        r0 = ch * CHUNK
        za = jax.nn.gelu(z_ref[r0:r0 + CHUNK, 0:2 * D_A])
        for h in range(A_HEADS):
            lo, hi = h * A_HEAD_DIM, (h + 1) * A_HEAD_DIM
            vh = _ln(za[:, D_A + lo:D_A + hi], alng_ref[:, lo:hi], alnb_ref[:, lo:hi])
            gate = _dot(wm[h], vh.astype(BF16)) + bs_ref[:, lo:hi]
            cat_ref[r0:r0 + CHUNK, lo:hi] = (za[:, lo:hi] * gate).astype(BF16)
        zb = z_ref[r0:r0 + CHUNK, 2 * D_A:]
        glu = zb[:, :D_B] * jax.nn.sigmoid(zb[:, D_B:])
        for s in range(8):
            s_ref[s, 32 - s + r0:32 - s + r0 + CHUNK, :] = glu

    def conv_body(i, carry):
        r0 = pl.multiple_of(i * CONV_RB, CONV_RB)
        acc = jnp.zeros((CONV_RB, D_B), F32) + bcb_ref[...]
        for m in range(2, B_CONV_WIDTH + 2):
            start = pl.multiple_of(r0 + 8 * (m // 8), 8)
            acc = acc + bcw_ref[m - 2:m - 1, :] * s_ref[m % 8, pl.ds(start, CONV_RB), :]
        y = _ln(acc, blng_ref[...], blnb_ref[...])
        cat_ref[pl.ds(r0, CONV_RB), D_A:] = (y * jax.nn.sigmoid(y)).astype(BF16)
        return carry

    lax.fori_loop(0, tm // CONV_RB, conv_body, 0)

    @pl.when(t == nt - 1)
    def _():
        cb_ref[0] = s_ref[0, tm + 2:tm + 32, :]

    s_ref[:, 0:32, :] = s_ref[:, tm:tm + 32, :]
    xo_ref[0] = x_ref[0] + _dot(cat_ref[...], wout_ref[...])


def _even_prompt(x, g, win, wout, alng, alnb, ws, bs, bcw, bcb, blng, blnb):
    bsz, seq, _ = x.shape
    row = lambda n: _const_spec((1, n))
    return pl.pallas_call(
        _even_prompt_kernel,
        grid=(bsz, seq // TM),
        in_specs=[pl.BlockSpec((1, TM, D_MODEL), lambda b, t: (b, t, 0)), row(D_MODEL),
                  _const_spec((D_MODEL, D_IN)), _const_spec((D_MODEL, D_MODEL)),
                  row(D_A), row(D_A), _const_spec((A_HEADS, CHUNK, CHUNK)), _const_spec((CHUNK, D_A)),
                  _const_spec((B_CONV_WIDTH, D_B)), row(D_B), row(D_B), row(D_B)],
        out_specs=[pl.BlockSpec((1, TM, D_MODEL), lambda b, t: (b, t, 0)),
                   pl.BlockSpec((1, B_CONV_WIDTH - 1, D_B), lambda b, t: (b, 0, 0))],
        out_shape=[jax.ShapeDtypeStruct(x.shape, F32),
                   jax.ShapeDtypeStruct((bsz, B_CONV_WIDTH - 1, D_B), F32)],
        scratch_shapes=[pltpu.VMEM((TM, D_IN), F32), pltpu.VMEM((8, TM + 32, D_B), F32),
                        pltpu.VMEM((TM, D_MODEL), BF16)],
        compiler_params=_params(("arbitrary", "arbitrary")),
        name="even_prompt",
    )(x, g, win, wout, alng, alnb, ws, bs, bcw, bcb, blng, blnb)


def _ffn_prompt_kernel(x_ref, g_ref, wg_ref, wu_ref, wd_ref, cw_ref, gf_ref, xo_ref, fc_ref,
                       xn_ref, stage_ref, halo_ref, h_ref, *, final):
    t = pl.program_id(1)
    nt = pl.num_programs(1)
    tm = x_ref.shape[1]

    @pl.when(t == 0)
    def _():
        halo_ref[...] = jnp.zeros_like(halo_ref)

    xn_ref[...] = _rms(x_ref[0], g_ref[...]).astype(BF16)
    for c0 in range(0, D_FF, FF_CW):
        zg = _dot(xn_ref[...], wg_ref[:, c0:c0 + FF_CW])
        zu = _dot(xn_ref[...], wu_ref[:, c0:c0 + FF_CW])
        stage_ref[6:8, :] = halo_ref[:, c0:c0 + FF_CW]
        stage_ref[8:8 + tm, :] = zg
        halo_ref[:, c0:c0 + FF_CW] = stage_ref[tm + 6:tm + 8, :]
        gc = (cw_ref[0:1, c0:c0 + FF_CW] * stage_ref[6:6 + tm, :]
              + cw_ref[1:2, c0:c0 + FF_CW] * stage_ref[7:7 + tm, :]
              + cw_ref[2:3, c0:c0 + FF_CW] * zg)
        h_ref[:, c0:c0 + FF_CW] = (gc * jax.nn.sigmoid(gc) * zu).astype(BF16)

    @pl.when(t == nt - 1)
    def _():
        fc_ref[0] = halo_ref[...]

    y = x_ref[0] + _dot(h_ref[...], wd_ref[...])
    if final:
        y = _rms(y, gf_ref[...])
    xo_ref[0] = y


def _ffn_prompt(x, g, wg, wu, wd, cw, gf, final):
    bsz, seq, _ = x.shape
    return pl.pallas_call(
        functools.partial(_ffn_prompt_kernel, final=final),
        grid=(bsz, seq // TM),
        in_specs=[pl.BlockSpec((1, TM, D_MODEL), lambda b, t: (b, t, 0)), _const_spec((1, D_MODEL)),
                  _const_spec((D_MODEL, D_FF)), _const_spec((D_MODEL, D_FF)),
                  _const_spec((D_FF, D_MODEL)), _const_spec((3, D_FF)), _const_spec((1, D_MODEL))],
        out_specs=[pl.BlockSpec((1, TM, D_MODEL), lambda b, t: (b, t, 0)),
                   pl.BlockSpec((1, 2, D_FF), lambda b, t: (b, 0, 0))],
        out_shape=[jax.ShapeDtypeStruct(x.shape, F32), jax.ShapeDtypeStruct((bsz, 2, D_FF), F32)],
        scratch_shapes=[pltpu.VMEM((TM, D_MODEL), BF16), pltpu.VMEM((TM + 8, FF_CW), F32),
                        pltpu.VMEM((2, D_FF), F32), pltpu.VMEM((TM, D_FF), BF16)],
        compiler_params=_params(("arbitrary", "arbitrary")),
        name="ffn_prompt",
    )(x, g, wg, wu, wd, cw, gf)


def _odd_in_kernel(x_ref, g_ref, win_ref, ccw_ref, oc_ref, u_ref, cc_ref, stage_ref):
    t = pl.program_id(1)
    nt = pl.num_programs(1)
    tm = x_ref.shape[1]

    @pl.when(t == 0)
    def _():
        stage_ref[0:8, :] = jnp.zeros((8, D_C), F32)

    xn = _rms(x_ref[0], g_ref[...]).astype(BF16)
    z = _dot(xn, win_ref[...])
    cin = z[:, 2 * D_C:3 * D_C] * z[:, :D_C]
    stage_ref[8:8 + tm, :] = cin
    conv = (ccw_ref[0:1, :] * stage_ref[6:6 + tm, :] + ccw_ref[1:2, :] * stage_ref[7:7 + tm, :]
            + ccw_ref[2:3, :] * cin)
    oc_ref[0] = (z[:, D_C:2 * D_C] * conv).astype(BF16)
    u_ref[0] = z[:, 3 * D_C:].astype(BF16)
    tail = stage_ref[tm + 6:tm + 8, :]
    stage_ref[6:8, :] = tail

    @pl.when(t == nt - 1)
    def _():
        cc_ref[0] = tail


def _odd_in(x, g, win, ccw):
    bsz, seq, _ = x.shape
    return pl.pallas_call(
        _odd_in_kernel,
        grid=(bsz, seq // TM),
        in_specs=[pl.BlockSpec((1, TM, D_MODEL), lambda b, t: (b, t, 0)), _const_spec((1, D_MODEL)),
                  _const_spec((D_MODEL, D_IN)), _const_spec((3, D_C))],
        out_specs=[pl.BlockSpec((1, TM, D_C), lambda b, t: (b, t, 0)),
                   pl.BlockSpec((1, TM, D_D), lambda b, t: (b, t, 0)),
                   pl.BlockSpec((1, 2, D_C), lambda b, t: (b, 0, 0))],
        out_shape=[jax.ShapeDtypeStruct((bsz, seq, D_C), BF16), jax.ShapeDtypeStruct((bsz, seq, D_D), BF16),
                   jax.ShapeDtypeStruct((bsz, 2, D_C), F32)],
        scratch_shapes=[pltpu.VMEM((TM + 8, D_C), F32)],
        compiler_params=_params(("arbitrary", "arbitrary")),
        name="odd_in",
    )(x, g, win, ccw)


def _s5_prompt_kernel(ug_ref, m_ref, s_ref, o_ref, a1_ref, a2_ref, y_ref, hs_ref, h_ref, *, cps):
    ug = ug_ref[0]
    x = _dot(ug, s_ref[0])
    nr = x.shape[0]
    cidx = lax.broadcasted_iota(jnp.int32, (nr, 2 * S5_STATE), 0) & (cps - 1)
    a1 = a1_ref[0]
    a2 = a2_ref[0]
    for k in range(S5_LEVELS):
        sh = 1 << k
        xs = jnp.where(cidx >= sh, pltpu.roll(x, sh, 0), 0.0)
        x = x + a1[k:k + 1, :] * xs + a2[k:k + 1, :] * pltpu.roll(xs, S5_STATE, 1)
    h_ref[...] = x
    hs_ref[0] = h_ref[pl.ds(cps - 1, nr // cps, stride=cps), :]
    hprev = jnp.where(cidx >= 1, pltpu.roll(x, 1, 0), 0.0)
    y_ref[0] = _dot(ug, m_ref[0]) + _dot(hprev.astype(BF16), o_ref[0])


def _s5_prompt(ug, m, s, o, a1, a2, bsz):
    ng, nr, lc = ug.shape
    blk = lambda *shape: pl.BlockSpec((1,) + shape, lambda g: (g, 0, 0))
    return pl.pallas_call(
        functools.partial(_s5_prompt_kernel, cps=nr // bsz),
        grid=(ng,),
        in_specs=[blk(nr, lc), blk(lc, lc), blk(lc, 2 * S5_STATE), blk(2 * S5_STATE, lc),
                  blk(8, 2 * S5_STATE), blk(8, 2 * S5_STATE)],
        out_specs=[blk(nr, lc), blk(bsz, 2 * S5_STATE)],
        out_shape=[jax.ShapeDtypeStruct((ng, nr, lc), F32), jax.ShapeDtypeStruct((ng, bsz, 2 * S5_STATE), F32)],
        scratch_shapes=[pltpu.VMEM((nr, 2 * S5_STATE), F32)],
        compiler_params=_params(("arbitrary",)),
        name="s5_prompt",
    )(ug, m, s, o, a1, a2)


def _odd_out_kernel(x_ref, oc_ref, y_ref, gw_ref, gb_ref, wout_ref, xo_ref, cat_ref):
    g = jax.nn.gelu(y_ref[0])
    od = g * jax.nn.sigmoid(_dot(g.astype(BF16), gw_ref[...]) + gb_ref[...])
    cat_ref[:, :D_C] = oc_ref[0]
    cat_ref[:, D_C:] = od.astype(BF16)
    xo_ref[0] = x_ref[0] + _dot(cat_ref[...], wout_ref[...])


def _odd_out(x, oc, y, gw, gb, wout):
    bsz, seq, _ = x.shape
    tok = lambda n: pl.BlockSpec((1, TM, n), lambda b, t: (b, t, 0))
    return pl.pallas_call(
        _odd_out_kernel,
        grid=(bsz, seq // TM),
        in_specs=[tok(D_MODEL), tok(D_C), tok(D_D), _const_spec((D_D, D_D)), _const_spec((1, D_D)),
                  _const_spec((D_MODEL, D_MODEL))],
        out_specs=tok(D_MODEL),
        out_shape=jax.ShapeDtypeStruct(x.shape, F32),
        scratch_shapes=[pltpu.VMEM((TM, D_MODEL), BF16)],
        compiler_params=_params(("arbitrary", "arbitrary")),
        name="odd_out",
    )(x, oc, y, gw, gb, wout)


def _even_sample_kernel(x_ref, g_ref, win_ref, wout_ref, alng_ref, alnb_ref, ws0_ref, bs0_ref,
                        bcw_ref, bcb_ref, blng_ref, blnb_ref, buf_ref, xo_ref, v_ref, nb_ref):
    x = x_ref[...]
    z = _dot(_rms(x, g_ref[...]).astype(BF16), win_ref[...])
    za = jax.nn.gelu(z[:, :2 * D_A])
    vs = []
    for h in range(A_HEADS):
        lo, hi = h * A_HEAD_DIM, (h + 1) * A_HEAD_DIM
        vs.append(_ln(za[:, D_A + lo:D_A + hi], alng_ref[:, lo:hi], alnb_ref[:, lo:hi]))
    v = jnp.concatenate(vs, axis=-1)
    v_ref[...] = v
    gate = ws0_ref[...] * v + bs0_ref[...]
    out_a = za[:, :D_A] * gate
    glu = z[:, 2 * D_A:2 * D_A + D_B] * jax.nn.sigmoid(z[:, 2 * D_A + D_B:])
    nk = B_CONV_WIDTH - 1
    acc = bcb_ref[...] + bcw_ref[nk:nk + 1, :] * glu
    for k in range(nk):
        acc = acc + bcw_ref[k:k + 1, :] * buf_ref[:, k * D_B:(k + 1) * D_B]
    nb_ref[:, :(nk - 1) * D_B] = buf_ref[:, D_B:]
    nb_ref[:, (nk - 1) * D_B:] = glu
    y = _ln(acc, blng_ref[...], blnb_ref[...])
    out_b = y * jax.nn.sigmoid(y)
    cat = jnp.concatenate([out_a, out_b], axis=-1).astype(BF16)
    xo_ref[...] = x + _dot(cat, wout_ref[...])


def _whole(shape):
    nd = len(shape)
    return pl.BlockSpec(shape, lambda *_: (0,) * nd)


def _call_whole(kernel_fn, name, args, out_shapes):
    return pl.pallas_call(
        kernel_fn,
        grid=(1,),
        in_specs=[_const_spec(a.shape) for a in args],
        out_specs=[_whole(s.shape) for s in out_shapes],
        out_shape=out_shapes,
        compiler_params=_params(("arbitrary",)),
        name=name,
    )(*args)


def _odd_sample_kernel(x_ref, g_ref, win_ref, wout_ref, ccw_ref, buf_ref, hre_ref, him_ref,
                       lre_ref, lim_ref, bre_ref, bim_ref, cre_ref, cim_ref, dsk_ref, gw_ref, gb_ref,
                       xo_ref, nb_ref, nre_ref, nim_ref):
    x = x_ref[...]
    z = _dot(_rms(x, g_ref[...]).astype(BF16), win_ref[...])
    cin = z[:, 2 * D_C:3 * D_C] * z[:, :D_C]
    conv = ccw_ref[0:1, :] * buf_ref[:, :D_C] + ccw_ref[1:2, :] * buf_ref[:, D_C:] + ccw_ref[2:3, :] * cin
    nb_ref[:, :D_C] = buf_ref[:, D_C:]
    nb_ref[:, D_C:] = cin
    out_c = z[:, D_C:2 * D_C] * conv
    u = z[:, 3 * D_C:]
    ub = u.astype(BF16)
    hre, him, lre, lim = hre_ref[...], him_ref[...], lre_ref[...], lim_ref[...]
    nre = lre * hre - lim * him + _dot(ub, bre_ref[...])
    nim = lre * him + lim * hre + _dot(ub, bim_ref[...])
    nre_ref[...] = nre
    nim_ref[...] = nim
    y = _dot(nre.astype(BF16), cre_ref[...]) - _dot(nim.astype(BF16), cim_ref[...]) + dsk_ref[...] * u
    g = jax.nn.gelu(y)
    od = g * jax.nn.sigmoid(_dot(g.astype(BF16), gw_ref[...]) + gb_ref[...])
    cat = jnp.concatenate([out_c, od], axis=-1).astype(BF16)
    xo_ref[...] = x + _dot(cat, wout_ref[...])


def _ffn_sample_kernel(x_ref, g_ref, wg_ref, wu_ref, wd_ref, cw_ref, b0_ref, b1_ref, gf_ref,
                       xo_ref, n0_ref, n1_ref, xn_ref, acc_ref, *, final):
    j = pl.program_id(0)

    @pl.when(j == 0)
    def _():
        xn_ref[...] = _rms(x_ref[...], g_ref[...]).astype(BF16)
        acc_ref[...] = jnp.zeros_like(acc_ref)

    zg = _dot(xn_ref[...], wg_ref[...])
    zu = _dot(xn_ref[...], wu_ref[...])
    gc = cw_ref[0:1, :] * b0_ref[...] + cw_ref[1:2, :] * b1_ref[...] + cw_ref[2:3, :] * zg
    n0_ref[...] = b1_ref[...]
    n1_ref[...] = zg
    acc_ref[...] += _dot((gc * jax.nn.sigmoid(gc) * zu).astype(BF16), wd_ref[...])

    @pl.when(j == pl.num_programs(0) - 1)
    def _():
        y = x_ref[...] + acc_ref[...]
        if final:
            y = _rms(y, gf_ref[...])
        xo_ref[...] = y


def _ffn_sample(x, g, wg, wu, wd, cw, buf, gf, final):
    n = x.shape[0]
    nc = D_FF // FF_CW
    col = lambda rows: pl.BlockSpec((rows, FF_CW), lambda j: (0, j))
    return pl.pallas_call(
        functools.partial(_ffn_sample_kernel, final=final),
        grid=(nc,),
        in_specs=[_whole((n, D_MODEL)), _whole((1, D_MODEL)), col(D_MODEL), col(D_MODEL),
                  pl.BlockSpec((FF_CW, D_MODEL), lambda j: (j, 0)), col(3),
                  col(n), pl.BlockSpec((n, FF_CW), lambda j: (0, j + nc)), _whole((1, D_MODEL))],
        out_specs=[_whole((n, D_MODEL)), col(n), col(n)],
        out_shape=[jax.ShapeDtypeStruct((n, D_MODEL), F32), jax.ShapeDtypeStruct((n, D_FF), F32),
                   jax.ShapeDtypeStruct((n, D_FF), F32)],
        scratch_shapes=[pltpu.VMEM((n, D_MODEL), BF16), pltpu.VMEM((n, D_MODEL), F32)],
        compiler_params=_params(("arbitrary",)),
        name="ffn_sample",
    )(x, g, wg, wu, wd, cw, buf, buf, gf)


def _s5_prepare(lam_re, lam_im, log_dt, b_re, b_im, c_re, c_im, d_skip):
    hp = lax.Precision.HIGHEST
    ng, ns, gc, lc = S5_GROUPS, S5_STATE, S5_GROUP, S5_CHUNK
    dt = jnp.exp(log_dt)[:, None]
    er, ei = lam_re * dt, lam_im * dt
    lb_re, lb_im = jnp.exp(er) * jnp.cos(ei), jnp.exp(er) * jnp.sin(ei)
    den = lam_re * lam_re + lam_im * lam_im
    nr_, ni_ = lb_re - 1.0, lb_im
    cf_re = (nr_ * lam_re + ni_ * lam_im) / den
    cf_im = (ni_ * lam_re - nr_ * lam_im) / den
    bb_re = cf_re[..., None] * b_re - cf_im[..., None] * b_im
    bb_im = cf_re[..., None] * b_im + cf_im[..., None] * b_re

    def power(j):
        jr = j.astype(F32)[:, None, None]
        mag = jnp.exp(jr * er)
        return mag * jnp.cos(jr * ei), mag * jnp.sin(jr * ei)

    p_re, p_im = power(jnp.arange(lc + 1))
    cl_re = c_re[None] * p_re[:, :, None, :] - c_im[None] * p_im[:, :, None, :]
    cl_im = c_re[None] * p_im[:, :, None, :] + c_im[None] * p_re[:, :, None, :]
    kern = (jnp.einsum("jgcp,gpd->jgcd", cl_re[:lc], bb_re, precision=hp)
            - jnp.einsum("jgcp,gpd->jgcd", cl_im[:lc], bb_im, precision=hp))
    kern = kern.at[0].add(jnp.eye(gc, dtype=F32)[None] * d_skip[:, :, None])
    tq = jnp.arange(lc)
    lag = tq[None, :] - tq[:, None]
    toep = jnp.where((lag >= 0)[:, :, None, None, None], kern[jnp.clip(lag, 0)], 0.0)
    m = toep.transpose(2, 0, 4, 1, 3).reshape(ng, lc * gc, lc * gc)
    q_re, q_im = p_re[lc - 1 - tq], p_im[lc - 1 - tq]
    sb_re = q_re[..., None] * bb_re[None] - q_im[..., None] * bb_im[None]
    sb_im = q_re[..., None] * bb_im[None] + q_im[..., None] * bb_re[None]
    s = jnp.concatenate([sb_re, sb_im], axis=2).transpose(1, 0, 3, 2).reshape(ng, lc * gc, 2 * ns)
    o = jnp.concatenate([cl_re[1:], -cl_im[1:]], axis=3)
    o = o.transpose(1, 3, 0, 2).reshape(ng, 2 * ns, lc * gc)
    a_re, a_im = power(lc * (1 << jnp.arange(8)))
    a1 = jnp.concatenate([a_re, a_re], axis=-1).transpose(1, 0, 2)
    a2 = jnp.concatenate([-a_im, a_im], axis=-1).transpose(1, 0, 2)
    eye = jnp.eye(ng, dtype=F32)
    bd = lambda w: jnp.einsum("gab,gh->gahb", w, eye).reshape(ng * w.shape[1], ng * w.shape[2])
    bre, bim = bd(bb_re.transpose(0, 2, 1)), bd(bb_im.transpose(0, 2, 1))
    cre, cim = bd(c_re.transpose(0, 2, 1)), bd(c_im.transpose(0, 2, 1))
    return dict(m=m.astype(BF16), s=s.astype(BF16), o=o.astype(BF16), a1=a1, a2=a2,
                bre=bre.astype(BF16), bim=bim.astype(BF16), cre=cre.astype(BF16), cim=cim.astype(BF16),
                lre=lb_re.reshape(1, -1), lim=lb_im.reshape(1, -1), dsk=d_skip.reshape(1, -1))


def kernel(x_prompt, x_sample, state_conv_b, state_conv_c, state_ssm_re, state_ssm_im, state_ffn_conv, norm_mix, norm_ffn, norm_final, w_mix_in, w_mix_out, a_ln_g, a_ln_b, a_ws, a_bs, b_conv_w, b_conv_b, b_ln_g, b_ln_b, c_conv_w, s5_lam_re, s5_lam_im, s5_log_dt, s5_b_re, s5_b_im, s5_c_re, s5_c_im, s5_d, s5_glu_w, s5_glu_b, ffn_w_in, ffn_conv_w, ffn_w_down):
    bsz, seq, _ = x_prompt.shape
    nsmp = x_sample.shape[0]
    row = lambda a: a.reshape(1, -1)
    xp = x_prompt
    xs = x_sample.reshape(nsmp, D_MODEL)
    gf = row(norm_final)
    v_rows, cb_p, cb_s, cc_p, cc_s, re_p, re_s, im_p, im_s, fc_p, fc_s = ([] for _ in range(11))

    for l in range(DEPTH):
        g_mix = row(norm_mix[l])
        win = w_mix_in[l].astype(BF16)
        wout = w_mix_out[l].astype(BF16)
        if l % 2 == 0:
            e = l // 2
            alng, alnb = row(a_ln_g[e]), row(a_ln_b[e])
            bs_full = jnp.repeat(a_bs[e].T, A_HEAD_DIM, axis=1)
            common = (row(b_conv_b[e]), row(b_ln_g[e]), row(b_ln_b[e]))
            xp, cb = _even_prompt(xp, g_mix, win, wout, alng, alnb, a_ws[e], bs_full, b_conv_w[e], *common)
            cb_p.append(cb)
            ws0 = row(jnp.repeat(a_ws[e][:, 0, 0], A_HEAD_DIM))
            bs0 = row(jnp.repeat(a_bs[e][:, 0], A_HEAD_DIM))
            buf = state_conv_b[e].reshape(nsmp, -1)
            xs, v, nb = _call_whole(
                _even_sample_kernel, "even_sample",
                (xs, g_mix, win, wout, alng, alnb, ws0, bs0, b_conv_w[e], *common, buf),
                [jax.ShapeDtypeStruct((nsmp, D_MODEL), F32), jax.ShapeDtypeStruct((nsmp, D_A), F32),
                 jax.ShapeDtypeStruct(buf.shape, F32)])
            v_rows.append(v.reshape(nsmp, 1, D_A))
            cb_s.append(nb.reshape(nsmp, B_CONV_WIDTH - 1, D_B))
        else:
            o = l // 2
            p = _s5_prepare(s5_lam_re[o], s5_lam_im[o], s5_log_dt[o], s5_b_re[o], s5_b_im[o],
                            s5_c_re[o], s5_c_im[o], s5_d[o])
            gw, gb = s5_glu_w[o].astype(BF16), row(s5_glu_b[o])
            oc, u, cc = _odd_in(xp, g_mix, win, c_conv_w[o])
            cc_p.append(cc)
            nchunk = seq // S5_CHUNK
            ug = u.reshape(bsz, nchunk, S5_CHUNK, S5_GROUPS, S5_GROUP).transpose(3, 0, 1, 2, 4)
            ug = ug.reshape(S5_GROUPS, bsz * nchunk, S5_CHUNK * S5_GROUP)
            yg, hs = _s5_prompt(ug, p["m"], p["s"], p["o"], p["a1"], p["a2"], bsz)
            y = yg.reshape(S5_GROUPS, bsz, nchunk, S5_CHUNK, S5_GROUP).transpose(1, 2, 3, 0, 4)
            y = y.reshape(bsz, seq, D_D)
            hs = hs.transpose(1, 0, 2)
            re_p.append(hs[..., :S5_STATE])
            im_p.append(hs[..., S5_STATE:])
            xp = _odd_out(xp, oc, y, gw, gb, wout)
            bufc = state_conv_c[o].reshape(nsmp, -1)
            hre = state_ssm_re[o].reshape(nsmp, -1)
            him = state_ssm_im[o].reshape(nsmp, -1)
            xs, nbc, nre, nim = _call_whole(
                _odd_sample_kernel, "odd_sample",
                (xs, g_mix, win, wout, c_conv_w[o], bufc, hre, him, p["lre"], p["lim"], p["bre"], p["bim"],
                 p["cre"], p["cim"], p["dsk"], gw, gb),
                [jax.ShapeDtypeStruct((nsmp, D_MODEL), F32), jax.ShapeDtypeStruct(bufc.shape, F32),
                 jax.ShapeDtypeStruct(hre.shape, F32), jax.ShapeDtypeStruct(him.shape, F32)])
            cc_s.append(nbc.reshape(nsmp, 2, D_C))
            re_s.append(nre.reshape(nsmp, S5_GROUPS, S5_STATE))
            im_s.append(nim.reshape(nsmp, S5_GROUPS, S5_STATE))
        final = l == DEPTH - 1
        g_ffn = row(norm_ffn[l])
        wg = ffn_w_in[l][:, :D_FF].astype(BF16)
        wu = ffn_w_in[l][:, D_FF:].astype(BF16)
        wd = ffn_w_down[l].astype(BF16)
        xp, fc = _ffn_prompt(xp, g_ffn, wg, wu, wd, ffn_conv_w[l], gf, final)
        fc_p.append(fc)
        xs, n0, n1 = _ffn_sample(xs, g_ffn, wg, wu, wd, ffn_conv_w[l],
                                 state_ffn_conv[l].reshape(nsmp, -1), gf, final)
        fc_s.append(jnp.stack([n0, n1], axis=1))

    st = jnp.stack
    return (xp, xs.reshape(nsmp, 1, D_MODEL), st(v_rows), st(cb_p), st(cb_s), st(cc_p), st(cc_s),
            st(re_p), st(re_s), st(im_p), st(im_s), st(fc_p), st(fc_s))
```

```python
import functools

import numpy as np
import jax
import jax.numpy as jnp
from jax import lax
from jax.experimental import pallas as pl
from jax.experimental.pallas import tpu as pltpu

D_MODEL = 1024
DEPTH = 4
D_A = 512
D_B = 512
D_C = 512
D_D = 512
D_IN = 2048
A_HEADS = 4
A_HEAD_DIM = 128
CHUNK = 128
B_CONV_WIDTH = 31
S5_GROUP = 16
S5_GROUPS = 32
S5_STATE = 64
D_FF = 2816
EPS = 1e-6

S5_CHUNK = 16
S5_LEVELS = 7
TM = 512
FF_CW = 256
CONV_RB = 32
VMEM_LIMIT = 56 * 1024 * 1024

F32 = jnp.float32
BF16 = jnp.bfloat16


def _rms(x, g):
    return x * lax.rsqrt(jnp.mean(x * x, axis=-1, keepdims=True) + EPS) * g


def _ln(x, g, b):
    mu = jnp.mean(x, axis=-1, keepdims=True)
    xc = x - mu
    var = jnp.mean(xc * xc, axis=-1, keepdims=True)
    return xc * lax.rsqrt(var + EPS) * g + b


def _dot(a, b):
    return jnp.dot(a, b, preferred_element_type=F32)


def _dot_nt(a, b, precision=None):
    return lax.dot_general(a, b, (((1,), (1,)), ((), ())), precision=precision, preferred_element_type=F32)


def _const_spec(shape):
    nd = len(shape)
    return pl.BlockSpec(shape, lambda *_: (0,) * nd, pipeline_mode=pl.Buffered(1))


def _params(sem):
    return pltpu.CompilerParams(dimension_semantics=sem, vmem_limit_bytes=VMEM_LIMIT)


def _even_prompt_kernel(x_ref, g_ref, win_ref, wout_ref, alng_ref, alnb_ref, ws_ref, bs_ref,
                        bcw_ref, bcb_ref, blng_ref, blnb_ref, xo_ref, cb_ref,
                        z_ref, s_ref, cat_ref):
    b = pl.program_id(0)
    t = pl.program_id(1)
    nt = pl.num_programs(1)
    tm = x_ref.shape[1]

    @pl.when(jnp.logical_and(b == 0, t == 0))
    def _():
        s_ref[...] = jnp.zeros_like(s_ref)

    @pl.when(t == 0)
    def _():
        s_ref[:, 0:32, :] = jnp.zeros((8, 32, D_B), F32)

    xn = _rms(x_ref[0], g_ref[...]).astype(BF16)
    z_ref[...] = _dot(xn, win_ref[...])

    row = lax.broadcasted_iota(jnp.int32, (CHUNK, CHUNK), 0)
    col = lax.broadcasted_iota(jnp.int32, (CHUNK, CHUNK), 1)
    tril = row >= col
    wm = [jnp.where(tril, ws_ref[h], 0.0).astype(BF16) for h in range(A_HEADS)]

    for ch in range(tm // CHUNK):
        r0 = ch * CHUNK
        za = jax.nn.gelu(z_ref[r0:r0 + CHUNK, 0:2 * D_A])
        for h in range(A_HEADS):
            lo, hi = h * A_HEAD_DIM, (h + 1) * A_HEAD_DIM
            vh = _ln(za[:, D_A + lo:D_A + hi], alng_ref[:, lo:hi], alnb_ref[:, lo:hi])
            gate = _dot(wm[h], vh.astype(BF16)) + bs_ref[:, lo:hi]
            cat_ref[r0:r0 + CHUNK, lo:hi] = (za[:, lo:hi] * gate).astype(BF16)
        zb = z_ref[r0:r0 + CHUNK, 2 * D_A:]
        glu = zb[:, :D_B] * jax.nn.sigmoid(zb[:, D_B:])
        for s in range(8):
            s_ref[s, 32 - s + r0:32 - s + r0 + CHUNK, :] = glu

    def conv_body(i, carry):
        r0 = pl.multiple_of(i * CONV_RB, CONV_RB)
        acc = jnp.zeros((CONV_RB, D_B), F32) + bcb_ref[...]
        for m in range(2, B_CONV_WIDTH + 2):
            start = pl.multiple_of(r0 + 8 * (m // 8), 8)
            acc = acc + bcw_ref[m - 2:m - 1, :] * s_ref[m % 8, pl.ds(start, CONV_RB), :]
        y = _ln(acc, blng_ref[...], blnb_ref[...])
        cat_ref[pl.ds(r0, CONV_RB), D_A:] = (y * jax.nn.sigmoid(y)).astype(BF16)
        return carry

    lax.fori_loop(0, tm // CONV_RB, conv_body, 0)

    @pl.when(t == nt - 1)
    def _():
        cb_ref[0] = s_ref[0, tm + 2:tm + 32, :]

    s_ref[:, 0:32, :] = s_ref[:, tm:tm + 32, :]
    xo_ref[0] = x_ref[0] + _dot(cat_ref[...], wout_ref[...])


def _even_prompt(x, g, win, wout, alng, alnb, ws, bs, bcw, bcb, blng, blnb):
    bsz, seq, _ = x.shape
    row = lambda n: _const_spec((1, n))
    return pl.pallas_call(
        _even_prompt_kernel,
        grid=(bsz, seq // TM),
        in_specs=[pl.BlockSpec((1, TM, D_MODEL), lambda b, t: (b, t, 0)), row(D_MODEL),
                  _const_spec((D_MODEL, D_IN)), _const_spec((D_MODEL, D_MODEL)),
                  row(D_A), row(D_A), _const_spec((A_HEADS, CHUNK, CHUNK)), _const_spec((CHUNK, D_A)),
                  _const_spec((B_CONV_WIDTH, D_B)), row(D_B), row(D_B), row(D_B)],
        out_specs=[pl.BlockSpec((1, TM, D_MODEL), lambda b, t: (b, t, 0)),
                   pl.BlockSpec((1, B_CONV_WIDTH - 1, D_B), lambda b, t: (b, 0, 0))],
        out_shape=[jax.ShapeDtypeStruct(x.shape, F32),
                   jax.ShapeDtypeStruct((bsz, B_CONV_WIDTH - 1, D_B), F32)],
        scratch_shapes=[pltpu.VMEM((TM, D_IN), F32), pltpu.VMEM((8, TM + 32, D_B), F32),
                        pltpu.VMEM((TM, D_MODEL), BF16)],
        compiler_params=_params(("arbitrary", "arbitrary")),
        name="even_prompt",
    )(x, g, win, wout, alng, alnb, ws, bs, bcw, bcb, blng, blnb)


def _ffn_prompt_kernel(x_ref, g_ref, wg_ref, wu_ref, wd_ref, cw_ref, gf_ref, xo_ref, fc_ref,
                       xn_ref, stage_ref, halo_ref, h_ref, *, final):
    t = pl.program_id(1)
    nt = pl.num_programs(1)
    tm = x_ref.shape[1]

    @pl.when(t == 0)
    def _():
        halo_ref[...] = jnp.zeros_like(halo_ref)

    xn_ref[...] = _rms(x_ref[0], g_ref[...]).astype(BF16)
    for c0 in range(0, D_FF, FF_CW):
        zg = _dot(xn_ref[...], wg_ref[:, c0:c0 + FF_CW])
        zu = _dot(xn_ref[...], wu_ref[:, c0:c0 + FF_CW])
        st = stage_ref
        st[6:8, :] = halo_ref[:, c0:c0 + FF_CW]
        st[8:8 + tm, :] = zg
        halo_ref[:, c0:c0 + FF_CW] = st[tm + 6:tm + 8, :]
        gc = (cw_ref[0:1, c0:c0 + FF_CW] * st[6:6 + tm, :]
              + cw_ref[1:2, c0:c0 + FF_CW] * st[7:7 + tm, :]
              + cw_ref[2:3, c0:c0 + FF_CW] * zg)
        h_ref[:, c0:c0 + FF_CW] = (gc * jax.nn.sigmoid(gc) * zu).astype(BF16)

    @pl.when(t == nt - 1)
    def _():
        fc_ref[0] = halo_ref[...]

    y = x_ref[0] + _dot(h_ref[...], wd_ref[...])
    if final:
        y = _rms(y, gf_ref[...])
    xo_ref[0] = y


def _ffn_prompt(x, g, wg, wu, wd, cw, gf, final):
    bsz, seq, _ = x.shape
    return pl.pallas_call(
        functools.partial(_ffn_prompt_kernel, final=final),
        grid=(bsz, seq // TM),
        in_specs=[pl.BlockSpec((1, TM, D_MODEL), lambda b, t: (b, t, 0)), _const_spec((1, D_MODEL)),
                  _const_spec((D_MODEL, D_FF)), _const_spec((D_MODEL, D_FF)),
                  _const_spec((D_FF, D_MODEL)), _const_spec((3, D_FF)), _const_spec((1, D_MODEL))],
        out_specs=[pl.BlockSpec((1, TM, D_MODEL), lambda b, t: (b, t, 0)),
                   pl.BlockSpec((1, 2, D_FF), lambda b, t: (b, 0, 0))],
        out_shape=[jax.ShapeDtypeStruct(x.shape, F32), jax.ShapeDtypeStruct((bsz, 2, D_FF), F32)],
        scratch_shapes=[pltpu.VMEM((TM, D_MODEL), BF16), pltpu.VMEM((TM + 8, FF_CW), F32),
                        pltpu.VMEM((2, D_FF), F32), pltpu.VMEM((TM, D_FF), BF16)],
        compiler_params=_params(("arbitrary", "arbitrary")),
        name="ffn_prompt",
    )(x, g, wg, wu, wd, cw, gf)


def _odd_in_kernel(x_ref, g_ref, win_ref, ccw_ref, oc_ref, u_ref, cc_ref, stage_ref):
    t = pl.program_id(1)
    nt = pl.num_programs(1)
    tm = x_ref.shape[1]

    @pl.when(t == 0)
    def _():
        stage_ref[0:8, :] = jnp.zeros((8, D_C), F32)

    xn = _rms(x_ref[0], g_ref[...]).astype(BF16)
    z = _dot(xn, win_ref[...])
    cin = z[:, 2 * D_C:3 * D_C] * z[:, :D_C]
    stage_ref[8:8 + tm, :] = cin
    conv = (ccw_ref[0:1, :] * stage_ref[6:6 + tm, :] + ccw_ref[1:2, :] * stage_ref[7:7 + tm, :]
            + ccw_ref[2:3, :] * cin)
    oc_ref[0] = (z[:, D_C:2 * D_C] * conv).astype(BF16)
    u_ref[0] = z[:, 3 * D_C:].astype(BF16)
    tail = stage_ref[tm + 6:tm + 8, :]
    stage_ref[6:8, :] = tail

    @pl.when(t == nt - 1)
    def _():
        cc_ref[0] = tail


def _odd_in(x, g, win, ccw):
    bsz, seq, _ = x.shape
    return pl.pallas_call(
        _odd_in_kernel,
        grid=(bsz, seq // TM),
        in_specs=[pl.BlockSpec((1, TM, D_MODEL), lambda b, t: (b, t, 0)), _const_spec((1, D_MODEL)),
                  _const_spec((D_MODEL, D_IN)), _const_spec((3, D_C))],
        out_specs=[pl.BlockSpec((1, TM, D_C), lambda b, t: (b, t, 0)),
                   pl.BlockSpec((1, TM, D_D), lambda b, t: (b, t, 0)),
                   pl.BlockSpec((1, 2, D_C), lambda b, t: (b, 0, 0))],
        out_shape=[jax.ShapeDtypeStruct((bsz, seq, D_C), BF16), jax.ShapeDtypeStruct((bsz, seq, D_D), BF16),
                   jax.ShapeDtypeStruct((bsz, 2, D_C), F32)],
        scratch_shapes=[pltpu.VMEM((TM + 8, D_C), F32)],
        compiler_params=_params(("arbitrary", "arbitrary")),
        name="odd_in",
    )(x, g, win, ccw)


def _s5_prompt_kernel(ug_ref, mt_ref, s_ref, ot_ref, a1_ref, a2_ref, y_ref, hs_ref, h_ref, *, cps):
    ug = ug_ref[0]
    x = _dot(ug, s_ref[0])
    nr = x.shape[0]
    cidx = lax.broadcasted_iota(jnp.int32, (nr, 2 * S5_STATE), 0) & (cps - 1)
    a1 = a1_ref[0]
    a2 = a2_ref[0]
    for k in range(S5_LEVELS):
        sh = 1 << k
        xs = jnp.where(cidx >= sh, pltpu.roll(x, sh, 0), 0.0)
        x = x + a1[k:k + 1, :] * xs + a2[k:k + 1, :] * pltpu.roll(xs, S5_STATE, 1)
    h_ref[...] = x
    hs_ref[0] = h_ref[pl.ds(cps - 1, nr // cps, stride=cps), :]
    hprev = jnp.where(cidx >= 1, pltpu.roll(x, 1, 0), 0.0)
    y_ref[0] = _dot_nt(ug, mt_ref[0]) + _dot_nt(hprev.astype(BF16), ot_ref[0])


def _s5_prompt(ug, mt, s, ot, a1, a2, bsz):
    ng, nr, lc = ug.shape
    blk = lambda *shape: pl.BlockSpec((1,) + shape, lambda g: (g, 0, 0))
    return pl.pallas_call(
        functools.partial(_s5_prompt_kernel, cps=nr // bsz),
        grid=(ng,),
        in_specs=[blk(nr, lc), blk(lc, lc), blk(lc, 2 * S5_STATE), blk(lc, 2 * S5_STATE),
                  blk(8, 2 * S5_STATE), blk(8, 2 * S5_STATE)],
        out_specs=[blk(nr, lc), blk(bsz, 2 * S5_STATE)],
        out_shape=[jax.ShapeDtypeStruct((ng, nr, lc), F32), jax.ShapeDtypeStruct((ng, bsz, 2 * S5_STATE), F32)],
        scratch_shapes=[pltpu.VMEM((nr, 2 * S5_STATE), F32)],
        compiler_params=_params(("arbitrary",)),
        name="s5_prompt",
    )(ug, mt, s, ot, a1, a2)


def _odd_out_kernel(x_ref, oc_ref, y_ref, gw_ref, gb_ref, wout_ref, xo_ref, cat_ref):
    g = jax.nn.gelu(y_ref[0])
    od = g * jax.nn.sigmoid(_dot(g.astype(BF16), gw_ref[...]) + gb_ref[...])
    cat_ref[:, :D_C] = oc_ref[0]
    cat_ref[:, D_C:] = od.astype(BF16)
    xo_ref[0] = x_ref[0] + _dot(cat_ref[...], wout_ref[...])


def _odd_out(x, oc, y, gw, gb, wout):
    bsz, seq, _ = x.shape
    tok = lambda n: pl.BlockSpec((1, TM, n), lambda b, t: (b, t, 0))
    return pl.pallas_call(
        _odd_out_kernel,
        grid=(bsz, seq // TM),
        in_specs=[tok(D_MODEL), tok(D_C), tok(D_D), _const_spec((D_D, D_D)), _const_spec((1, D_D)),
                  _const_spec((D_MODEL, D_MODEL))],
        out_specs=tok(D_MODEL),
        out_shape=jax.ShapeDtypeStruct(x.shape, F32),
        scratch_shapes=[pltpu.VMEM((TM, D_MODEL), BF16)],
        compiler_params=_params(("arbitrary", "arbitrary")),
        name="odd_out",
    )(x, oc, y, gw, gb, wout)


def _even_sample_kernel(x_ref, g_ref, win_ref, wout_ref, alng_ref, alnb_ref, ws0_ref, bs0_ref,
                        bcw_ref, bcb_ref, blng_ref, blnb_ref, buf_ref, xo_ref, v_ref, nb_ref):
    x = x_ref[...]
    z = _dot(_rms(x, g_ref[...]).astype(BF16), win_ref[...])
    za = jax.nn.gelu(z[:, :2 * D_A])
    vs = []
    for h in range(A_HEADS):
        lo, hi = h * A_HEAD_DIM, (h + 1) * A_HEAD_DIM
        vs.append(_ln(za[:, D_A + lo:D_A + hi], alng_ref[:, lo:hi], alnb_ref[:, lo:hi]))
    v = jnp.concatenate(vs, axis=-1)
    v_ref[...] = v
    gate = ws0_ref[...] * v + bs0_ref[...]
    out_a = za[:, :D_A] * gate
    glu = z[:, 2 * D_A:2 * D_A + D_B] * jax.nn.sigmoid(z[:, 2 * D_A + D_B:])
    nk = B_CONV_WIDTH - 1
    acc = bcb_ref[...] + bcw_ref[nk:nk + 1, :] * glu
    for k in range(nk):
        acc = acc + bcw_ref[k:k + 1, :] * buf_ref[:, k * D_B:(k + 1) * D_B]
    nb_ref[:, :(nk - 1) * D_B] = buf_ref[:, D_B:]
    nb_ref[:, (nk - 1) * D_B:] = glu
    y = _ln(acc, blng_ref[...], blnb_ref[...])
    out_b = y * jax.nn.sigmoid(y)
    cat = jnp.concatenate([out_a, out_b], axis=-1).astype(BF16)
    xo_ref[...] = x + _dot(cat, wout_ref[...])


def _whole(shape):
    nd = len(shape)
    return pl.BlockSpec(shape, lambda *_: (0,) * nd)


def _call_whole(kernel_fn, name, args, out_shapes, scratch=()):
    return pl.pallas_call(
        kernel_fn,
        grid=(1,),
        in_specs=[_const_spec(a.shape) for a in args],
        out_specs=[_whole(s.shape) for s in out_shapes],
        out_shape=out_shapes,
        scratch_shapes=list(scratch),
        compiler_params=_params(("arbitrary",)),
        name=name,
    )(*args)


def _odd_sample_kernel(x_ref, g_ref, win_ref, wout_ref, ccw_ref, buf_ref, h_ref, p1_ref, p2_ref,
                       bbar_ref, ccn_ref, dsk_ref, gw_ref, gb_ref,
                       xo_ref, nb_ref, nh_ref, bd_ref, cd_ref):
    lw = 2 * S5_STATE
    bd_ref[...] = jnp.zeros_like(bd_ref)
    cd_ref[...] = jnp.zeros_like(cd_ref)
    for gi in range(S5_GROUPS):
        r0, l0 = gi * S5_GROUP, gi * lw
        bd_ref[r0:r0 + S5_GROUP, l0:l0 + lw] = bbar_ref[gi]
        cd_ref[r0:r0 + S5_GROUP, l0:l0 + lw] = ccn_ref[gi]
    x = x_ref[...]
    z = _dot(_rms(x, g_ref[...]).astype(BF16), win_ref[...])
    cin = z[:, 2 * D_C:3 * D_C] * z[:, :D_C]
    conv = ccw_ref[0:1, :] * buf_ref[:, :D_C] + ccw_ref[1:2, :] * buf_ref[:, D_C:] + ccw_ref[2:3, :] * cin
    nb_ref[:, :D_C] = buf_ref[:, D_C:]
    nb_ref[:, D_C:] = cin
    out_c = z[:, D_C:2 * D_C] * conv
    u = z[:, 3 * D_C:]
    h = h_ref[...]
    hswap = jnp.concatenate([pltpu.roll(h[:, gi * lw:(gi + 1) * lw], S5_STATE, 1) for gi in range(S5_GROUPS)],
                            axis=-1)
    nh = p1_ref[...] * h + p2_ref[...] * hswap + _dot(u.astype(BF16), bd_ref[...])
    nh_ref[...] = nh
    y = _dot_nt(nh.astype(BF16), cd_ref[...]) + dsk_ref[...] * u
    g = jax.nn.gelu(y)
    od = g * jax.nn.sigmoid(_dot(g.astype(BF16), gw_ref[...]) + gb_ref[...])
    cat = jnp.concatenate([out_c, od], axis=-1).astype(BF16)
    xo_ref[...] = x + _dot(cat, wout_ref[...])


def _ffn_sample_kernel(x_ref, g_ref, wg_ref, wu_ref, wd_ref, cw_ref, b0_ref, b1_ref, gf_ref,
                       xo_ref, n0_ref, n1_ref, xn_ref, acc_ref, *, final):
    j = pl.program_id(0)

    @pl.when(j == 0)
    def _():
        xn_ref[...] = _rms(x_ref[...], g_ref[...]).astype(BF16)
        acc_ref[...] = jnp.zeros_like(acc_ref)

    zg = _dot(xn_ref[...], wg_ref[...])
    zu = _dot(xn_ref[...], wu_ref[...])
    gc = cw_ref[0:1, :] * b0_ref[...] + cw_ref[1:2, :] * b1_ref[...] + cw_ref[2:3, :] * zg
    n0_ref[...] = b1_ref[...]
    n1_ref[...] = zg
    acc_ref[...] += _dot((gc * jax.nn.sigmoid(gc) * zu).astype(BF16), wd_ref[...])

    @pl.when(j == pl.num_programs(0) - 1)
    def _():
        y = x_ref[...] + acc_ref[...]
        if final:
            y = _rms(y, gf_ref[...])
        xo_ref[...] = y


def _ffn_sample(x, g, wg, wu, wd, cw, buf, gf, final):
    n = x.shape[0]
    nc = D_FF // FF_CW
    col = lambda rows: pl.BlockSpec((rows, FF_CW), lambda j: (0, j))
    return pl.pallas_call(
        functools.partial(_ffn_sample_kernel, final=final),
        grid=(nc,),
        in_specs=[_whole((n, D_MODEL)), _whole((1, D_MODEL)), col(D_MODEL), col(D_MODEL),
                  pl.BlockSpec((FF_CW, D_MODEL), lambda j: (j, 0)), col(3),
                  col(n), pl.BlockSpec((n, FF_CW), lambda j: (0, j + nc)), _whole((1, D_MODEL))],
        out_specs=[_whole((n, D_MODEL)), col(n), col(n)],
        out_shape=[jax.ShapeDtypeStruct((n, D_MODEL), F32), jax.ShapeDtypeStruct((n, D_FF), F32),
                   jax.ShapeDtypeStruct((n, D_FF), F32)],
        scratch_shapes=[pltpu.VMEM((n, D_MODEL), BF16), pltpu.VMEM((n, D_MODEL), F32)],
        compiler_params=_params(("arbitrary",)),
        name="ffn_sample",
    )(x, g, wg, wu, wd, cw, buf, buf, gf)


_S5_POW = np.concatenate([np.arange(1 - S5_CHUNK, S5_CHUNK + 1), S5_CHUNK * 2 ** np.arange(8)]).astype(np.float32)
_S5_POW_ROWS = len(_S5_POW)


def _s5_prep_kernel(pw_ref, lr_ref, li_ref, ldt_ref, bb_ref, bbs_ref, cc_ref, ccs_ref, dd_ref,
                    mt_ref, s_ref, ot_ref, a1_ref, a2_ref, bbar_ref, ccn_ref, p1_ref, p2_ref):
    lw, lc = 2 * S5_STATE, S5_CHUNK
    lane = lax.broadcasted_iota(jnp.int32, (1, lw), 1)
    sgn = jnp.where(lane < S5_STATE, -1.0, 1.0)
    lr, li = lr_ref[0], li_ref[0]
    dt = jnp.exp(ldt_ref[0])
    er, ei = lr * dt, li * dt
    jm = pw_ref[...]
    mag = jnp.exp(jm * er)
    cr, ci = mag * jnp.cos(jm * ei), mag * jnp.sin(jm * ei)

    def scale(j, x, xs):
        i = j + lc - 1
        return cr[i:i + 1] * x + (sgn * ci[i:i + 1]) * xs

    one = lc
    nr_, ni_ = cr[one:one + 1] - 1.0, ci[one:one + 1]
    den = lr * lr + li * li
    cfr = (nr_ * lr + ni_ * li) / den
    cfi = (ni_ * lr - nr_ * li) / den
    bb, bbs, cc, ccs = bb_ref[0], bbs_ref[0], cc_ref[0], ccs_ref[0]
    bbar = cfr * bb + (sgn * cfi) * bbs
    bbars = cfr * bbs - (sgn * cfi) * bb
    cat = lambda blocks: jnp.concatenate(blocks, axis=0)
    ymat = cat([scale(t, cc, ccs) for t in range(lc)])
    xmat = cat([scale(-t, bbar, bbars) * (-sgn) for t in range(lc)])
    mt = _dot_nt(ymat, xmat, precision=lax.Precision.HIGHEST)
    n = lc * S5_GROUP
    row = lax.broadcasted_iota(jnp.int32, (n, n), 0)
    col = lax.broadcasted_iota(jnp.int32, (n, n), 1)
    mt = jnp.where((row >> 4) >= (col >> 4), mt, 0.0)
    mt = mt + jnp.where(row == col, dd_ref[0], 0.0)
    mt_ref[0] = mt.astype(BF16)
    s_ref[0] = cat([scale(lc - 1 - t, bbar, bbars) for t in range(lc)]).astype(BF16)
    ot_ref[0] = cat([scale(t + 1, cc, ccs) * (-sgn) for t in range(lc)]).astype(BF16)
    a1_ref[0] = cr[2 * lc:2 * lc + 8]
    a2_ref[0] = sgn * ci[2 * lc:2 * lc + 8]
    bbar_ref[0] = bbar.astype(BF16)
    ccn_ref[0] = (cc * (-sgn)).astype(BF16)
    p1_ref[0] = cr[one:one + 1]
    p2_ref[0] = sgn * ci[one:one + 1]


def _s5_prepare(lam_re, lam_im, log_dt, b_re, b_im, c_re, c_im, d_skip):
    ng, lw, n = S5_GROUPS, 2 * S5_STATE, S5_CHUNK * S5_GROUP
    two = lambda a: jnp.concatenate([a, a], axis=-1)[:, None, :]
    pack = lambda a, b: jnp.concatenate([a, b], axis=-1)
    bt_re, bt_im = b_re.transpose(0, 2, 1), b_im.transpose(0, 2, 1)
    args = (jnp.asarray(np.tile(_S5_POW[:, None], (1, lw))), two(lam_re), two(lam_im),
            jnp.broadcast_to(log_dt[:, None, None], (ng, 1, lw)),
            pack(bt_re, bt_im), pack(bt_im, bt_re), pack(c_re, c_im), pack(c_im, c_re),
            jnp.tile(d_skip, (1, S5_CHUNK))[:, None, :])
    blk = lambda *shape: pl.BlockSpec((1,) + shape, lambda g: (g, 0, 0))
    shapes = [((n, n), BF16), ((n, lw), BF16), ((n, lw), BF16), ((8, lw), F32), ((8, lw), F32),
              ((S5_GROUP, lw), BF16), ((S5_GROUP, lw), BF16), ((1, lw), F32), ((1, lw), F32)]
    outs = pl.pallas_call(
        _s5_prep_kernel,
        grid=(ng,),
        in_specs=[_const_spec((_S5_POW_ROWS, lw))] + [blk(*a.shape[1:]) for a in args[1:]],
        out_specs=[blk(*sh) for sh, _ in shapes],
        out_shape=[jax.ShapeDtypeStruct((ng,) + sh, dtp) for sh, dtp in shapes],
        compiler_params=_params(("arbitrary",)),
        name="s5_prep",
    )(*args)
    return dict(zip(("mt", "s", "ot", "a1", "a2", "bbar", "ccn", "p1", "p2"), outs))


def _cast_kernel(w_ref, o_ref):
    o_ref[...] = w_ref[...].astype(BF16)


def _to_bf16(w, layer, col_block=0, ncols=None):
    _, k, n = w.shape
    ncols = ncols or n
    tk = 256 if k % 256 == 0 else k
    return pl.pallas_call(
        _cast_kernel,
        grid=(k // tk,),
        in_specs=[pl.BlockSpec((None, tk, ncols), lambda i: (layer, i, col_block))],
        out_specs=pl.BlockSpec((tk, ncols), lambda i: (i, 0)),
        out_shape=jax.ShapeDtypeStruct((k, ncols), BF16),
        compiler_params=_params(("arbitrary",)),
        name="to_bf16",
    )(w)


def kernel(x_prompt, x_sample, state_conv_b, state_conv_c, state_ssm_re, state_ssm_im, state_ffn_conv, norm_mix, norm_ffn, norm_final, w_mix_in, w_mix_out, a_ln_g, a_ln_b, a_ws, a_bs, b_conv_w, b_conv_b, b_ln_g, b_ln_b, c_conv_w, s5_lam_re, s5_lam_im, s5_log_dt, s5_b_re, s5_b_im, s5_c_re, s5_c_im, s5_d, s5_glu_w, s5_glu_b, ffn_w_in, ffn_conv_w, ffn_w_down):
    bsz, seq, _ = x_prompt.shape
    nsmp = x_sample.shape[0]
    row = lambda a: a.reshape(1, -1)
    xp = x_prompt
    xs = x_sample.reshape(nsmp, D_MODEL)
    gf = row(norm_final)
    v_rows, cb_p, cb_s, cc_p, cc_s, re_p, re_s, im_p, im_s, fc_p, fc_s = ([] for _ in range(11))

    for l in range(DEPTH):
        g_mix = row(norm_mix[l])
        win = _to_bf16(w_mix_in, l)
        wout = _to_bf16(w_mix_out, l)
        if l % 2 == 0:
            e = l // 2
            alng, alnb = row(a_ln_g[e]), row(a_ln_b[e])
            bs_full = jnp.repeat(a_bs[e].T, A_HEAD_DIM, axis=1)
            common = (row(b_conv_b[e]), row(b_ln_g[e]), row(b_ln_b[e]))
            xp, cb = _even_prompt(xp, g_mix, win, wout, alng, alnb, a_ws[e], bs_full, b_conv_w[e], *common)
            cb_p.append(cb)
            ws0 = row(jnp.repeat(a_ws[e][:, 0, 0], A_HEAD_DIM))
            bs0 = row(jnp.repeat(a_bs[e][:, 0], A_HEAD_DIM))
            buf = state_conv_b[e].reshape(nsmp, -1)
            xs, v, nb = _call_whole(
                _even_sample_kernel, "even_sample",
                (xs, g_mix, win, wout, alng, alnb, ws0, bs0, b_conv_w[e], *common, buf),
                [jax.ShapeDtypeStruct((nsmp, D_MODEL), F32), jax.ShapeDtypeStruct((nsmp, D_A), F32),
                 jax.ShapeDtypeStruct(buf.shape, F32)])
            v_rows.append(v.reshape(nsmp, 1, D_A))
            cb_s.append(nb.reshape(nsmp, B_CONV_WIDTH - 1, D_B))
        else:
            o = l // 2
            p = _s5_prepare(s5_lam_re[o], s5_lam_im[o], s5_log_dt[o], s5_b_re[o], s5_b_im[o],
                            s5_c_re[o], s5_c_im[o], s5_d[o])
            gw, gb = _to_bf16(s5_glu_w, o), row(s5_glu_b[o])
            oc, u, cc = _odd_in(xp, g_mix, win, c_conv_w[o])
            cc_p.append(cc)
            nchunk = seq // S5_CHUNK
            ug = u.reshape(bsz, nchunk, S5_CHUNK, S5_GROUPS, S5_GROUP).transpose(3, 0, 1, 2, 4)
            ug = ug.reshape(S5_GROUPS, bsz * nchunk, S5_CHUNK * S5_GROUP)
            yg, hs = _s5_prompt(ug, p["mt"], p["s"], p["ot"], p["a1"], p["a2"], bsz)
            y = yg.reshape(S5_GROUPS, bsz, nchunk, S5_CHUNK, S5_GROUP).transpose(1, 2, 3, 0, 4)
            y = y.reshape(bsz, seq, D_D)
            hs = hs.transpose(1, 0, 2)
            re_p.append(hs[..., :S5_STATE])
            im_p.append(hs[..., S5_STATE:])
            xp = _odd_out(xp, oc, y, gw, gb, wout)
            bufc = state_conv_c[o].reshape(nsmp, -1)
            hin = jnp.concatenate([state_ssm_re[o], state_ssm_im[o]], axis=-1).reshape(nsmp, -1)
            nstate = S5_GROUPS * 2 * S5_STATE
            xs, nbc, nh = _call_whole(
                _odd_sample_kernel, "odd_sample",
                (xs, g_mix, win, wout, c_conv_w[o], bufc, hin, p["p1"].reshape(1, -1), p["p2"].reshape(1, -1),
                 p["bbar"], p["ccn"], row(s5_d[o]), gw, gb),
                [jax.ShapeDtypeStruct((nsmp, D_MODEL), F32), jax.ShapeDtypeStruct(bufc.shape, F32),
                 jax.ShapeDtypeStruct(hin.shape, F32)],
                scratch=[pltpu.VMEM((D_D, nstate), BF16), pltpu.VMEM((D_D, nstate), BF16)])
            cc_s.append(nbc.reshape(nsmp, 2, D_C))
            nh = nh.reshape(nsmp, S5_GROUPS, 2 * S5_STATE)
            re_s.append(nh[..., :S5_STATE])
            im_s.append(nh[..., S5_STATE:])
        final = l == DEPTH - 1
        g_ffn = row(norm_ffn[l])
        wg = _to_bf16(ffn_w_in, l, 0, D_FF)
        wu = _to_bf16(ffn_w_in, l, 1, D_FF)
        wd = _to_bf16(ffn_w_down, l)
        xp, fc = _ffn_prompt(xp, g_ffn, wg, wu, wd, ffn_conv_w[l], gf, final)
        fc_p.append(fc)
        xs, n0, n1 = _ffn_sample(xs, g_ffn, wg, wu, wd, ffn_conv_w[l],
                                 state_ffn_conv[l].reshape(nsmp, -1), gf, final)
        fc_s.append(jnp.stack([n0, n1], axis=1))

    st = jnp.stack
    return (xp, xs.reshape(nsmp, 1, D_MODEL), st(v_rows), st(cb_p), st(cb_s), st(cc_p), st(cc_s),
            st(re_p), st(re_s), st(im_p), st(im_s), st(fc_p), st(fc_s))
```

```python
import functools

import numpy as np
import jax
import jax.numpy as jnp
from jax import lax
from jax.experimental import pallas as pl
from jax.experimental.pallas import tpu as pltpu

D_MODEL = 1024
DEPTH = 4
D_A = 512
D_B = 512
D_C = 512
D_D = 512
D_IN = 2048
A_HEADS = 4
A_HEAD_DIM = 128
CHUNK = 128
B_CONV_WIDTH = 31
S5_GROUP = 16
S5_GROUPS = 32
S5_STATE = 64
D_FF = 2816
EPS = 1e-6

S5_CHUNK = 16
S5_LEVELS = 7
RC = 32
PH = 4
FF_CW = 256
SUB = 8
VMEM_LIMIT = 56 * 1024 * 1024

F32 = jnp.float32
BF16 = jnp.bfloat16


def _rms(x, g):
    return x * lax.rsqrt(jnp.mean(x * x, axis=-1, keepdims=True) + EPS) * g


def _ln(x, g, b):
    mu = jnp.mean(x, axis=-1, keepdims=True)
    xc = x - mu
    var = jnp.mean(xc * xc, axis=-1, keepdims=True)
    return xc * lax.rsqrt(var + EPS) * g + b


def _dot(a, b):
    return jnp.dot(a, b, preferred_element_type=F32)


def _dot_nt(a, b, precision=None):
    return lax.dot_general(a, b, (((1,), (1,)), ((), ())), precision=precision, preferred_element_type=F32)


def _const_spec(shape):
    nd = len(shape)
    return pl.BlockSpec(shape, lambda *_: (0,) * nd, pipeline_mode=pl.Buffered(1))


def _params(sem):
    return pltpu.CompilerParams(dimension_semantics=sem, vmem_limit_bytes=VMEM_LIMIT)


def _shift_rows(blk, first_row):
    row0 = lax.broadcasted_iota(jnp.int32, blk.shape, 0) == 0
    return jnp.where(row0, first_row, pltpu.roll(blk, 1, 0))


def _chunk_tile_spec(n):
    return pl.BlockSpec((1, S5_CHUNK, RC, n), lambda b, i: (b, 0, i, 0))


def _even_prompt_kernel(x_ref, g_ref, win_ref, wout_ref, alng_ref, alnb_ref, ws_ref, bs_ref,
                        bcw_ref, bcb_ref, blng_ref, blnb_ref, xo_ref, cb_ref,
                        z_ref, p_ref, gsh_ref, oa_ref, ob_ref):
    i = pl.program_id(1)
    ni = pl.num_programs(1)
    nph, rc = x_ref.shape[1], x_ref.shape[2]
    tm = nph * rc

    @pl.when(i == 0)
    def _():
        p_ref[:, 0:SUB, :] = jnp.zeros((nph, SUB, D_B), F32)

    x = x_ref[0].reshape(tm, D_MODEL)
    z_ref[...] = _dot(_rms(x, g_ref[...]).astype(BF16), win_ref[...])

    q = CHUNK // nph
    r = lax.broadcasted_iota(jnp.int32, (CHUNK, CHUNK), 0)
    c = lax.broadcasted_iota(jnp.int32, (CHUNK, CHUNK), 1)
    pos = lambda k: (k & (q - 1)) * nph + (k >> (q.bit_length() - 1))
    keep = pos(r) >= pos(c)
    wm = [jnp.where(keep, ws_ref[h], 0.0).astype(BF16) for h in range(A_HEADS)]
    for m in range(rc // q):
        rows = [ph * rc + q * m for ph in range(nph)]
        za = jax.nn.gelu(jnp.concatenate([z_ref[r0:r0 + q, 0:2 * D_A] for r0 in rows], axis=0))
        for h in range(A_HEADS):
            lo, hi = h * A_HEAD_DIM, (h + 1) * A_HEAD_DIM
            vh = _ln(za[:, D_A + lo:D_A + hi], alng_ref[:, lo:hi], alnb_ref[:, lo:hi])
            oa = za[:, lo:hi] * (_dot(wm[h], vh.astype(BF16)) + bs_ref[:, lo:hi])
            for ph, r0 in enumerate(rows):
                oa_ref[r0:r0 + q, lo:hi] = oa[ph * q:(ph + 1) * q]

    for ph in range(nph):
        zb = z_ref[ph * rc:(ph + 1) * rc, 2 * D_A:]
        p_ref[ph, SUB:SUB + rc, :] = zb[:, :D_B] * jax.nn.sigmoid(zb[:, D_B:])
    for ph in range(nph):
        gsh_ref[0, ph] = p_ref[ph, SUB - 1:SUB - 1 + rc, :]
        gsh_ref[1, ph] = p_ref[ph, SUB - 2:SUB - 2 + rc, :]
    for ph in range(nph):
        acc = jnp.zeros((rc, D_B), F32) + bcb_ref[...]
        for j in range(B_CONV_WIDTH):
            src_ph = (ph - j) % nph
            back = (j - ph + nph - 1) // nph if j > ph else 0
            src = p_ref[src_ph, SUB:SUB + rc, :] if back == 0 else gsh_ref[back - 1, src_ph]
            k = B_CONV_WIDTH - 1 - j
            acc = acc + bcw_ref[k:k + 1, :] * src
        y = _ln(acc, blng_ref[...], blnb_ref[...])
        ob_ref[ph * rc:(ph + 1) * rc, :] = (y * jax.nn.sigmoid(y)).astype(BF16)

    @pl.when(i == ni - 1)
    def _():
        n_out = B_CONV_WIDTH - 1
        for k in range(n_out):
            back = n_out - 1 - k
            ph, cl = (nph - 1 - back) % nph, rc - 1 - back // nph
            cb_ref[0, k:k + 1, :] = p_ref[ph, SUB + cl:SUB + cl + 1, :]

    p_ref[:, SUB - 2:SUB, :] = p_ref[:, SUB + rc - 2:SUB + rc, :]
    out = x + _dot(oa_ref[...].astype(BF16), wout_ref[0:D_A, :]) + _dot(ob_ref[...], wout_ref[D_A:, :])
    xo_ref[0] = out.reshape(nph, rc, D_MODEL)


def _even_prompt(x, g, win, wout, alng, alnb, ws, bs, bcw, bcb, blng, blnb):
    bsz, nph, nchunk, _ = x.shape
    row = lambda n: _const_spec((1, n))
    tm = nph * RC
    return pl.pallas_call(
        _even_prompt_kernel,
        grid=(bsz, nchunk // RC),
        in_specs=[_chunk_tile_spec(D_MODEL), row(D_MODEL),
                  _const_spec((D_MODEL, D_IN)), _const_spec((D_MODEL, D_MODEL)),
                  row(D_A), row(D_A), _const_spec((A_HEADS, CHUNK, CHUNK)), _const_spec((CHUNK, D_A)),
                  _const_spec((B_CONV_WIDTH, D_B)), row(D_B), row(D_B), row(D_B)],
        out_specs=[_chunk_tile_spec(D_MODEL),
                   pl.BlockSpec((1, B_CONV_WIDTH - 1, D_B), lambda b, i: (b, 0, 0))],
        out_shape=[jax.ShapeDtypeStruct(x.shape, F32),
                   jax.ShapeDtypeStruct((bsz, B_CONV_WIDTH - 1, D_B), F32)],
        scratch_shapes=[pltpu.VMEM((tm, D_IN), F32), pltpu.VMEM((nph, RC + SUB, D_B), F32),
                        pltpu.VMEM((2, nph, RC, D_B), F32), pltpu.VMEM((tm, D_A), F32),
                        pltpu.VMEM((tm, D_B), BF16)],
        compiler_params=_params(("arbitrary", "arbitrary")),
        name="even_prompt",
    )(x, g, win, wout, alng, alnb, ws, bs, bcw, bcb, blng, blnb)


def _ffn_prompt_kernel(x_ref, g_ref, wg_ref, wu_ref, wd_ref, cw_ref, gf_ref, xo_ref, fc_ref,
                       xn_ref, carry_ref, h_ref, *, final):
    i = pl.program_id(1)
    ni = pl.num_programs(1)
    nph, rc = x_ref.shape[1], x_ref.shape[2]
    tm = nph * rc

    @pl.when(i == 0)
    def _():
        carry_ref[...] = jnp.zeros_like(carry_ref)

    x = x_ref[0].reshape(tm, D_MODEL)
    xn_ref[...] = _rms(x, g_ref[...]).astype(BF16)
    lo2, lo1 = (nph - 2) * rc, (nph - 1) * rc
    for c0 in range(0, D_FF, FF_CW):
        cs = slice(c0, c0 + FF_CW)
        zg = _dot(xn_ref[...], wg_ref[:, cs])
        zu = _dot(xn_ref[...], wu_ref[:, cs])
        s2 = _shift_rows(zg[lo2:lo1], carry_ref[0:1, cs])
        s1 = _shift_rows(zg[lo1:], carry_ref[1:2, cs])
        carry_ref[0:1, cs] = zg[lo1 - 1:lo1]
        carry_ref[1:2, cs] = zg[tm - 1:tm]
        z1 = jnp.concatenate([s1, zg[:lo1]], axis=0)
        z2 = jnp.concatenate([s2, s1, zg[:lo2]], axis=0)
        gc = cw_ref[0:1, cs] * z2 + cw_ref[1:2, cs] * z1 + cw_ref[2:3, cs] * zg
        h_ref[:, cs] = (gc * jax.nn.sigmoid(gc) * zu).astype(BF16)

    @pl.when(i == ni - 1)
    def _():
        fc_ref[0] = carry_ref[...]

    y = x + _dot(h_ref[...], wd_ref[...])
    if final:
        y = _rms(y, gf_ref[...])
    xo_ref[0] = y.reshape(nph, rc, D_MODEL)


def _ffn_prompt(x, g, wg, wu, wd, cw, gf, final):
    bsz, nph, nchunk, _ = x.shape
    tm = nph * RC
    return pl.pallas_call(
        functools.partial(_ffn_prompt_kernel, final=final),
        grid=(bsz, nchunk // RC),
        in_specs=[_chunk_tile_spec(D_MODEL), _const_spec((1, D_MODEL)),
                  _const_spec((D_MODEL, D_FF)), _const_spec((D_MODEL, D_FF)),
                  _const_spec((D_FF, D_MODEL)), _const_spec((3, D_FF)), _const_spec((1, D_MODEL))],
        out_specs=[_chunk_tile_spec(D_MODEL),
                   pl.BlockSpec((1, 2, D_FF), lambda b, i: (b, 0, 0))],
        out_shape=[jax.ShapeDtypeStruct(x.shape, F32), jax.ShapeDtypeStruct((bsz, 2, D_FF), F32)],
        scratch_shapes=[pltpu.VMEM((tm, D_MODEL), BF16), pltpu.VMEM((2, D_FF), F32),
                        pltpu.VMEM((tm, D_FF), BF16)],
        compiler_params=_params(("arbitrary", "arbitrary")),
        name="ffn_prompt",
    )(x, g, wg, wu, wd, cw, gf)


def _odd_in_kernel(x_ref, g_ref, win_ref, wut_ref, ccw_ref, oc_ref, ut_ref, cc_ref, cin_ref, gb_ref):
    nph, nchunk = x_ref.shape[1], x_ref.shape[2]
    for s in range(nph // PH):
        rows = slice(s * PH * nchunk, (s + 1) * PH * nchunk)
        xn = _rms(x_ref[0, s * PH:(s + 1) * PH].reshape(PH * nchunk, D_MODEL), g_ref[...]).astype(BF16)
        z = _dot(xn, win_ref[...])
        cin_ref[rows, :] = z[:, 2 * D_C:] * z[:, :D_C]
        gb_ref[rows, :] = z[:, D_C:2 * D_C]
        ut = _dot_nt(wut_ref[...], xn)
        for k in range(PH):
            ut_ref[0, s * PH + k] = ut[:, k * nchunk:(k + 1) * nchunk].astype(BF16)

    zero = jnp.zeros((1, D_C), F32)
    blk = lambda ph: cin_ref[ph * nchunk:(ph + 1) * nchunk, :]
    s2 = _shift_rows(blk(nph - 2), zero)
    s1 = _shift_rows(blk(nph - 1), zero)
    for ph in range(nph):
        p1 = blk(ph - 1) if ph >= 1 else s1
        p2 = blk(ph - 2) if ph >= 2 else (s1 if ph == 1 else s2)
        conv = ccw_ref[0:1, :] * p2 + ccw_ref[1:2, :] * p1 + ccw_ref[2:3, :] * blk(ph)
        oc_ref[0, ph] = (gb_ref[ph * nchunk:(ph + 1) * nchunk, :] * conv).astype(BF16)
    last = nchunk - 1
    cc_ref[0, 0:1, :] = cin_ref[(nph - 2) * nchunk + last:(nph - 2) * nchunk + last + 1, :]
    cc_ref[0, 1:2, :] = cin_ref[(nph - 1) * nchunk + last:(nph - 1) * nchunk + last + 1, :]


def _odd_in(x, g, win, wut, ccw):
    bsz, nph, nchunk, _ = x.shape
    seq = lambda *shape: pl.BlockSpec((1,) + shape, lambda b: (b,) + (0,) * len(shape))
    return pl.pallas_call(
        _odd_in_kernel,
        grid=(bsz,),
        in_specs=[seq(nph, nchunk, D_MODEL), _const_spec((1, D_MODEL)), _const_spec((D_MODEL, 3 * D_C)),
                  _const_spec((D_D, D_MODEL)), _const_spec((3, D_C))],
        out_specs=[seq(nph, nchunk, D_C), seq(nph, D_D, nchunk), seq(2, D_C)],
        out_shape=[jax.ShapeDtypeStruct((bsz, nph, nchunk, D_C), BF16),
                   jax.ShapeDtypeStruct((bsz, nph, D_D, nchunk), BF16),
                   jax.ShapeDtypeStruct((bsz, 2, D_C), F32)],
        scratch_shapes=[pltpu.VMEM((nph * nchunk, D_C), F32), pltpu.VMEM((nph * nchunk, D_C), F32)],
        compiler_params=_params(("arbitrary",)),
        name="odd_in",
    )(x, g, win, wut, ccw)


def _s5_prompt_kernel(ut_ref, mt_ref, st_ref, ot_ref, a1_ref, a2_ref, yt_ref, hs_ref, h_ref):
    bsz, nph = ut_ref.shape[0], ut_ref.shape[1]
    nchunk = ut_ref.shape[3]
    n = nph * S5_GROUP
    ut = jnp.concatenate([ut_ref[b].reshape(n, nchunk) for b in range(bsz)], axis=-1)
    x = _dot(st_ref[0], ut)
    cidx = lax.broadcasted_iota(jnp.int32, x.shape, 1) & (nchunk - 1)
    a1, a2 = a1_ref[0], a2_ref[0]
    swap = lambda v: jnp.concatenate([v[S5_STATE:], v[:S5_STATE]], axis=0)
    for k in range(S5_LEVELS):
        sh = 1 << k
        xs = jnp.where(cidx >= sh, pltpu.roll(x, sh, 1), 0.0)
        x = x + a1[:, k:k + 1] * xs + a2[:, k:k + 1] * swap(xs)
    h_ref[...] = x.T
    hs_ref[0] = h_ref[pl.ds(nchunk - 1, bsz, stride=nchunk), :]
    hprev = jnp.where(cidx >= 1, pltpu.roll(x, 1, 1), 0.0)
    yt = _dot(mt_ref[0], ut) + _dot(ot_ref[0], hprev.astype(BF16))
    for b in range(bsz):
        yt_ref[b] = yt[:, b * nchunk:(b + 1) * nchunk].reshape(nph, S5_GROUP, nchunk)


def _s5_prompt(ut, mt, st, ot, a1t, a2t):
    bsz, nph, _, nchunk = ut.shape
    n, lw = nph * S5_GROUP, 2 * S5_STATE
    grp = lambda *shape: pl.BlockSpec((1,) + shape, lambda g: (g, 0, 0))
    act = pl.BlockSpec((bsz, nph, S5_GROUP, nchunk), lambda g: (0, 0, g, 0))
    return pl.pallas_call(
        _s5_prompt_kernel,
        grid=(S5_GROUPS,),
        in_specs=[act, grp(n, n), grp(lw, n), grp(n, lw), grp(lw, 8), grp(lw, 8)],
        out_specs=[act, grp(bsz, lw)],
        out_shape=[jax.ShapeDtypeStruct(ut.shape, F32), jax.ShapeDtypeStruct((S5_GROUPS, bsz, lw), F32)],
        scratch_shapes=[pltpu.VMEM((bsz * nchunk, lw), F32)],
        compiler_params=_params(("arbitrary",)),
        name="s5_prompt",
    )(ut, mt, st, ot, a1t, a2t)


def _odd_out_kernel(x_ref, oc_ref, yt_ref, gw_ref, gb_ref, wout_ref, xo_ref):
    nph, nchunk = x_ref.shape[1], x_ref.shape[2]
    tm = nph * nchunk
    y = jnp.concatenate([yt_ref[0, k].T for k in range(nph)], axis=0)
    g = jax.nn.gelu(y)
    od = g * jax.nn.sigmoid(_dot(g.astype(BF16), gw_ref[...]) + gb_ref[...])
    out = (x_ref[0].reshape(tm, D_MODEL) + _dot(oc_ref[0].reshape(tm, D_C), wout_ref[0:D_C, :])
           + _dot(od.astype(BF16), wout_ref[D_C:, :]))
    xo_ref[0] = out.reshape(nph, nchunk, D_MODEL)


def _odd_out(x, oc, yt, gw, gb, wout):
    bsz, nph, nchunk, _ = x.shape
    tok = lambda n: pl.BlockSpec((1, PH, nchunk, n), lambda b, s: (b, s, 0, 0))
    return pl.pallas_call(
        _odd_out_kernel,
        grid=(bsz, nph // PH),
        in_specs=[tok(D_MODEL), tok(D_C), pl.BlockSpec((1, PH, D_D, nchunk), lambda b, s: (b, s, 0, 0)),
                  _const_spec((D_D, D_D)), _const_spec((1, D_D)), _const_spec((D_MODEL, D_MODEL))],
        out_specs=tok(D_MODEL),
        out_shape=jax.ShapeDtypeStruct(x.shape, F32),
        compiler_params=_params(("arbitrary", "arbitrary")),
        name="odd_out",
    )(x, oc, yt, gw, gb, wout)


def _even_sample_kernel(x_ref, g_ref, win_ref, wout_ref, alng_ref, alnb_ref, ws0_ref, bs0_ref,
                        bcw_ref, bcb_ref, blng_ref, blnb_ref, buf_ref, xo_ref, v_ref, nb_ref):
    x = x_ref[...]
    z = _dot(_rms(x, g_ref[...]).astype(BF16), win_ref[...])
    za = jax.nn.gelu(z[:, :2 * D_A])
    vs = []
    for h in range(A_HEADS):
        lo, hi = h * A_HEAD_DIM, (h + 1) * A_HEAD_DIM
        vs.append(_ln(za[:, D_A + lo:D_A + hi], alng_ref[:, lo:hi], alnb_ref[:, lo:hi]))
    v = jnp.concatenate(vs, axis=-1)
    v_ref[...] = v
    gate = ws0_ref[...] * v + bs0_ref[...]
    out_a = za[:, :D_A] * gate
    glu = z[:, 2 * D_A:2 * D_A + D_B] * jax.nn.sigmoid(z[:, 2 * D_A + D_B:])
    nk = B_CONV_WIDTH - 1
    acc = bcb_ref[...] + bcw_ref[nk:nk + 1, :] * glu
    for k in range(nk):
        acc = acc + bcw_ref[k:k + 1, :] * buf_ref[:, k * D_B:(k + 1) * D_B]
    nb_ref[:, :(nk - 1) * D_B] = buf_ref[:, D_B:]
    nb_ref[:, (nk - 1) * D_B:] = glu
    y = _ln(acc, blng_ref[...], blnb_ref[...])
    out_b = y * jax.nn.sigmoid(y)
    cat = jnp.concatenate([out_a, out_b], axis=-1).astype(BF16)
    xo_ref[...] = x + _dot(cat, wout_ref[...])


def _whole(shape):
    nd = len(shape)
    return pl.BlockSpec(shape, lambda *_: (0,) * nd)


def _call_whole(kernel_fn, name, args, out_shapes, scratch=()):
    return pl.pallas_call(
        kernel_fn,
        grid=(1,),
        in_specs=[_const_spec(a.shape) for a in args],
        out_specs=[_whole(s.shape) for s in out_shapes],
        out_shape=out_shapes,
        scratch_shapes=list(scratch),
        compiler_params=_params(("arbitrary",)),
        name=name,
    )(*args)


def _odd_sample_kernel(x_ref, g_ref, wina_ref, wut_ref, wout_ref, ccw_ref, buf_ref, h_ref, p1_ref, p2_ref,
                       bbar_ref, ccn_ref, dsk_ref, gw_ref, gb_ref,
                       xo_ref, nb_ref, nh_ref, bd_ref, cd_ref):
    lw = 2 * S5_STATE
    bd_ref[...] = jnp.zeros_like(bd_ref)
    cd_ref[...] = jnp.zeros_like(cd_ref)
    for gi in range(S5_GROUPS):
        r0, l0 = gi * S5_GROUP, gi * lw
        bd_ref[r0:r0 + S5_GROUP, l0:l0 + lw] = bbar_ref[gi]
        cd_ref[r0:r0 + S5_GROUP, l0:l0 + lw] = ccn_ref[gi]
    x = x_ref[...]
    xn = _rms(x, g_ref[...]).astype(BF16)
    z = _dot(xn, wina_ref[...])
    u = _dot_nt(xn, wut_ref[...])
    cin = z[:, 2 * D_C:] * z[:, :D_C]
    conv = ccw_ref[0:1, :] * buf_ref[:, :D_C] + ccw_ref[1:2, :] * buf_ref[:, D_C:] + ccw_ref[2:3, :] * cin
    nb_ref[:, :D_C] = buf_ref[:, D_C:]
    nb_ref[:, D_C:] = cin
    out_c = z[:, D_C:2 * D_C] * conv
    h = h_ref[...]
    hswap = jnp.concatenate([pltpu.roll(h[:, gi * lw:(gi + 1) * lw], S5_STATE, 1) for gi in range(S5_GROUPS)],
                            axis=-1)
    nh = p1_ref[...] * h + p2_ref[...] * hswap + _dot(u.astype(BF16), bd_ref[...])
    nh_ref[...] = nh
    y = _dot_nt(nh.astype(BF16), cd_ref[...]) + dsk_ref[...] * u
    g = jax.nn.gelu(y)
    od = g * jax.nn.sigmoid(_dot(g.astype(BF16), gw_ref[...]) + gb_ref[...])
    cat = jnp.concatenate([out_c, od], axis=-1).astype(BF16)
    xo_ref[...] = x + _dot(cat, wout_ref[...])


def _ffn_sample_kernel(x_ref, g_ref, wg_ref, wu_ref, wd_ref, cw_ref, b0_ref, b1_ref, gf_ref,
                       xo_ref, n0_ref, n1_ref, xn_ref, acc_ref, *, final):
    j = pl.program_id(0)

    @pl.when(j == 0)
    def _():
        xn_ref[...] = _rms(x_ref[...], g_ref[...]).astype(BF16)
        acc_ref[...] = jnp.zeros_like(acc_ref)

    zg = _dot(xn_ref[...], wg_ref[...])
    zu = _dot(xn_ref[...], wu_ref[...])
    gc = cw_ref[0:1, :] * b0_ref[...] + cw_ref[1:2, :] * b1_ref[...] + cw_ref[2:3, :] * zg
    n0_ref[...] = b1_ref[...]
    n1_ref[...] = zg
    acc_ref[...] += _dot((gc * jax.nn.sigmoid(gc) * zu).astype(BF16), wd_ref[...])

    @pl.when(j == pl.num_programs(0) - 1)
    def _():
        y = x_ref[...] + acc_ref[...]
        if final:
            y = _rms(y, gf_ref[...])
        xo_ref[...] = y


def _ffn_sample(x, g, wg, wu, wd, cw, buf, gf, final):
    n = x.shape[0]
    nc = D_FF // FF_CW
    col = lambda rows: pl.BlockSpec((rows, FF_CW), lambda j: (0, j))
    return pl.pallas_call(
        functools.partial(_ffn_sample_kernel, final=final),
        grid=(nc,),
        in_specs=[_whole((n, D_MODEL)), _whole((1, D_MODEL)), col(D_MODEL), col(D_MODEL),
                  pl.BlockSpec((FF_CW, D_MODEL), lambda j: (j, 0)), col(3),
                  col(n), pl.BlockSpec((n, FF_CW), lambda j: (0, j + nc)), _whole((1, D_MODEL))],
        out_specs=[_whole((n, D_MODEL)), col(n), col(n)],
        out_shape=[jax.ShapeDtypeStruct((n, D_MODEL), F32), jax.ShapeDtypeStruct((n, D_FF), F32),
                   jax.ShapeDtypeStruct((n, D_FF), F32)],
        scratch_shapes=[pltpu.VMEM((n, D_MODEL), BF16), pltpu.VMEM((n, D_MODEL), F32)],
        compiler_params=_params(("arbitrary",)),
        name="ffn_sample",
    )(x, g, wg, wu, wd, cw, buf, buf, gf)


_S5_POW = np.concatenate([np.arange(1 - S5_CHUNK, S5_CHUNK + 1), S5_CHUNK * 2 ** np.arange(8)]).astype(np.float32)
_S5_POW_ROWS = len(_S5_POW)


def _s5_prep_kernel(pw_ref, lr_ref, li_ref, ldt_ref, bb_ref, bbs_ref, cc_ref, ccs_ref, dd_ref,
                    mt_ref, st_ref, ot_ref, a1_ref, a2_ref, bbar_ref, ccn_ref, p1_ref, p2_ref):
    lw, lc = 2 * S5_STATE, S5_CHUNK
    lane = lax.broadcasted_iota(jnp.int32, (1, lw), 1)
    sgn = jnp.where(lane < S5_STATE, -1.0, 1.0)
    lr, li = lr_ref[0], li_ref[0]
    dt = jnp.exp(ldt_ref[0])
    er, ei = lr * dt, li * dt
    jm = pw_ref[...]
    mag = jnp.exp(jm * er)
    cr, ci = mag * jnp.cos(jm * ei), mag * jnp.sin(jm * ei)

    def scale(j, x, xs):
        i = j + lc - 1
        return cr[i:i + 1] * x + (sgn * ci[i:i + 1]) * xs

    one = lc
    nr_, ni_ = cr[one:one + 1] - 1.0, ci[one:one + 1]
    den = lr * lr + li * li
    cfr = (nr_ * lr + ni_ * li) / den
    cfi = (ni_ * lr - nr_ * li) / den
    bb, bbs, cc, ccs = bb_ref[0], bbs_ref[0], cc_ref[0], ccs_ref[0]
    bbar = cfr * bb + (sgn * cfi) * bbs
    bbars = cfr * bbs - (sgn * cfi) * bb
    cat = lambda blocks: jnp.concatenate(blocks, axis=0)
    ymat = cat([scale(t, cc, ccs) for t in range(lc)])
    xmat = cat([scale(-t, bbar, bbars) * (-sgn) for t in range(lc)])
    mt = _dot_nt(ymat, xmat, precision=lax.Precision.HIGHEST)
    n = lc * S5_GROUP
    row = lax.broadcasted_iota(jnp.int32, (n, n), 0)
    col = lax.broadcasted_iota(jnp.int32, (n, n), 1)
    mt = jnp.where((row >> 4) >= (col >> 4), mt, 0.0)
    mt = mt + jnp.where(row == col, dd_ref[0], 0.0)
    mt_ref[0] = mt.astype(BF16)
    smat = cat([scale(lc - 1 - t, bbar, bbars) for t in range(lc)])
    st_ref[0] = smat.T.astype(BF16)
    ot_ref[0] = cat([scale(t + 1, cc, ccs) * (-sgn) for t in range(lc)]).astype(BF16)
    a1_ref[0] = cr[2 * lc:2 * lc + 8].T
    a2_ref[0] = (sgn * ci[2 * lc:2 * lc + 8]).T
    bbar_ref[0] = bbar.astype(BF16)
    ccn_ref[0] = (cc * (-sgn)).astype(BF16)
    p1_ref[0] = cr[one:one + 1]
    p2_ref[0] = sgn * ci[one:one + 1]


def _s5_prepare(lam_re, lam_im, log_dt, b_re, b_im, c_re, c_im, d_skip):
    ng, lw, n = S5_GROUPS, 2 * S5_STATE, S5_CHUNK * S5_GROUP
    two = lambda a: jnp.concatenate([a, a], axis=-1)[:, None, :]
    pack = lambda a, b: jnp.concatenate([a, b], axis=-1)
    bt_re, bt_im = b_re.transpose(0, 2, 1), b_im.transpose(0, 2, 1)
    args = (jnp.asarray(np.tile(_S5_POW[:, None], (1, lw))), two(lam_re), two(lam_im),
            jnp.broadcast_to(log_dt[:, None, None], (ng, 1, lw)),
            pack(bt_re, bt_im), pack(bt_im, bt_re), pack(c_re, c_im), pack(c_im, c_re),
            jnp.tile(d_skip, (1, S5_CHUNK))[:, None, :])
    blk = lambda *shape: pl.BlockSpec((1,) + shape, lambda g: (g, 0, 0))
    shapes = [((n, n), BF16), ((lw, n), BF16), ((n, lw), BF16), ((lw, 8), F32), ((lw, 8), F32),
              ((S5_GROUP, lw), BF16), ((S5_GROUP, lw), BF16), ((1, lw), F32), ((1, lw), F32)]
    outs = pl.pallas_call(
        _s5_prep_kernel,
        grid=(ng,),
        in_specs=[_const_spec((_S5_POW_ROWS, lw))] + [blk(*a.shape[1:]) for a in args[1:]],
        out_specs=[blk(*sh) for sh, _ in shapes],
        out_shape=[jax.ShapeDtypeStruct((ng,) + sh, dtp) for sh, dtp in shapes],
        compiler_params=_params(("arbitrary",)),
        name="s5_prep",
    )(*args)
    return dict(zip(("mt", "st", "ot", "a1t", "a2t", "bbar", "ccn", "p1", "p2"), outs))


def _cast_kernel(w_ref, o_ref, *, transpose):
    w = w_ref[...]
    o_ref[...] = (w.T if transpose else w).astype(BF16)


def _to_bf16(w, layer, col_block=0, ncols=None, transpose=False):
    _, k, n = w.shape
    ncols = ncols or n
    tk = 256 if k % 256 == 0 else k
    out_block, out_map, out_shape = ((ncols, tk), lambda i: (0, i), (ncols, k)) if transpose else \
        ((tk, ncols), lambda i: (i, 0), (k, ncols))
    return pl.pallas_call(
        functools.partial(_cast_kernel, transpose=transpose),
        grid=(k // tk,),
        in_specs=[pl.BlockSpec((None, tk, ncols), lambda i: (layer, i, col_block))],
        out_specs=pl.BlockSpec(out_block, out_map),
        out_shape=jax.ShapeDtypeStruct(out_shape, BF16),
        compiler_params=_params(("arbitrary",)),
        name="to_bf16",
    )(w)


def kernel(x_prompt, x_sample, state_conv_b, state_conv_c, state_ssm_re, state_ssm_im, state_ffn_conv, norm_mix, norm_ffn, norm_final, w_mix_in, w_mix_out, a_ln_g, a_ln_b, a_ws, a_bs, b_conv_w, b_conv_b, b_ln_g, b_ln_b, c_conv_w, s5_lam_re, s5_lam_im, s5_log_dt, s5_b_re, s5_b_im, s5_c_re, s5_c_im, s5_d, s5_glu_w, s5_glu_b, ffn_w_in, ffn_conv_w, ffn_w_down):
    bsz, seq, _ = x_prompt.shape
    nsmp = x_sample.shape[0]
    nph, nchunk = S5_CHUNK, seq // S5_CHUNK
    row = lambda a: a.reshape(1, -1)
    xp = x_prompt.reshape(bsz, nchunk, nph, D_MODEL).transpose(0, 2, 1, 3)
    xs = x_sample.reshape(nsmp, D_MODEL)
    gf = row(norm_final)
    v_rows, cb_p, cb_s, cc_p, cc_s, re_p, re_s, im_p, im_s, fc_p, fc_s = ([] for _ in range(11))
    q = CHUNK // nph

    for l in range(DEPTH):
        g_mix = row(norm_mix[l])
        wout = _to_bf16(w_mix_out, l)
        if l % 2 == 0:
            e = l // 2
            win = _to_bf16(w_mix_in, l)
            alng, alnb = row(a_ln_g[e]), row(a_ln_b[e])
            ws_p = a_ws[e].reshape(A_HEADS, q, nph, q, nph).transpose(0, 2, 1, 4, 3).reshape(A_HEADS, CHUNK, CHUNK)
            bs_full = jnp.repeat(a_bs[e].T, A_HEAD_DIM, axis=1)
            bs_p = bs_full.reshape(q, nph, D_A).transpose(1, 0, 2).reshape(CHUNK, D_A)
            common = (row(b_conv_b[e]), row(b_ln_g[e]), row(b_ln_b[e]))
            xp, cb = _even_prompt(xp, g_mix, win, wout, alng, alnb, ws_p, bs_p, b_conv_w[e], *common)
            cb_p.append(cb)
            ws0 = row(jnp.repeat(a_ws[e][:, 0, 0], A_HEAD_DIM))
            bs0 = row(jnp.repeat(a_bs[e][:, 0], A_HEAD_DIM))
            buf = state_conv_b[e].reshape(nsmp, -1)
            xs, v, nb = _call_whole(
                _even_sample_kernel, "even_sample",
                (xs, g_mix, win, wout, alng, alnb, ws0, bs0, b_conv_w[e], *common, buf),
                [jax.ShapeDtypeStruct((nsmp, D_MODEL), F32), jax.ShapeDtypeStruct((nsmp, D_A), F32),
                 jax.ShapeDtypeStruct(buf.shape, F32)])
            v_rows.append(v.reshape(nsmp, 1, D_A))
            cb_s.append(nb.reshape(nsmp, B_CONV_WIDTH - 1, D_B))
        else:
            o = l // 2
            wina = _to_bf16(w_mix_in, l, 0, 3 * D_C)
            wut = _to_bf16(w_mix_in, l, 3, D_D, transpose=True)
            p = _s5_prepare(s5_lam_re[o], s5_lam_im[o], s5_log_dt[o], s5_b_re[o], s5_b_im[o],
                            s5_c_re[o], s5_c_im[o], s5_d[o])
            gw, gb = _to_bf16(s5_glu_w, o), row(s5_glu_b[o])
            oc, ut, cc = _odd_in(xp, g_mix, wina, wut, c_conv_w[o])
            cc_p.append(cc)
            yt, hs = _s5_prompt(ut, p["mt"], p["st"], p["ot"], p["a1t"], p["a2t"])
            hs = hs.transpose(1, 0, 2)
            re_p.append(hs[..., :S5_STATE])
            im_p.append(hs[..., S5_STATE:])
            xp = _odd_out(xp, oc, yt, gw, gb, wout)
            bufc = state_conv_c[o].reshape(nsmp, -1)
            hin = jnp.concatenate([state_ssm_re[o], state_ssm_im[o]], axis=-1).reshape(nsmp, -1)
            nstate = S5_GROUPS * 2 * S5_STATE
            xs, nbc, nh = _call_whole(
                _odd_sample_kernel, "odd_sample",
                (xs, g_mix, wina, wut, wout, c_conv_w[o], bufc, hin, p["p1"].reshape(1, -1), p["p2"].reshape(1, -1),
                 p["bbar"], p["ccn"], row(s5_d[o]), gw, gb),
                [jax.ShapeDtypeStruct((nsmp, D_MODEL), F32), jax.ShapeDtypeStruct(bufc.shape, F32),
                 jax.ShapeDtypeStruct(hin.shape, F32)],
                scratch=[pltpu.VMEM((D_D, nstate), BF16), pltpu.VMEM((D_D, nstate), BF16)])
            cc_s.append(nbc.reshape(nsmp, 2, D_C))
            nh = nh.reshape(nsmp, S5_GROUPS, 2 * S5_STATE)
            re_s.append(nh[..., :S5_STATE])
            im_s.append(nh[..., S5_STATE:])
        final = l == DEPTH - 1
        g_ffn = row(norm_ffn[l])
        wg = _to_bf16(ffn_w_in, l, 0, D_FF)
        wu = _to_bf16(ffn_w_in, l, 1, D_FF)
        wd = _to_bf16(ffn_w_down, l)
        xp, fc = _ffn_prompt(xp, g_ffn, wg, wu, wd, ffn_conv_w[l], gf, final)
        fc_p.append(fc)
        xs, n0, n1 = _ffn_sample(xs, g_ffn, wg, wu, wd, ffn_conv_w[l],
                                 state_ffn_conv[l].reshape(nsmp, -1), gf, final)
        fc_s.append(jnp.stack([n0, n1], axis=1))

    st = jnp.stack
    y_prompt = xp.transpose(0, 2, 1, 3).reshape(bsz, seq, D_MODEL)
    return (y_prompt, xs.reshape(nsmp, 1, D_MODEL), st(v_rows), st(cb_p), st(cb_s), st(cc_p), st(cc_s),
            st(re_p), st(re_s), st(im_p), st(im_s), st(fc_p), st(fc_s))
```

```python
import functools

import numpy as np
import jax
import jax.numpy as jnp
from jax import lax
from jax.experimental import pallas as pl
from jax.experimental.pallas import tpu as pltpu

D_MODEL = 1024
DEPTH = 4
D_A = 512
D_B = 512
D_C = 512
D_D = 512
D_IN = 2048
A_HEADS = 4
A_HEAD_DIM = 128
CHUNK = 128
B_CONV_WIDTH = 31
S5_GROUP = 16
S5_GROUPS = 32
S5_STATE = 64
D_FF = 2816
EPS = 1e-6

S5_CHUNK = 16
S5_LEVELS = 7
S5_PREP_GB = 8
RC = 32
PH = 4
FF_CW = 256
FS_CW = 1408
SUB = 8
VMEM_LIMIT = 56 * 1024 * 1024

F32 = jnp.float32
BF16 = jnp.bfloat16


def _rms(x, g):
    return x * lax.rsqrt(jnp.mean(x * x, axis=-1, keepdims=True) + EPS) * g


def _ln(x, g, b):
    mu = jnp.mean(x, axis=-1, keepdims=True)
    xc = x - mu
    var = jnp.mean(xc * xc, axis=-1, keepdims=True)
    return xc * lax.rsqrt(var + EPS) * g + b


def _dot(a, b):
    return jnp.dot(a, b, preferred_element_type=F32)


def _dot_nt(a, b, precision=None):
    return lax.dot_general(a, b, (((1,), (1,)), ((), ())), precision=precision, preferred_element_type=F32)


def _const_spec(shape):
    nd = len(shape)
    return pl.BlockSpec(shape, lambda *_: (0,) * nd, pipeline_mode=pl.Buffered(1))


def _layer_spec(shape2d, layer, col_block=0):
    return pl.BlockSpec((None,) + tuple(shape2d), lambda *_: (layer, 0, col_block), pipeline_mode=pl.Buffered(1))


def _params(sem):
    return pltpu.CompilerParams(dimension_semantics=sem, vmem_limit_bytes=VMEM_LIMIT)


def _shift_rows(blk, first_row):
    row0 = lax.broadcasted_iota(jnp.int32, blk.shape, 0) == 0
    return jnp.where(row0, first_row, pltpu.roll(blk, 1, 0))


def _chunk_tile_spec(n):
    return pl.BlockSpec((1, S5_CHUNK, RC, n), lambda b, i: (b, 0, i, 0))


def _even_prompt_kernel(x_ref, g_ref, win_ref, wout_ref, alng_ref, alnb_ref, ws_ref, bs_ref,
                        bcw_ref, bcb_ref, blng_ref, blnb_ref, xo_ref, cb_ref,
                        z_ref, p_ref, gsh_ref, oa_ref, ob_ref, wb_ref, *, natural_in):
    i = pl.program_id(1)
    ni = pl.num_programs(1)
    nph, rc = xo_ref.shape[1], xo_ref.shape[2]
    tm = nph * rc

    @pl.when(jnp.logical_and(pl.program_id(0) == 0, i == 0))
    def _():
        for k in range(B_CONV_WIDTH):
            wb_ref[k] = jnp.broadcast_to(bcw_ref[k:k + 1, :], (SUB, D_B))

    @pl.when(i == 0)
    def _():
        p_ref[:, 0:SUB, :] = jnp.zeros((nph, SUB, D_B), F32)

    xt = jnp.swapaxes(x_ref[0], 0, 1) if natural_in else x_ref[0]
    x = xt.reshape(tm, D_MODEL)
    z_ref[...] = _dot(_rms(x, g_ref[...]).astype(BF16), win_ref[...])

    q = CHUNK // nph
    r = lax.broadcasted_iota(jnp.int32, (CHUNK, CHUNK), 0)
    c = lax.broadcasted_iota(jnp.int32, (CHUNK, CHUNK), 1)
    pos = lambda k: (k & (q - 1)) * nph + (k >> (q.bit_length() - 1))
    keep = pos(r) >= pos(c)
    wm = [jnp.where(keep, ws_ref[h], 0.0).astype(BF16) for h in range(A_HEADS)]
    for m in range(rc // q):
        rows = [ph * rc + q * m for ph in range(nph)]
        za = jax.nn.gelu(jnp.concatenate([z_ref[r0:r0 + q, 0:2 * D_A] for r0 in rows], axis=0))
        for h in range(A_HEADS):
            lo, hi = h * A_HEAD_DIM, (h + 1) * A_HEAD_DIM
            vh = _ln(za[:, D_A + lo:D_A + hi], alng_ref[:, lo:hi], alnb_ref[:, lo:hi])
            oa = za[:, lo:hi] * (_dot(wm[h], vh.astype(BF16)) + bs_ref[:, lo:hi])
            for ph, r0 in enumerate(rows):
                oa_ref[r0:r0 + q, lo:hi] = oa[ph * q:(ph + 1) * q]

    for ph in range(nph):
        zb = z_ref[ph * rc:(ph + 1) * rc, 2 * D_A:]
        p_ref[ph, SUB:SUB + rc, :] = zb[:, :D_B] * jax.nn.sigmoid(zb[:, D_B:])
    for ph in range(nph):
        gsh_ref[0, ph] = p_ref[ph, SUB - 1:SUB - 1 + rc, :]
        gsh_ref[1, ph] = p_ref[ph, SUB - 2:SUB - 2 + rc, :]
    for ph in range(nph):
        acc = jnp.zeros((rc // SUB, SUB, D_B), F32)
        for j in range(B_CONV_WIDTH):
            src_ph = (ph - j) % nph
            back = (j - ph + nph - 1) // nph if j > ph else 0
            src = p_ref[src_ph, SUB:SUB + rc, :] if back == 0 else gsh_ref[back - 1, src_ph]
            acc = acc + wb_ref[B_CONV_WIDTH - 1 - j][None] * src.reshape(rc // SUB, SUB, D_B)
        y = _ln(acc.reshape(rc, D_B) + bcb_ref[...], blng_ref[...], blnb_ref[...])
        ob_ref[ph * rc:(ph + 1) * rc, :] = (y * jax.nn.sigmoid(y)).astype(BF16)

    @pl.when(i == ni - 1)
    def _():
        n_out = B_CONV_WIDTH - 1
        for k in range(n_out):
            back = n_out - 1 - k
            ph, cl = (nph - 1 - back) % nph, rc - 1 - back // nph
            cb_ref[0, k:k + 1, :] = p_ref[ph, SUB + cl:SUB + cl + 1, :]

    p_ref[:, SUB - 2:SUB, :] = p_ref[:, SUB + rc - 2:SUB + rc, :]
    out = x + _dot(oa_ref[...].astype(BF16), wout_ref[0:D_A, :]) + _dot(ob_ref[...], wout_ref[D_A:, :])
    xo_ref[0] = out.reshape(nph, rc, D_MODEL)


def _even_prompt(x, g, w_in, w_out, layer, alng, alnb, ws, bs, bcw, bcb, blng, blnb, natural_in):
    bsz, nph, nchunk = x.shape[0], S5_CHUNK, x.shape[1] * x.shape[2] // S5_CHUNK
    row = lambda n: _const_spec((1, n))
    tm = nph * RC
    x_spec = pl.BlockSpec((1, RC, nph, D_MODEL), lambda b, i: (b, i, 0, 0)) if natural_in else \
        _chunk_tile_spec(D_MODEL)
    return pl.pallas_call(
        functools.partial(_even_prompt_kernel, natural_in=natural_in),
        grid=(bsz, nchunk // RC),
        in_specs=[x_spec, row(D_MODEL),
                  _layer_spec((D_MODEL, D_IN), layer), _layer_spec((D_MODEL, D_MODEL), layer),
                  row(D_A), row(D_A), _const_spec((A_HEADS, CHUNK, CHUNK)), _const_spec((CHUNK, D_A)),
                  _const_spec((B_CONV_WIDTH, D_B)), row(D_B), row(D_B), row(D_B)],
        out_specs=[_chunk_tile_spec(D_MODEL),
                   pl.BlockSpec((1, B_CONV_WIDTH - 1, D_B), lambda b, i: (b, 0, 0))],
        out_shape=[jax.ShapeDtypeStruct((bsz, nph, nchunk, D_MODEL), F32),
                   jax.ShapeDtypeStruct((bsz, B_CONV_WIDTH - 1, D_B), F32)],
        scratch_shapes=[pltpu.VMEM((tm, D_IN), F32), pltpu.VMEM((nph, RC + SUB, D_B), F32),
                        pltpu.VMEM((2, nph, RC, D_B), F32), pltpu.VMEM((tm, D_A), F32),
                        pltpu.VMEM((tm, D_B), BF16), pltpu.VMEM((B_CONV_WIDTH, SUB, D_B), F32)],
        compiler_params=_params(("arbitrary", "arbitrary")),
        name="even_prompt",
    )(x, g, w_in, w_out, alng, alnb, ws, bs, bcw, bcb, blng, blnb)


def _ffn_prompt_kernel(x_ref, g_ref, wg_ref, wu_ref, wd_ref, cw_ref, gf_ref, xo_ref, fc_ref,
                       xn_ref, carry_ref, h_ref, *, final):
    i = pl.program_id(1)
    ni = pl.num_programs(1)
    nph, rc = x_ref.shape[1], x_ref.shape[2]
    tm = nph * rc

    @pl.when(i == 0)
    def _():
        carry_ref[...] = jnp.zeros_like(carry_ref)

    x = x_ref[0].reshape(tm, D_MODEL)
    xn_ref[...] = _rms(x, g_ref[...]).astype(BF16)
    lo2, lo1 = (nph - 2) * rc, (nph - 1) * rc
    for c0 in range(0, D_FF, FF_CW):
        cs = slice(c0, c0 + FF_CW)
        zg = _dot(xn_ref[...], wg_ref[:, cs])
        zu = _dot(xn_ref[...], wu_ref[:, cs])
        s2 = _shift_rows(zg[lo2:lo1], carry_ref[0:1, cs])
        s1 = _shift_rows(zg[lo1:], carry_ref[1:2, cs])
        carry_ref[0:1, cs] = zg[lo1 - 1:lo1]
        carry_ref[1:2, cs] = zg[tm - 1:tm]
        z1 = jnp.concatenate([s1, zg[:lo1]], axis=0)
        z2 = jnp.concatenate([s2, s1, zg[:lo2]], axis=0)
        gc = cw_ref[0:1, cs] * z2 + cw_ref[1:2, cs] * z1 + cw_ref[2:3, cs] * zg
        h_ref[:, cs] = (gc * jax.nn.sigmoid(gc) * zu).astype(BF16)

    @pl.when(i == ni - 1)
    def _():
        fc_ref[0] = carry_ref[...]

    y = x + _dot(h_ref[...], wd_ref[...])
    if final:
        xo_ref[0] = jnp.swapaxes(_rms(y, gf_ref[...]).reshape(nph, rc, D_MODEL), 0, 1)
    else:
        xo_ref[0] = y.reshape(nph, rc, D_MODEL)


def _ffn_prompt(x, g, w_in, w_down, layer, cw, gf, final):
    bsz, nph, nchunk, _ = x.shape
    tm = nph * RC
    out_spec, out_shape = (pl.BlockSpec((1, RC, nph, D_MODEL), lambda b, i: (b, i, 0, 0)),
                           (bsz, nchunk, nph, D_MODEL)) if final else (_chunk_tile_spec(D_MODEL), x.shape)
    return pl.pallas_call(
        functools.partial(_ffn_prompt_kernel, final=final),
        grid=(bsz, nchunk // RC),
        in_specs=[_chunk_tile_spec(D_MODEL), _const_spec((1, D_MODEL)),
                  _layer_spec((D_MODEL, D_FF), layer, 0), _layer_spec((D_MODEL, D_FF), layer, 1),
                  _layer_spec((D_FF, D_MODEL), layer), _const_spec((3, D_FF)), _const_spec((1, D_MODEL))],
        out_specs=[out_spec, pl.BlockSpec((1, 2, D_FF), lambda b, i: (b, 0, 0))],
        out_shape=[jax.ShapeDtypeStruct(out_shape, F32), jax.ShapeDtypeStruct((bsz, 2, D_FF), F32)],
        scratch_shapes=[pltpu.VMEM((tm, D_MODEL), BF16), pltpu.VMEM((2, D_FF), F32),
                        pltpu.VMEM((tm, D_FF), BF16)],
        compiler_params=_params(("arbitrary", "arbitrary")),
        name="ffn_prompt",
    )(x, g, w_in, w_in, w_down, cw, gf)


def _odd_in_kernel(x_ref, g_ref, win_ref, wut_ref, ccw_ref, oc_ref, ut_ref, cc_ref, cin_ref, gb_ref):
    nph, nchunk = x_ref.shape[1], x_ref.shape[2]
    for s in range(nph // PH):
        rows = slice(s * PH * nchunk, (s + 1) * PH * nchunk)
        xn = _rms(x_ref[0, s * PH:(s + 1) * PH].reshape(PH * nchunk, D_MODEL), g_ref[...]).astype(BF16)
        z = _dot(xn, win_ref[...])
        cin_ref[rows, :] = z[:, 2 * D_C:] * z[:, :D_C]
        gb_ref[rows, :] = z[:, D_C:2 * D_C]
        ut = _dot_nt(wut_ref[...], xn)
        for k in range(PH):
            ut_ref[0, s * PH + k] = ut[:, k * nchunk:(k + 1) * nchunk].astype(BF16)

    zero = jnp.zeros((1, D_C), F32)
    blk = lambda ph: cin_ref[ph * nchunk:(ph + 1) * nchunk, :]
    s2 = _shift_rows(blk(nph - 2), zero)
    s1 = _shift_rows(blk(nph - 1), zero)
    for ph in range(nph):
        p1 = blk(ph - 1) if ph >= 1 else s1
        p2 = blk(ph - 2) if ph >= 2 else (s1 if ph == 1 else s2)
        conv = ccw_ref[0:1, :] * p2 + ccw_ref[1:2, :] * p1 + ccw_ref[2:3, :] * blk(ph)
        oc_ref[0, ph] = (gb_ref[ph * nchunk:(ph + 1) * nchunk, :] * conv).astype(BF16)
    last = nchunk - 1
    cc_ref[0, 0:1, :] = cin_ref[(nph - 2) * nchunk + last:(nph - 2) * nchunk + last + 1, :]
    cc_ref[0, 1:2, :] = cin_ref[(nph - 1) * nchunk + last:(nph - 1) * nchunk + last + 1, :]


def _odd_in(x, g, w_in, layer, wut, ccw):
    bsz, nph, nchunk, _ = x.shape
    seq = lambda *shape: pl.BlockSpec((1,) + shape, lambda b: (b,) + (0,) * len(shape))
    return pl.pallas_call(
        _odd_in_kernel,
        grid=(bsz,),
        in_specs=[seq(nph, nchunk, D_MODEL), _const_spec((1, D_MODEL)), _layer_spec((D_MODEL, 3 * D_C), layer),
                  _const_spec((D_D, D_MODEL)), _const_spec((3, D_C))],
        out_specs=[seq(nph, nchunk, D_C), seq(nph, D_D, nchunk), seq(2, D_C)],
        out_shape=[jax.ShapeDtypeStruct((bsz, nph, nchunk, D_C), BF16),
                   jax.ShapeDtypeStruct((bsz, nph, D_D, nchunk), BF16),
                   jax.ShapeDtypeStruct((bsz, 2, D_C), F32)],
        scratch_shapes=[pltpu.VMEM((nph * nchunk, D_C), F32), pltpu.VMEM((nph * nchunk, D_C), F32)],
        compiler_params=_params(("arbitrary",)),
        name="odd_in",
    )(x, g, w_in, wut, ccw)


def _s5_prompt_kernel(ut_ref, mt_ref, st_ref, ot_ref, a1_ref, a2_ref, yt_ref, hs_ref, h_ref):
    bsz, nph = ut_ref.shape[0], ut_ref.shape[1]
    nchunk = ut_ref.shape[3]
    n = nph * S5_GROUP
    ut = jnp.concatenate([ut_ref[b].reshape(n, nchunk) for b in range(bsz)], axis=-1)
    x = _dot(st_ref[0], ut)
    cidx = lax.broadcasted_iota(jnp.int32, x.shape, 1) & (nchunk - 1)
    a1, a2 = a1_ref[0], a2_ref[0]
    swap = lambda v: jnp.concatenate([v[S5_STATE:], v[:S5_STATE]], axis=0)
    for k in range(S5_LEVELS):
        sh = 1 << k
        xs = jnp.where(cidx >= sh, pltpu.roll(x, sh, 1), 0.0)
        x = x + a1[:, k:k + 1] * xs + a2[:, k:k + 1] * swap(xs)
    h_ref[...] = x.T
    hs_ref[0] = h_ref[pl.ds(nchunk - 1, bsz, stride=nchunk), :]
    hprev = jnp.where(cidx >= 1, pltpu.roll(x, 1, 1), 0.0)
    yt = _dot(mt_ref[0], ut) + _dot(ot_ref[0], hprev.astype(BF16))
    for b in range(bsz):
        yt_ref[b] = yt[:, b * nchunk:(b + 1) * nchunk].reshape(nph, S5_GROUP, nchunk)


def _s5_prompt(ut, mt, st, ot, a1t, a2t, odd):
    bsz, nph, _, nchunk = ut.shape
    n, lw = nph * S5_GROUP, 2 * S5_STATE
    grp = lambda *shape: pl.BlockSpec((1,) + shape, lambda g: (g + odd * S5_GROUPS, 0, 0))
    act = pl.BlockSpec((bsz, nph, S5_GROUP, nchunk), lambda g: (0, 0, g, 0))
    return pl.pallas_call(
        _s5_prompt_kernel,
        grid=(S5_GROUPS,),
        in_specs=[act, grp(n, n), grp(lw, n), grp(n, lw), grp(lw, 8), grp(lw, 8)],
        out_specs=[act, pl.BlockSpec((1, bsz, lw), lambda g: (g, 0, 0))],
        out_shape=[jax.ShapeDtypeStruct(ut.shape, F32), jax.ShapeDtypeStruct((S5_GROUPS, bsz, lw), F32)],
        scratch_shapes=[pltpu.VMEM((bsz * nchunk, lw), F32)],
        compiler_params=_params(("arbitrary",)),
        name="s5_prompt",
    )(ut, mt, st, ot, a1t, a2t)


def _odd_out_kernel(x_ref, oc_ref, yt_ref, gw_ref, gb_ref, wout_ref, xo_ref):
    nph, nchunk = x_ref.shape[1], x_ref.shape[2]
    tm = nph * nchunk
    y = jnp.concatenate([yt_ref[0, k].T for k in range(nph)], axis=0)
    g = jax.nn.gelu(y)
    od = g * jax.nn.sigmoid(_dot(g.astype(BF16), gw_ref[...]) + gb_ref[...])
    out = (x_ref[0].reshape(tm, D_MODEL) + _dot(oc_ref[0].reshape(tm, D_C), wout_ref[0:D_C, :])
           + _dot(od.astype(BF16), wout_ref[D_C:, :]))
    xo_ref[0] = out.reshape(nph, nchunk, D_MODEL)


def _odd_out(x, oc, yt, glu_w, odd, gb, w_out, layer):
    bsz, nph, nchunk, _ = x.shape
    tok = lambda n: pl.BlockSpec((1, PH, nchunk, n), lambda b, s: (b, s, 0, 0))
    return pl.pallas_call(
        _odd_out_kernel,
        grid=(bsz, nph // PH),
        in_specs=[tok(D_MODEL), tok(D_C), pl.BlockSpec((1, PH, D_D, nchunk), lambda b, s: (b, s, 0, 0)),
                  _layer_spec((D_D, D_D), odd), _const_spec((1, D_D)), _layer_spec((D_MODEL, D_MODEL), layer)],
        out_specs=tok(D_MODEL),
        out_shape=jax.ShapeDtypeStruct(x.shape, F32),
        compiler_params=_params(("arbitrary", "arbitrary")),
        name="odd_out",
    )(x, oc, yt, glu_w, gb, w_out)


def _even_sample_kernel(x_ref, g_ref, win_ref, wout_ref, alng_ref, alnb_ref, ws0_ref, bs0_ref,
                        bcw_ref, bcb_ref, blng_ref, blnb_ref, buf_ref, xo_ref, v_ref, nb_ref):
    x = x_ref[...]
    z = _dot(_rms(x, g_ref[...]).astype(BF16), win_ref[...])
    za = jax.nn.gelu(z[:, :2 * D_A])
    vs = []
    for h in range(A_HEADS):
        lo, hi = h * A_HEAD_DIM, (h + 1) * A_HEAD_DIM
        vs.append(_ln(za[:, D_A + lo:D_A + hi], alng_ref[:, lo:hi], alnb_ref[:, lo:hi]))
    v = jnp.concatenate(vs, axis=-1)
    v_ref[...] = v
    gate = ws0_ref[...] * v + bs0_ref[...]
    out_a = za[:, :D_A] * gate
    glu = z[:, 2 * D_A:2 * D_A + D_B] * jax.nn.sigmoid(z[:, 2 * D_A + D_B:])
    nk = B_CONV_WIDTH - 1
    buf = buf_ref[...]
    acc = bcb_ref[...] + bcw_ref[nk:nk + 1, :] * glu + jnp.sum(buf * bcw_ref[0:nk, :][None], axis=1)
    nb_ref[:, 0:nk - 1, :] = buf[:, 1:nk, :]
    nb_ref[:, nk - 1:nk, :] = glu[:, None, :]
    y = _ln(acc, blng_ref[...], blnb_ref[...])
    out_b = y * jax.nn.sigmoid(y)
    cat = jnp.concatenate([out_a, out_b], axis=-1).astype(BF16)
    xo_ref[...] = x + _dot(cat, wout_ref[...])


def _whole(shape):
    nd = len(shape)
    return pl.BlockSpec(shape, lambda *_: (0,) * nd)


def _call_whole(kernel_fn, name, args, out_shapes, scratch=(), specs=None):
    specs = specs or {}
    return pl.pallas_call(
        kernel_fn,
        grid=(1,),
        in_specs=[specs.get(i) or _const_spec(a.shape) for i, a in enumerate(args)],
        out_specs=[_whole(s.shape) for s in out_shapes],
        out_shape=out_shapes,
        scratch_shapes=list(scratch),
        compiler_params=_params(("arbitrary",)),
        name=name,
    )(*args)


def _odd_sample_kernel(x_ref, g_ref, wina_ref, wut_ref, wout_ref, ccw_ref, buf_ref, h_ref, p1_ref, p2_ref,
                       bbar_ref, ccn_ref, dsk_ref, gw_ref, gb_ref,
                       xo_ref, nb_ref, nh_ref, bd_ref, cd_ref):
    lw = 2 * S5_STATE
    bd_ref[...] = jnp.zeros_like(bd_ref)
    cd_ref[...] = jnp.zeros_like(cd_ref)
    for gi in range(S5_GROUPS):
        r0, l0 = gi * S5_GROUP, gi * lw
        bd_ref[r0:r0 + S5_GROUP, l0:l0 + lw] = bbar_ref[gi]
        cd_ref[r0:r0 + S5_GROUP, l0:l0 + lw] = ccn_ref[gi]
    x = x_ref[...]
    xn = _rms(x, g_ref[...]).astype(BF16)
    z = _dot(xn, wina_ref[...])
    u = _dot_nt(xn, wut_ref[...])
    cin = z[:, 2 * D_C:] * z[:, :D_C]
    buf = buf_ref[...]
    conv = jnp.sum(buf * ccw_ref[0:2, :][None], axis=1) + ccw_ref[2:3, :] * cin
    nb_ref[:, 0:1, :] = buf[:, 1:2, :]
    nb_ref[:, 1:2, :] = cin[:, None, :]
    out_c = z[:, D_C:2 * D_C] * conv
    h = h_ref[...]
    hswap = jnp.concatenate([pltpu.roll(h[:, gi * lw:(gi + 1) * lw], S5_STATE, 1) for gi in range(S5_GROUPS)],
                            axis=-1)
    nh = p1_ref[...] * h + p2_ref[...] * hswap + _dot(u.astype(BF16), bd_ref[...])
    nh_ref[...] = nh
    y = _dot_nt(nh.astype(BF16), cd_ref[...]) + dsk_ref[...] * u
    g = jax.nn.gelu(y)
    od = g * jax.nn.sigmoid(_dot(g.astype(BF16), gw_ref[...]) + gb_ref[...])
    cat = jnp.concatenate([out_c, od], axis=-1).astype(BF16)
    xo_ref[...] = x + _dot(cat, wout_ref[...])


def _ffn_sample_kernel(x_ref, g_ref, wg_ref, wu_ref, wd_ref, cw_ref, b_ref, gf_ref,
                       xo_ref, n_ref, xn_ref, acc_ref, *, final):
    j = pl.program_id(0)

    @pl.when(j == 0)
    def _():
        xn_ref[...] = _rms(x_ref[...], g_ref[...]).astype(BF16)
        acc_ref[...] = jnp.zeros_like(acc_ref)

    zg = _dot(xn_ref[...], wg_ref[...])
    zu = _dot(xn_ref[...], wu_ref[...])
    buf = b_ref[...]
    gc = jnp.sum(buf * cw_ref[0:2, :][None], axis=1) + cw_ref[2:3, :] * zg
    n_ref[:, 0:1, :] = buf[:, 1:2, :]
    n_ref[:, 1:2, :] = zg[:, None, :]
    acc_ref[...] += _dot((gc * jax.nn.sigmoid(gc) * zu).astype(BF16), wd_ref[...])

    @pl.when(j == pl.num_programs(0) - 1)
    def _():
        y = x_ref[...] + acc_ref[...]
        if final:
            y = _rms(y, gf_ref[...])
        xo_ref[...] = y


def _ffn_sample(x, g, w_in, w_down, layer, cw, buf, gf, final):
    n = x.shape[0]
    nc = D_FF // FS_CW
    state = pl.BlockSpec((n, 2, FS_CW), lambda j: (0, 0, j))
    return pl.pallas_call(
        functools.partial(_ffn_sample_kernel, final=final),
        grid=(nc,),
        in_specs=[_whole((n, D_MODEL)), _whole((1, D_MODEL)),
                  pl.BlockSpec((None, D_MODEL, FS_CW), lambda j: (layer, 0, j)),
                  pl.BlockSpec((None, D_MODEL, FS_CW), lambda j: (layer, 0, j + nc)),
                  pl.BlockSpec((None, FS_CW, D_MODEL), lambda j: (layer, j, 0)),
                  pl.BlockSpec((3, FS_CW), lambda j: (0, j)), state, _whole((1, D_MODEL))],
        out_specs=[_whole((n, D_MODEL)), state],
        out_shape=[jax.ShapeDtypeStruct((n, D_MODEL), F32), jax.ShapeDtypeStruct((n, 2, D_FF), F32)],
        scratch_shapes=[pltpu.VMEM((n, D_MODEL), BF16), pltpu.VMEM((n, D_MODEL), F32)],
        compiler_params=_params(("arbitrary",)),
        name="ffn_sample",
    )(x, g, w_in, w_in, w_down, cw, buf, gf)


_S5_POW = np.concatenate([np.arange(1 - S5_CHUNK, S5_CHUNK + 1), S5_CHUNK * 2 ** np.arange(8)]).astype(np.float32)
_S5_POW_ROWS = len(_S5_POW)


def _s5_prep_kernel(pw_ref, *refs):
    for gi in range(S5_PREP_GB):
        _s5_prep_group(pw_ref, *[r.at[pl.ds(gi, 1)] for r in refs])


def _s5_prep_group(pw_ref, lr_ref, li_ref, ldt_ref, bb_ref, bbs_ref, cc_ref, ccs_ref, dd_ref,
                   mt_ref, st_ref, ot_ref, a1_ref, a2_ref, bbar_ref, ccn_ref, p1_ref, p2_ref):
    lw, lc = 2 * S5_STATE, S5_CHUNK
    lane = lax.broadcasted_iota(jnp.int32, (1, lw), 1)
    sgn = jnp.where(lane < S5_STATE, -1.0, 1.0)
    lr, li = lr_ref[0], li_ref[0]
    dt = jnp.exp(ldt_ref[0])
    er, ei = lr * dt, li * dt
    jm = pw_ref[...]
    mag = jnp.exp(jm * er)
    cr, ci = mag * jnp.cos(jm * ei), mag * jnp.sin(jm * ei)

    def scale(j, x, xs):
        i = j + lc - 1
        return cr[i:i + 1] * x + (sgn * ci[i:i + 1]) * xs

    one = lc
    nr_, ni_ = cr[one:one + 1] - 1.0, ci[one:one + 1]
    den = lr * lr + li * li
    cfr = (nr_ * lr + ni_ * li) / den
    cfi = (ni_ * lr - nr_ * li) / den
    bb, bbs, cc, ccs = bb_ref[0], bbs_ref[0], cc_ref[0], ccs_ref[0]
    bbar = cfr * bb + (sgn * cfi) * bbs
    bbars = cfr * bbs - (sgn * cfi) * bb
    cat = lambda blocks: jnp.concatenate(blocks, axis=0)
    ymat = cat([scale(t, cc, ccs) for t in range(lc)])
    xmat = cat([scale(-t, bbar, bbars) * (-sgn) for t in range(lc)])
    mt = _dot_nt(ymat, xmat, precision=lax.Precision.HIGHEST)
    n = lc * S5_GROUP
    row = lax.broadcasted_iota(jnp.int32, (n, n), 0)
    col = lax.broadcasted_iota(jnp.int32, (n, n), 1)
    mt = jnp.where((row >> 4) >= (col >> 4), mt, 0.0)
    mt = mt + jnp.where(row == col, dd_ref[0], 0.0)
    mt_ref[0] = mt.astype(BF16)
    smat = cat([scale(lc - 1 - t, bbar, bbars) for t in range(lc)])
    st_ref[0] = smat.T.astype(BF16)
    ot_ref[0] = cat([scale(t + 1, cc, ccs) * (-sgn) for t in range(lc)]).astype(BF16)
    a1_ref[0] = cr[2 * lc:2 * lc + 8].T
    a2_ref[0] = (sgn * ci[2 * lc:2 * lc + 8]).T
    bbar_ref[0] = bbar.astype(BF16)
    ccn_ref[0] = (cc * (-sgn)).astype(BF16)
    p1_ref[0] = cr[one:one + 1]
    p2_ref[0] = sgn * ci[one:one + 1]


def _s5_prepare(lam_re, lam_im, log_dt, b_re, b_im, c_re, c_im, d_skip):
    flat = lambda a: a.reshape((-1,) + a.shape[2:])
    lam_re, lam_im, log_dt, b_re, b_im, c_re, c_im, d_skip = map(
        flat, (lam_re, lam_im, log_dt, b_re, b_im, c_re, c_im, d_skip))
    ng, lw, n = lam_re.shape[0], 2 * S5_STATE, S5_CHUNK * S5_GROUP
    two = lambda a: jnp.concatenate([a, a], axis=-1)[:, None, :]
    pack = lambda a, b: jnp.concatenate([a, b], axis=-1)
    bt_re, bt_im = b_re.transpose(0, 2, 1), b_im.transpose(0, 2, 1)
    args = (jnp.asarray(np.tile(_S5_POW[:, None], (1, lw))), two(lam_re), two(lam_im),
            jnp.broadcast_to(log_dt[:, None, None], (ng, 1, lw)),
            pack(bt_re, bt_im), pack(bt_im, bt_re), pack(c_re, c_im), pack(c_im, c_re),
            jnp.tile(d_skip, (1, S5_CHUNK))[:, None, :])
    blk = lambda *shape: pl.BlockSpec((S5_PREP_GB,) + shape, lambda s: (s, 0, 0))
    shapes = [((n, n), BF16), ((lw, n), BF16), ((n, lw), BF16), ((lw, 8), F32), ((lw, 8), F32),
              ((S5_GROUP, lw), BF16), ((S5_GROUP, lw), BF16), ((1, lw), F32), ((1, lw), F32)]
    outs = pl.pallas_call(
        _s5_prep_kernel,
        grid=(ng // S5_PREP_GB,),
        in_specs=[_const_spec((_S5_POW_ROWS, lw))] + [blk(*a.shape[1:]) for a in args[1:]],
        out_specs=[blk(*sh) for sh, _ in shapes],
        out_shape=[jax.ShapeDtypeStruct((ng,) + sh, dtp) for sh, dtp in shapes],
        compiler_params=_params(("arbitrary",)),
        name="s5_prep",
    )(*args)
    return dict(zip(("mt", "st", "ot", "a1t", "a2t", "bbar", "ccn", "p1", "p2"), outs))


def _cast_kernel(w_ref, o_ref, *, transpose):
    w = w_ref[...]
    o_ref[...] = (w.T if transpose else w).astype(BF16)


def _to_bf16_all(w, tk):
    nl, k, n = w.shape
    spec = pl.BlockSpec((None, tk, n), lambda l, i: (l, i, 0))
    return pl.pallas_call(
        functools.partial(_cast_kernel, transpose=False),
        grid=(nl, k // tk),
        in_specs=[spec],
        out_specs=spec,
        out_shape=jax.ShapeDtypeStruct(w.shape, BF16),
        compiler_params=_params(("arbitrary", "arbitrary")),
        name="to_bf16_all",
    )(w)


def _to_bf16(w, layer, col_block=0, ncols=None, transpose=False):
    _, k, n = w.shape
    ncols = ncols or n
    tk = 256 if k % 256 == 0 else k
    out_block, out_map, out_shape = ((ncols, tk), lambda i: (0, i), (ncols, k)) if transpose else \
        ((tk, ncols), lambda i: (i, 0), (k, ncols))
    return pl.pallas_call(
        functools.partial(_cast_kernel, transpose=transpose),
        grid=(k // tk,),
        in_specs=[pl.BlockSpec((None, tk, ncols), lambda i: (layer, i, col_block))],
        out_specs=pl.BlockSpec(out_block, out_map),
        out_shape=jax.ShapeDtypeStruct(out_shape, BF16),
        compiler_params=_params(("arbitrary",)),
        name="to_bf16",
    )(w)


def kernel(x_prompt, x_sample, state_conv_b, state_conv_c, state_ssm_re, state_ssm_im, state_ffn_conv, norm_mix, norm_ffn, norm_final, w_mix_in, w_mix_out, a_ln_g, a_ln_b, a_ws, a_bs, b_conv_w, b_conv_b, b_ln_g, b_ln_b, c_conv_w, s5_lam_re, s5_lam_im, s5_log_dt, s5_b_re, s5_b_im, s5_c_re, s5_c_im, s5_d, s5_glu_w, s5_glu_b, ffn_w_in, ffn_conv_w, ffn_w_down):
    bsz, seq, _ = x_prompt.shape
    nsmp = x_sample.shape[0]
    nph, nchunk = S5_CHUNK, seq // S5_CHUNK
    n_odd = s5_lam_re.shape[0]
    row = lambda a: a.reshape(1, -1)
    xp = x_prompt.reshape(bsz, nchunk, nph, D_MODEL)
    xs = x_sample.reshape(nsmp, D_MODEL)
    gf = row(norm_final)
    v_rows, cb_p, cb_s, cc_p, cc_s, re_p, re_s, im_p, im_s, fc_p, fc_s = ([] for _ in range(11))
    q = CHUNK // nph
    sds = jax.ShapeDtypeStruct

    w_in_bf = _to_bf16_all(w_mix_in, D_MODEL)
    w_out_bf = _to_bf16_all(w_mix_out, D_MODEL)
    ffn_in_bf = _to_bf16_all(ffn_w_in, D_MODEL // 2)
    ffn_down_bf = _to_bf16_all(ffn_w_down, D_FF // 2)
    glu_bf = _to_bf16_all(s5_glu_w, D_D)
    p = _s5_prepare(s5_lam_re, s5_lam_im, s5_log_dt, s5_b_re, s5_b_im, s5_c_re, s5_c_im, s5_d)
    p1_rows, p2_rows = p["p1"].reshape(n_odd, 1, -1), p["p2"].reshape(n_odd, 1, -1)

    for l in range(DEPTH):
        g_mix = row(norm_mix[l])
        if l % 2 == 0:
            e = l // 2
            alng, alnb = row(a_ln_g[e]), row(a_ln_b[e])
            ws_p = a_ws[e].reshape(A_HEADS, q, nph, q, nph).transpose(0, 2, 1, 4, 3).reshape(A_HEADS, CHUNK, CHUNK)
            bs_full = jnp.repeat(a_bs[e].T, A_HEAD_DIM, axis=1)
            bs_p = bs_full.reshape(q, nph, D_A).transpose(1, 0, 2).reshape(CHUNK, D_A)
            common = (row(b_conv_b[e]), row(b_ln_g[e]), row(b_ln_b[e]))
            xp, cb = _even_prompt(xp, g_mix, w_in_bf, w_out_bf, l, alng, alnb, ws_p, bs_p, b_conv_w[e], *common,
                                  natural_in=(l == 0))
            cb_p.append(cb)
            ws0 = row(jnp.repeat(a_ws[e][:, 0, 0], A_HEAD_DIM))
            bs0 = row(jnp.repeat(a_bs[e][:, 0], A_HEAD_DIM))
            buf = state_conv_b[e]
            xs, v, nb = _call_whole(
                _even_sample_kernel, "even_sample",
                (xs, g_mix, w_in_bf, w_out_bf, alng, alnb, ws0, bs0, b_conv_w[e], *common, buf),
                [sds((nsmp, D_MODEL), F32), sds((nsmp, D_A), F32), sds(buf.shape, F32)],
                specs={2: _layer_spec((D_MODEL, D_IN), l), 3: _layer_spec((D_MODEL, D_MODEL), l)})
            v_rows.append(v.reshape(nsmp, 1, D_A))
            cb_s.append(nb)
        else:
            o = l // 2
            wut = _to_bf16(w_mix_in, l, 3, D_D, transpose=True)
            gb = row(s5_glu_b[o])
            oc, ut, cc = _odd_in(xp, g_mix, w_in_bf, l, wut, c_conv_w[o])
            cc_p.append(cc)
            yt, hs = _s5_prompt(ut, p["mt"], p["st"], p["ot"], p["a1t"], p["a2t"], o)
            hs = hs.transpose(1, 0, 2)
            re_p.append(hs[..., :S5_STATE])
            im_p.append(hs[..., S5_STATE:])
            xp = _odd_out(xp, oc, yt, glu_bf, o, gb, w_out_bf, l)
            bufc = state_conv_c[o]
            hin = jnp.concatenate([state_ssm_re[o], state_ssm_im[o]], axis=-1).reshape(nsmp, -1)
            nstate = S5_GROUPS * 2 * S5_STATE
            grp_spec = pl.BlockSpec((S5_GROUPS, S5_GROUP, 2 * S5_STATE), lambda *_: (o, 0, 0),
                                    pipeline_mode=pl.Buffered(1))
            xs, nbc, nh = _call_whole(
                _odd_sample_kernel, "odd_sample",
                (xs, g_mix, w_in_bf, wut, w_out_bf, c_conv_w[o], bufc, hin, p1_rows[o], p2_rows[o],
                 p["bbar"], p["ccn"], row(s5_d[o]), glu_bf, gb),
                [sds((nsmp, D_MODEL), F32), sds(bufc.shape, F32), sds(hin.shape, F32)],
                scratch=[pltpu.VMEM((D_D, nstate), BF16), pltpu.VMEM((D_D, nstate), BF16)],
                specs={2: _layer_spec((D_MODEL, 3 * D_C), l), 4: _layer_spec((D_MODEL, D_MODEL), l),
                       10: grp_spec, 11: grp_spec, 13: _layer_spec((D_D, D_D), o)})
            cc_s.append(nbc)
            nh = nh.reshape(nsmp, S5_GROUPS, 2 * S5_STATE)
            re_s.append(nh[..., :S5_STATE])
            im_s.append(nh[..., S5_STATE:])
        final = l == DEPTH - 1
        g_ffn = row(norm_ffn[l])
        xp, fc = _ffn_prompt(xp, g_ffn, ffn_in_bf, ffn_down_bf, l, ffn_conv_w[l], gf, final)
        fc_p.append(fc)
        xs, nfc = _ffn_sample(xs, g_ffn, ffn_in_bf, ffn_down_bf, l, ffn_conv_w[l], state_ffn_conv[l], gf, final)
        fc_s.append(nfc)

    st = jnp.stack
    return (xp.reshape(bsz, seq, D_MODEL), xs.reshape(nsmp, 1, D_MODEL), st(v_rows), st(cb_p), st(cb_s),
            st(cc_p), st(cc_s), st(re_p), st(re_s), st(im_p), st(im_s), st(fc_p), st(fc_s))
```

```python
import functools

import numpy as np
import jax
import jax.numpy as jnp
from jax import lax
from jax.experimental import pallas as pl
from jax.experimental.pallas import tpu as pltpu

D_MODEL = 1024
DEPTH = 4
D_A = 512
D_B = 512
D_C = 512
D_D = 512
D_IN = 2048
A_HEADS = 4
A_HEAD_DIM = 128
CHUNK = 128
B_CONV_WIDTH = 31
S5_GROUP = 16
S5_GROUPS = 32
S5_STATE = 64
D_FF = 2816
EPS = 1e-6

S5_CHUNK = 16
S5_LEVELS = 7
S5_PREP_GB = 8
RC = 64
RS = 32
PH = 4
FF_CW = 256
FS_CW = 1408
SUB = 8
VMEM_LIMIT = 56 * 1024 * 1024

F32 = jnp.float32
BF16 = jnp.bfloat16


def _rms(x, g):
    return x * lax.rsqrt(jnp.mean(x * x, axis=-1, keepdims=True) + EPS) * g


def _ln(x, g, b):
    mu = jnp.mean(x, axis=-1, keepdims=True)
    xc = x - mu
    var = jnp.mean(xc * xc, axis=-1, keepdims=True)
    return xc * lax.rsqrt(var + EPS) * g + b


def _dot(a, b):
    return jnp.dot(a, b, preferred_element_type=F32)


def _dot_nt(a, b, precision=None):
    return lax.dot_general(a, b, (((1,), (1,)), ((), ())), precision=precision, preferred_element_type=F32)


def _const_spec(shape):
    nd = len(shape)
    return pl.BlockSpec(shape, lambda *_: (0,) * nd, pipeline_mode=pl.Buffered(1))


def _layer_spec(shape2d, layer, col_block=0):
    return pl.BlockSpec((None,) + tuple(shape2d), lambda *_: (layer, 0, col_block), pipeline_mode=pl.Buffered(1))


def _params(sem):
    return pltpu.CompilerParams(dimension_semantics=sem, vmem_limit_bytes=VMEM_LIMIT)


def _shift_rows(blk, first_row):
    row0 = lax.broadcasted_iota(jnp.int32, blk.shape, 0) == 0
    return jnp.where(row0, first_row, pltpu.roll(blk, 1, 0))


def _chunk_tile_spec(n):
    return pl.BlockSpec((1, S5_CHUNK, RC, n), lambda b, i: (b, 0, i, 0))


def _even_prompt_kernel(x_ref, g_ref, win_ref, wout_ref, alng_ref, alnb_ref, ws_ref, bs_ref,
                        bcw_ref, bcb_ref, blng_ref, blnb_ref, xo_ref, cb_ref,
                        z_ref, p_ref, gsh_ref, oa_ref, ob_ref, wb_ref, *, natural_in):
    i = pl.program_id(1)
    ni = pl.num_programs(1)
    nph, rc = xo_ref.shape[1], xo_ref.shape[2]
    rs = RS
    nsub, tms = rc // rs, nph * rs

    @pl.when(jnp.logical_and(pl.program_id(0) == 0, i == 0))
    def _():
        for k in range(B_CONV_WIDTH):
            wb_ref[k] = jnp.broadcast_to(bcw_ref[k:k + 1, :], (SUB, D_B))

    @pl.when(i == 0)
    def _():
        p_ref[0, :, 0:SUB, :] = jnp.zeros((nph, SUB, D_B), F32)

    def load(s):
        xt = jnp.swapaxes(x_ref[0, s * rs:(s + 1) * rs], 0, 1) if natural_in else x_ref[0, :, s * rs:(s + 1) * rs, :]
        return xt.reshape(tms, D_MODEL)

    xs = [load(s) for s in range(nsub)]
    for s in range(nsub):
        z_ref[s] = _dot(_rms(xs[s], g_ref[...]).astype(BF16), win_ref[...])

    q = CHUNK // nph
    r = lax.broadcasted_iota(jnp.int32, (CHUNK, CHUNK), 0)
    c = lax.broadcasted_iota(jnp.int32, (CHUNK, CHUNK), 1)
    pos = lambda k: (k & (q - 1)) * nph + (k >> (q.bit_length() - 1))
    keep = pos(r) >= pos(c)
    wm = [jnp.where(keep, ws_ref[h], 0.0).astype(BF16) for h in range(A_HEADS)]

    for s in range(nsub):
        if s > 0:
            p_ref[s, :, SUB - 2:SUB, :] = p_ref[s - 1, :, SUB + rs - 2:SUB + rs, :]
        for m in range(rs // q):
            rows = [ph * rs + q * m for ph in range(nph)]
            za = jax.nn.gelu(jnp.concatenate([z_ref[s, r0:r0 + q, 0:2 * D_A] for r0 in rows], axis=0))
            for h in range(A_HEADS):
                lo, hi = h * A_HEAD_DIM, (h + 1) * A_HEAD_DIM
                vh = _ln(za[:, D_A + lo:D_A + hi], alng_ref[:, lo:hi], alnb_ref[:, lo:hi])
                oa = za[:, lo:hi] * (_dot(wm[h], vh.astype(BF16)) + bs_ref[:, lo:hi])
                for ph, r0 in enumerate(rows):
                    oa_ref[s, r0:r0 + q, lo:hi] = oa[ph * q:(ph + 1) * q]

        for ph in range(nph):
            zb = z_ref[s, ph * rs:(ph + 1) * rs, 2 * D_A:]
            p_ref[s, ph, SUB:SUB + rs, :] = zb[:, :D_B] * jax.nn.sigmoid(zb[:, D_B:])
        for ph in range(nph):
            gsh_ref[s, 0, ph] = p_ref[s, ph, SUB - 1:SUB - 1 + rs, :]
            gsh_ref[s, 1, ph] = p_ref[s, ph, SUB - 2:SUB - 2 + rs, :]
        for ph in range(nph):
            acc = jnp.zeros((rs // SUB, SUB, D_B), F32)
            for j in range(B_CONV_WIDTH):
                src_ph = (ph - j) % nph
                back = (j - ph + nph - 1) // nph if j > ph else 0
                src = p_ref[s, src_ph, SUB:SUB + rs, :] if back == 0 else gsh_ref[s, back - 1, src_ph]
                acc = acc + wb_ref[B_CONV_WIDTH - 1 - j][None] * src.reshape(rs // SUB, SUB, D_B)
            y = _ln(acc.reshape(rs, D_B) + bcb_ref[...], blng_ref[...], blnb_ref[...])
            ob_ref[s, ph * rs:(ph + 1) * rs, :] = (y * jax.nn.sigmoid(y)).astype(BF16)

        out = (xs[s] + _dot(oa_ref[s].astype(BF16), wout_ref[0:D_A, :]) + _dot(ob_ref[s], wout_ref[D_A:, :]))
        xo_ref[0, :, s * rs:(s + 1) * rs, :] = out.reshape(nph, rs, D_MODEL)

    last = nsub - 1

    @pl.when(i == ni - 1)
    def _():
        n_out = B_CONV_WIDTH - 1
        for k in range(n_out):
            back = n_out - 1 - k
            ph, cl = (nph - 1 - back) % nph, rs - 1 - back // nph
            cb_ref[0, k:k + 1, :] = p_ref[last, ph, SUB + cl:SUB + cl + 1, :]

    p_ref[0, :, SUB - 2:SUB, :] = p_ref[last, :, SUB + rs - 2:SUB + rs, :]


def _even_prompt(x, g, w_in, w_out, layer, alng, alnb, ws, bs, bcw, bcb, blng, blnb, natural_in):
    bsz, nph, nchunk = x.shape[0], S5_CHUNK, x.shape[1] * x.shape[2] // S5_CHUNK
    row = lambda n: _const_spec((1, n))
    nsub, tms = RC // RS, nph * RS
    x_spec = pl.BlockSpec((1, RC, nph, D_MODEL), lambda b, i: (b, i, 0, 0)) if natural_in else \
        _chunk_tile_spec(D_MODEL)
    return pl.pallas_call(
        functools.partial(_even_prompt_kernel, natural_in=natural_in),
        grid=(bsz, nchunk // RC),
        in_specs=[x_spec, row(D_MODEL),
                  _layer_spec((D_MODEL, D_IN), layer), _layer_spec((D_MODEL, D_MODEL), layer),
                  row(D_A), row(D_A), _const_spec((A_HEADS, CHUNK, CHUNK)), _const_spec((CHUNK, D_A)),
                  _const_spec((B_CONV_WIDTH, D_B)), row(D_B), row(D_B), row(D_B)],
        out_specs=[_chunk_tile_spec(D_MODEL),
                   pl.BlockSpec((1, B_CONV_WIDTH - 1, D_B), lambda b, i: (b, 0, 0))],
        out_shape=[jax.ShapeDtypeStruct((bsz, nph, nchunk, D_MODEL), F32),
                   jax.ShapeDtypeStruct((bsz, B_CONV_WIDTH - 1, D_B), F32)],
        scratch_shapes=[pltpu.VMEM((nsub, tms, D_IN), F32), pltpu.VMEM((nsub, nph, RS + SUB, D_B), F32),
                        pltpu.VMEM((nsub, 2, nph, RS, D_B), F32), pltpu.VMEM((nsub, tms, D_A), F32),
                        pltpu.VMEM((nsub, tms, D_B), BF16), pltpu.VMEM((B_CONV_WIDTH, SUB, D_B), F32)],
        compiler_params=_params(("arbitrary", "arbitrary")),
        name="even_prompt",
    )(x, g, w_in, w_out, alng, alnb, ws, bs, bcw, bcb, blng, blnb)


def _ffn_prompt_kernel(x_ref, g_ref, wg_ref, wu_ref, wd_ref, cw_ref, gf_ref, xo_ref, fc_ref,
                       xn_ref, carry_ref, h_ref, *, final):
    i = pl.program_id(1)
    ni = pl.num_programs(1)
    nph, rc = x_ref.shape[1], x_ref.shape[2]
    tm = nph * rc

    @pl.when(i == 0)
    def _():
        carry_ref[...] = jnp.zeros_like(carry_ref)

    x = x_ref[0].reshape(tm, D_MODEL)
    xn_ref[...] = _rms(x, g_ref[...]).astype(BF16)
    lo2, lo1 = (nph - 2) * rc, (nph - 1) * rc
    for c0 in range(0, D_FF, FF_CW):
        cs = slice(c0, c0 + FF_CW)
        zg = _dot(xn_ref[...], wg_ref[:, cs])
        zu = _dot(xn_ref[...], wu_ref[:, cs])
        s2 = _shift_rows(zg[lo2:lo1], carry_ref[0:1, cs])
        s1 = _shift_rows(zg[lo1:], carry_ref[1:2, cs])
        carry_ref[0:1, cs] = zg[lo1 - 1:lo1]
        carry_ref[1:2, cs] = zg[tm - 1:tm]
        z1 = jnp.concatenate([s1, zg[:lo1]], axis=0)
        z2 = jnp.concatenate([s2, s1, zg[:lo2]], axis=0)
        gc = cw_ref[0:1, cs] * z2 + cw_ref[1:2, cs] * z1 + cw_ref[2:3, cs] * zg
        h_ref[:, cs] = (gc * jax.nn.sigmoid(gc) * zu).astype(BF16)

    @pl.when(i == ni - 1)
    def _():
        fc_ref[0] = carry_ref[...]

    y = x + _dot(h_ref[...], wd_ref[...])
    if final:
        xo_ref[0] = jnp.swapaxes(_rms(y, gf_ref[...]).reshape(nph, rc, D_MODEL), 0, 1)
    else:
        xo_ref[0] = y.reshape(nph, rc, D_MODEL)


def _ffn_prompt(x, g, w_in, w_down, layer, cw, gf, final):
    bsz, nph, nchunk, _ = x.shape
    tm = nph * RC
    out_spec, out_shape = (pl.BlockSpec((1, RC, nph, D_MODEL), lambda b, i: (b, i, 0, 0)),
                           (bsz, nchunk, nph, D_MODEL)) if final else (_chunk_tile_spec(D_MODEL), x.shape)
    return pl.pallas_call(
        functools.partial(_ffn_prompt_kernel, final=final),
        grid=(bsz, nchunk // RC),
        in_specs=[_chunk_tile_spec(D_MODEL), _const_spec((1, D_MODEL)),
                  _layer_spec((D_MODEL, D_FF), layer, 0), _layer_spec((D_MODEL, D_FF), layer, 1),
                  _layer_spec((D_FF, D_MODEL), layer), _const_spec((3, D_FF)), _const_spec((1, D_MODEL))],
        out_specs=[out_spec, pl.BlockSpec((1, 2, D_FF), lambda b, i: (b, 0, 0))],
        out_shape=[jax.ShapeDtypeStruct(out_shape, F32), jax.ShapeDtypeStruct((bsz, 2, D_FF), F32)],
        scratch_shapes=[pltpu.VMEM((tm, D_MODEL), BF16), pltpu.VMEM((2, D_FF), F32),
                        pltpu.VMEM((tm, D_FF), BF16)],
        compiler_params=_params(("arbitrary", "arbitrary")),
        name="ffn_prompt",
    )(x, g, w_in, w_in, w_down, cw, gf)


def _odd_in_kernel(x_ref, g_ref, win_ref, wut_ref, ccw_ref, oc_ref, ut_ref, cc_ref, cin_ref, gb_ref):
    nph, nchunk = x_ref.shape[1], x_ref.shape[2]
    for s in range(nph // PH):
        rows = slice(s * PH * nchunk, (s + 1) * PH * nchunk)
        xn = _rms(x_ref[0, s * PH:(s + 1) * PH].reshape(PH * nchunk, D_MODEL), g_ref[...]).astype(BF16)
        z = _dot(xn, win_ref[...])
        cin_ref[rows, :] = z[:, 2 * D_C:] * z[:, :D_C]
        gb_ref[rows, :] = z[:, D_C:2 * D_C]
        ut = _dot_nt(wut_ref[...], xn)
        for k in range(PH):
            ut_ref[0, s * PH + k] = ut[:, k * nchunk:(k + 1) * nchunk].astype(BF16)

    zero = jnp.zeros((1, D_C), F32)
    blk = lambda ph: cin_ref[ph * nchunk:(ph + 1) * nchunk, :]
    s2 = _shift_rows(blk(nph - 2), zero)
    s1 = _shift_rows(blk(nph - 1), zero)
    for ph in range(nph):
        p1 = blk(ph - 1) if ph >= 1 else s1
        p2 = blk(ph - 2) if ph >= 2 else (s1 if ph == 1 else s2)
        conv = ccw_ref[0:1, :] * p2 + ccw_ref[1:2, :] * p1 + ccw_ref[2:3, :] * blk(ph)
        oc_ref[0, ph] = (gb_ref[ph * nchunk:(ph + 1) * nchunk, :] * conv).astype(BF16)
    last = nchunk - 1
    cc_ref[0, 0:1, :] = cin_ref[(nph - 2) * nchunk + last:(nph - 2) * nchunk + last + 1, :]
    cc_ref[0, 1:2, :] = cin_ref[(nph - 1) * nchunk + last:(nph - 1) * nchunk + last + 1, :]


def _odd_in(x, g, w_in, layer, wut, ccw):
    bsz, nph, nchunk, _ = x.shape
    seq = lambda *shape: pl.BlockSpec((1,) + shape, lambda b: (b,) + (0,) * len(shape))
    return pl.pallas_call(
        _odd_in_kernel,
        grid=(bsz,),
        in_specs=[seq(nph, nchunk, D_MODEL), _const_spec((1, D_MODEL)), _layer_spec((D_MODEL, 3 * D_C), layer),
                  _const_spec((D_D, D_MODEL)), _const_spec((3, D_C))],
        out_specs=[seq(nph, nchunk, D_C), seq(nph, D_D, nchunk), seq(2, D_C)],
        out_shape=[jax.ShapeDtypeStruct((bsz, nph, nchunk, D_C), BF16),
                   jax.ShapeDtypeStruct((bsz, nph, D_D, nchunk), BF16),
                   jax.ShapeDtypeStruct((bsz, 2, D_C), F32)],
        scratch_shapes=[pltpu.VMEM((nph * nchunk, D_C), F32), pltpu.VMEM((nph * nchunk, D_C), F32)],
        compiler_params=_params(("arbitrary",)),
        name="odd_in",
    )(x, g, w_in, wut, ccw)


def _s5_prompt_kernel(ut_ref, mt_ref, st_ref, ot_ref, a1_ref, a2_ref, yt_ref, hs_ref, h_ref):
    bsz, nph = ut_ref.shape[0], ut_ref.shape[1]
    nchunk = ut_ref.shape[3]
    n = nph * S5_GROUP
    ut = jnp.concatenate([ut_ref[b].reshape(n, nchunk) for b in range(bsz)], axis=-1)
    x = _dot(st_ref[0], ut)
    cidx = lax.broadcasted_iota(jnp.int32, x.shape, 1) & (nchunk - 1)
    a1, a2 = a1_ref[0], a2_ref[0]
    swap = lambda v: jnp.concatenate([v[S5_STATE:], v[:S5_STATE]], axis=0)

    def shift(v, sh):
        rolled = jnp.concatenate([pltpu.roll(v[:, b * nchunk:(b + 1) * nchunk], sh, 1) for b in range(bsz)], axis=1)
        return jnp.where(cidx >= sh, rolled, 0.0)

    for k in range(S5_LEVELS):
        xs = shift(x, 1 << k)
        x = x + a1[:, k:k + 1] * xs + a2[:, k:k + 1] * swap(xs)
    h_ref[...] = x.T
    hs_ref[0] = h_ref[pl.ds(nchunk - 1, bsz, stride=nchunk), :]
    hprev = shift(x, 1)
    yt = _dot(mt_ref[0], ut) + _dot(ot_ref[0], hprev.astype(BF16))
    for b in range(bsz):
        yt_ref[b] = yt[:, b * nchunk:(b + 1) * nchunk].reshape(nph, S5_GROUP, nchunk)


def _s5_prompt(ut, mt, st, ot, a1t, a2t, odd):
    bsz, nph, _, nchunk = ut.shape
    n, lw = nph * S5_GROUP, 2 * S5_STATE
    grp = lambda *shape: pl.BlockSpec((1,) + shape, lambda g: (g + odd * S5_GROUPS, 0, 0))
    act = pl.BlockSpec((bsz, nph, S5_GROUP, nchunk), lambda g: (0, 0, g, 0))
    return pl.pallas_call(
        _s5_prompt_kernel,
        grid=(S5_GROUPS,),
        in_specs=[act, grp(n, n), grp(lw, n), grp(n, lw), grp(lw, 8), grp(lw, 8)],
        out_specs=[act, pl.BlockSpec((1, bsz, lw), lambda g: (g, 0, 0))],
        out_shape=[jax.ShapeDtypeStruct(ut.shape, F32), jax.ShapeDtypeStruct((S5_GROUPS, bsz, lw), F32)],
        scratch_shapes=[pltpu.VMEM((bsz * nchunk, lw), F32)],
        compiler_params=_params(("arbitrary",)),
        name="s5_prompt",
    )(ut, mt, st, ot, a1t, a2t)


def _odd_out_kernel(x_ref, oc_ref, yt_ref, gw_ref, gb_ref, wout_ref, xo_ref):
    nph, nchunk = x_ref.shape[1], x_ref.shape[2]
    tm = nph * nchunk
    y = jnp.concatenate([yt_ref[0, k].T for k in range(nph)], axis=0)
    g = jax.nn.gelu(y)
    od = g * jax.nn.sigmoid(_dot(g.astype(BF16), gw_ref[...]) + gb_ref[...])
    out = (x_ref[0].reshape(tm, D_MODEL) + _dot(oc_ref[0].reshape(tm, D_C), wout_ref[0:D_C, :])
           + _dot(od.astype(BF16), wout_ref[D_C:, :]))
    xo_ref[0] = out.reshape(nph, nchunk, D_MODEL)


def _odd_out(x, oc, yt, glu_w, odd, gb, w_out, layer):
    bsz, nph, nchunk, _ = x.shape
    tok = lambda n: pl.BlockSpec((1, PH, nchunk, n), lambda b, s: (b, s, 0, 0))
    return pl.pallas_call(
        _odd_out_kernel,
        grid=(bsz, nph // PH),
        in_specs=[tok(D_MODEL), tok(D_C), pl.BlockSpec((1, PH, D_D, nchunk), lambda b, s: (b, s, 0, 0)),
                  _layer_spec((D_D, D_D), odd), _const_spec((1, D_D)), _layer_spec((D_MODEL, D_MODEL), layer)],
        out_specs=tok(D_MODEL),
        out_shape=jax.ShapeDtypeStruct(x.shape, F32),
        compiler_params=_params(("arbitrary", "arbitrary")),
        name="odd_out",
    )(x, oc, yt, glu_w, gb, w_out)


def _even_sample_kernel(x_ref, g_ref, win_ref, wout_ref, alng_ref, alnb_ref, ws0_ref, bs0_ref,
                        bcw_ref, bcb_ref, blng_ref, blnb_ref, buf_ref, xo_ref, v_ref, nb_ref):
    x = x_ref[...]
    z = _dot(_rms(x, g_ref[...]).astype(BF16), win_ref[...])
    za = jax.nn.gelu(z[:, :2 * D_A])
    vs = []
    for h in range(A_HEADS):
        lo, hi = h * A_HEAD_DIM, (h + 1) * A_HEAD_DIM
        vs.append(_ln(za[:, D_A + lo:D_A + hi], alng_ref[:, lo:hi], alnb_ref[:, lo:hi]))
    v = jnp.concatenate(vs, axis=-1)
    v_ref[...] = v
    gate = ws0_ref[...] * v + bs0_ref[...]
    out_a = za[:, :D_A] * gate
    glu = z[:, 2 * D_A:2 * D_A + D_B] * jax.nn.sigmoid(z[:, 2 * D_A + D_B:])
    nk = B_CONV_WIDTH - 1
    buf = buf_ref[...]
    acc = bcb_ref[...] + bcw_ref[nk:nk + 1, :] * glu + jnp.sum(buf * bcw_ref[0:nk, :][None], axis=1)
    nb_ref[:, 0:nk - 1, :] = buf[:, 1:nk, :]
    nb_ref[:, nk - 1:nk, :] = glu[:, None, :]
    y = _ln(acc, blng_ref[...], blnb_ref[...])
    out_b = y * jax.nn.sigmoid(y)
    cat = jnp.concatenate([out_a, out_b], axis=-1).astype(BF16)
    xo_ref[...] = x + _dot(cat, wout_ref[...])


def _whole(shape):
    nd = len(shape)
    return pl.BlockSpec(shape, lambda *_: (0,) * nd)


def _call_whole(kernel_fn, name, args, out_shapes, scratch=(), specs=None):
    specs = specs or {}
    return pl.pallas_call(
        kernel_fn,
        grid=(1,),
        in_specs=[specs.get(i) or _const_spec(a.shape) for i, a in enumerate(args)],
        out_specs=[_whole(s.shape) for s in out_shapes],
        out_shape=out_shapes,
        scratch_shapes=list(scratch),
        compiler_params=_params(("arbitrary",)),
        name=name,
    )(*args)


def _odd_sample_kernel(x_ref, g_ref, wina_ref, wut_ref, wout_ref, ccw_ref, buf_ref, h_ref, p1_ref, p2_ref,
                       bbar_ref, ccn_ref, dsk_ref, gw_ref, gb_ref,
                       xo_ref, nb_ref, nh_ref, bd_ref, cd_ref):
    lw = 2 * S5_STATE
    bd_ref[...] = jnp.zeros_like(bd_ref)
    cd_ref[...] = jnp.zeros_like(cd_ref)
    for gi in range(S5_GROUPS):
        r0, l0 = gi * S5_GROUP, gi * lw
        bd_ref[r0:r0 + S5_GROUP, l0:l0 + lw] = bbar_ref[gi]
        cd_ref[r0:r0 + S5_GROUP, l0:l0 + lw] = ccn_ref[gi]
    x = x_ref[...]
    xn = _rms(x, g_ref[...]).astype(BF16)
    z = _dot(xn, wina_ref[...])
    u = _dot_nt(xn, wut_ref[...])
    cin = z[:, 2 * D_C:] * z[:, :D_C]
    buf = buf_ref[...]
    conv = jnp.sum(buf * ccw_ref[0:2, :][None], axis=1) + ccw_ref[2:3, :] * cin
    nb_ref[:, 0:1, :] = buf[:, 1:2, :]
    nb_ref[:, 1:2, :] = cin[:, None, :]
    out_c = z[:, D_C:2 * D_C] * conv
    h = h_ref[...]
    hswap = jnp.concatenate([pltpu.roll(h[:, gi * lw:(gi + 1) * lw], S5_STATE, 1) for gi in range(S5_GROUPS)],
                            axis=-1)
    nh = p1_ref[...] * h + p2_ref[...] * hswap + _dot(u.astype(BF16), bd_ref[...])
    nh_ref[...] = nh
    y = _dot_nt(nh.astype(BF16), cd_ref[...]) + dsk_ref[...] * u
    g = jax.nn.gelu(y)
    od = g * jax.nn.sigmoid(_dot(g.astype(BF16), gw_ref[...]) + gb_ref[...])
    cat = jnp.concatenate([out_c, od], axis=-1).astype(BF16)
    xo_ref[...] = x + _dot(cat, wout_ref[...])


def _ffn_sample_kernel(x_ref, g_ref, wg_ref, wu_ref, wd_ref, cw_ref, b_ref, gf_ref,
                       xo_ref, n_ref, xn_ref, acc_ref, *, final):
    j = pl.program_id(0)

    @pl.when(j == 0)
    def _():
        xn_ref[...] = _rms(x_ref[...], g_ref[...]).astype(BF16)
        acc_ref[...] = jnp.zeros_like(acc_ref)

    zg = _dot(xn_ref[...], wg_ref[...])
    zu = _dot(xn_ref[...], wu_ref[...])
    buf = b_ref[...]
    gc = jnp.sum(buf * cw_ref[0:2, :][None], axis=1) + cw_ref[2:3, :] * zg
    n_ref[:, 0:1, :] = buf[:, 1:2, :]
    n_ref[:, 1:2, :] = zg[:, None, :]
    acc_ref[...] += _dot((gc * jax.nn.sigmoid(gc) * zu).astype(BF16), wd_ref[...])

    @pl.when(j == pl.num_programs(0) - 1)
    def _():
        y = x_ref[...] + acc_ref[...]
        if final:
            y = _rms(y, gf_ref[...])
        xo_ref[...] = y


def _ffn_sample(x, g, w_in, w_down, layer, cw, buf, gf, final):
    n = x.shape[0]
    nc = D_FF // FS_CW
    state = pl.BlockSpec((n, 2, FS_CW), lambda j: (0, 0, j))
    return pl.pallas_call(
        functools.partial(_ffn_sample_kernel, final=final),
        grid=(nc,),
        in_specs=[_whole((n, D_MODEL)), _whole((1, D_MODEL)),
                  pl.BlockSpec((None, D_MODEL, FS_CW), lambda j: (layer, 0, j)),
                  pl.BlockSpec((None, D_MODEL, FS_CW), lambda j: (layer, 0, j + nc)),
                  pl.BlockSpec((None, FS_CW, D_MODEL), lambda j: (layer, j, 0)),
                  pl.BlockSpec((3, FS_CW), lambda j: (0, j)), state, _whole((1, D_MODEL))],
        out_specs=[_whole((n, D_MODEL)), state],
        out_shape=[jax.ShapeDtypeStruct((n, D_MODEL), F32), jax.ShapeDtypeStruct((n, 2, D_FF), F32)],
        scratch_shapes=[pltpu.VMEM((n, D_MODEL), BF16), pltpu.VMEM((n, D_MODEL), F32)],
        compiler_params=_params(("arbitrary",)),
        name="ffn_sample",
    )(x, g, w_in, w_in, w_down, cw, buf, gf)


_S5_POW = np.concatenate([np.arange(1 - S5_CHUNK, S5_CHUNK + 1), S5_CHUNK * 2 ** np.arange(8)]).astype(np.float32)
_S5_POW_ROWS = len(_S5_POW)


def _s5_prep_kernel(pw_ref, *refs):
    for gi in range(S5_PREP_GB):
        _s5_prep_group(pw_ref, *[r.at[pl.ds(gi, 1)] for r in refs])


def _s5_prep_group(pw_ref, lr_ref, li_ref, ldt_ref, bb_ref, bbs_ref, cc_ref, ccs_ref, dd_ref,
                   mt_ref, st_ref, ot_ref, a1_ref, a2_ref, bbar_ref, ccn_ref, p1_ref, p2_ref):
    lw, lc = 2 * S5_STATE, S5_CHUNK
    lane = lax.broadcasted_iota(jnp.int32, (1, lw), 1)
    sgn = jnp.where(lane < S5_STATE, -1.0, 1.0)
    lr, li = lr_ref[0], li_ref[0]
    dt = jnp.exp(ldt_ref[0])
    er, ei = lr * dt, li * dt
    jm = pw_ref[...]
    mag = jnp.exp(jm * er)
    cr, ci = mag * jnp.cos(jm * ei), mag * jnp.sin(jm * ei)

    def scale(j, x, xs):
        i = j + lc - 1
        return cr[i:i + 1] * x + (sgn * ci[i:i + 1]) * xs

    one = lc
    nr_, ni_ = cr[one:one + 1] - 1.0, ci[one:one + 1]
    den = lr * lr + li * li
    cfr = (nr_ * lr + ni_ * li) / den
    cfi = (ni_ * lr - nr_ * li) / den
    bb, bbs, cc, ccs = bb_ref[0], bbs_ref[0], cc_ref[0], ccs_ref[0]
    bbar = cfr * bb + (sgn * cfi) * bbs
    bbars = cfr * bbs - (sgn * cfi) * bb
    cat = lambda blocks: jnp.concatenate(blocks, axis=0)
    ymat = cat([scale(t, cc, ccs) for t in range(lc)])
    xmat = cat([scale(-t, bbar, bbars) * (-sgn) for t in range(lc)])
    mt = _dot_nt(ymat, xmat, precision=lax.Precision.HIGHEST)
    n = lc * S5_GROUP
    row = lax.broadcasted_iota(jnp.int32, (n, n), 0)
    col = lax.broadcasted_iota(jnp.int32, (n, n), 1)
    mt = jnp.where((row >> 4) >= (col >> 4), mt, 0.0)
    mt = mt + jnp.where(row == col, dd_ref[0], 0.0)
    mt_ref[0] = mt.astype(BF16)
    smat = cat([scale(lc - 1 - t, bbar, bbars) for t in range(lc)])
    st_ref[0] = smat.T.astype(BF16)
    ot_ref[0] = cat([scale(t + 1, cc, ccs) * (-sgn) for t in range(lc)]).astype(BF16)
    a1_ref[0] = cr[2 * lc:2 * lc + 8].T
    a2_ref[0] = (sgn * ci[2 * lc:2 * lc + 8]).T
    bbar_ref[0] = bbar.astype(BF16)
    ccn_ref[0] = (cc * (-sgn)).astype(BF16)
    p1_ref[0] = cr[one:one + 1]
    p2_ref[0] = sgn * ci[one:one + 1]


def _s5_prepare(lam_re, lam_im, log_dt, b_re, b_im, c_re, c_im, d_skip):
    flat = lambda a: a.reshape((-1,) + a.shape[2:])
    lam_re, lam_im, log_dt, b_re, b_im, c_re, c_im, d_skip = map(
        flat, (lam_re, lam_im, log_dt, b_re, b_im, c_re, c_im, d_skip))
    ng, lw, n = lam_re.shape[0], 2 * S5_STATE, S5_CHUNK * S5_GROUP
    two = lambda a: jnp.concatenate([a, a], axis=-1)[:, None, :]
    pack = lambda a, b: jnp.concatenate([a, b], axis=-1)
    bt_re, bt_im = b_re.transpose(0, 2, 1), b_im.transpose(0, 2, 1)
    args = (jnp.asarray(np.tile(_S5_POW[:, None], (1, lw))), two(lam_re), two(lam_im),
            jnp.broadcast_to(log_dt[:, None, None], (ng, 1, lw)),
            pack(bt_re, bt_im), pack(bt_im, bt_re), pack(c_re, c_im), pack(c_im, c_re),
            jnp.tile(d_skip, (1, S5_CHUNK))[:, None, :])
    blk = lambda *shape: pl.BlockSpec((S5_PREP_GB,) + shape, lambda s: (s, 0, 0))
    shapes = [((n, n), BF16), ((lw, n), BF16), ((n, lw), BF16), ((lw, 8), F32), ((lw, 8), F32),
              ((S5_GROUP, lw), BF16), ((S5_GROUP, lw), BF16), ((1, lw), F32), ((1, lw), F32)]
    outs = pl.pallas_call(
        _s5_prep_kernel,
        grid=(ng // S5_PREP_GB,),
        in_specs=[_const_spec((_S5_POW_ROWS, lw))] + [blk(*a.shape[1:]) for a in args[1:]],
        out_specs=[blk(*sh) for sh, _ in shapes],
        out_shape=[jax.ShapeDtypeStruct((ng,) + sh, dtp) for sh, dtp in shapes],
        compiler_params=_params(("arbitrary",)),
        name="s5_prep",
    )(*args)
    return dict(zip(("mt", "st", "ot", "a1t", "a2t", "bbar", "ccn", "p1", "p2"), outs))


def _cast_kernel(w_ref, o_ref, *, transpose):
    w = w_ref[...]
    o_ref[...] = (w.T if transpose else w).astype(BF16)


def _to_bf16_all(w, tk):
    nl, k, n = w.shape
    spec = pl.BlockSpec((None, tk, n), lambda l, i: (l, i, 0))
    return pl.pallas_call(
        functools.partial(_cast_kernel, transpose=False),
        grid=(nl, k // tk),
        in_specs=[spec],
        out_specs=spec,
        out_shape=jax.ShapeDtypeStruct(w.shape, BF16),
        compiler_params=_params(("arbitrary", "arbitrary")),
        name="to_bf16_all",
    )(w)


def _to_bf16(w, layer, col_block=0, ncols=None, transpose=False):
    _, k, n = w.shape
    ncols = ncols or n
    tk = 256 if k % 256 == 0 else k
    out_block, out_map, out_shape = ((ncols, tk), lambda i: (0, i), (ncols, k)) if transpose else \
        ((tk, ncols), lambda i: (i, 0), (k, ncols))
    return pl.pallas_call(
        functools.partial(_cast_kernel, transpose=transpose),
        grid=(k // tk,),
        in_specs=[pl.BlockSpec((None, tk, ncols), lambda i: (layer, i, col_block))],
        out_specs=pl.BlockSpec(out_block, out_map),
        out_shape=jax.ShapeDtypeStruct(out_shape, BF16),
        compiler_params=_params(("arbitrary",)),
        name="to_bf16",
    )(w)


def kernel(x_prompt, x_sample, state_conv_b, state_conv_c, state_ssm_re, state_ssm_im, state_ffn_conv, norm_mix, norm_ffn, norm_final, w_mix_in, w_mix_out, a_ln_g, a_ln_b, a_ws, a_bs, b_conv_w, b_conv_b, b_ln_g, b_ln_b, c_conv_w, s5_lam_re, s5_lam_im, s5_log_dt, s5_b_re, s5_b_im, s5_c_re, s5_c_im, s5_d, s5_glu_w, s5_glu_b, ffn_w_in, ffn_conv_w, ffn_w_down):
    bsz, seq, _ = x_prompt.shape
    nsmp = x_sample.shape[0]
    nph, nchunk = S5_CHUNK, seq // S5_CHUNK
    n_odd = s5_lam_re.shape[0]
    row = lambda a: a.reshape(1, -1)
    xp = x_prompt.reshape(bsz, nchunk, nph, D_MODEL)
    xs = x_sample.reshape(nsmp, D_MODEL)
    gf = row(norm_final)
    v_rows, cb_p, cb_s, cc_p, cc_s, re_p, re_s, im_p, im_s, fc_p, fc_s = ([] for _ in range(11))
    q = CHUNK // nph
    sds = jax.ShapeDtypeStruct

    w_in_bf = _to_bf16_all(w_mix_in, D_MODEL)
    w_out_bf = _to_bf16_all(w_mix_out, D_MODEL)
    ffn_in_bf = _to_bf16_all(ffn_w_in, D_MODEL // 2)
    ffn_down_bf = _to_bf16_all(ffn_w_down, D_FF // 2)
    glu_bf = _to_bf16_all(s5_glu_w, D_D)
    p = _s5_prepare(s5_lam_re, s5_lam_im, s5_log_dt, s5_b_re, s5_b_im, s5_c_re, s5_c_im, s5_d)
    p1_rows, p2_rows = p["p1"].reshape(n_odd, 1, -1), p["p2"].reshape(n_odd, 1, -1)

    for l in range(DEPTH):
        g_mix = row(norm_mix[l])
        if l % 2 == 0:
            e = l // 2
            alng, alnb = row(a_ln_g[e]), row(a_ln_b[e])
            ws_p = a_ws[e].reshape(A_HEADS, q, nph, q, nph).transpose(0, 2, 1, 4, 3).reshape(A_HEADS, CHUNK, CHUNK)
            bs_full = jnp.repeat(a_bs[e].T, A_HEAD_DIM, axis=1)
            bs_p = bs_full.reshape(q, nph, D_A).transpose(1, 0, 2).reshape(CHUNK, D_A)
            common = (row(b_conv_b[e]), row(b_ln_g[e]), row(b_ln_b[e]))
            xp, cb = _even_prompt(xp, g_mix, w_in_bf, w_out_bf, l, alng, alnb, ws_p, bs_p, b_conv_w[e], *common,
                                  natural_in=(l == 0))
            cb_p.append(cb)
            ws0 = row(jnp.repeat(a_ws[e][:, 0, 0], A_HEAD_DIM))
            bs0 = row(jnp.repeat(a_bs[e][:, 0], A_HEAD_DIM))
            buf = state_conv_b[e]
            xs, v, nb = _call_whole(
                _even_sample_kernel, "even_sample",
                (xs, g_mix, w_in_bf, w_out_bf, alng, alnb, ws0, bs0, b_conv_w[e], *common, buf),
                [sds((nsmp, D_MODEL), F32), sds((nsmp, D_A), F32), sds(buf.shape, F32)],
                specs={2: _layer_spec((D_MODEL, D_IN), l), 3: _layer_spec((D_MODEL, D_MODEL), l)})
            v_rows.append(v.reshape(nsmp, 1, D_A))
            cb_s.append(nb)
        else:
            o = l // 2
            wut = _to_bf16(w_mix_in, l, 3, D_D, transpose=True)
            gb = row(s5_glu_b[o])
            oc, ut, cc = _odd_in(xp, g_mix, w_in_bf, l, wut, c_conv_w[o])
            cc_p.append(cc)
            yt, hs = _s5_prompt(ut, p["mt"], p["st"], p["ot"], p["a1t"], p["a2t"], o)
            hs = hs.transpose(1, 0, 2)
            re_p.append(hs[..., :S5_STATE])
            im_p.append(hs[..., S5_STATE:])
            xp = _odd_out(xp, oc, yt, glu_bf, o, gb, w_out_bf, l)
            bufc = state_conv_c[o]
            hin = jnp.concatenate([state_ssm_re[o], state_ssm_im[o]], axis=-1).reshape(nsmp, -1)
            nstate = S5_GROUPS * 2 * S5_STATE
            grp_spec = pl.BlockSpec((S5_GROUPS, S5_GROUP, 2 * S5_STATE), lambda *_: (o, 0, 0),
                                    pipeline_mode=pl.Buffered(1))
            xs, nbc, nh = _call_whole(
                _odd_sample_kernel, "odd_sample",
                (xs, g_mix, w_in_bf, wut, w_out_bf, c_conv_w[o], bufc, hin, p1_rows[o], p2_rows[o],
                 p["bbar"], p["ccn"], row(s5_d[o]), glu_bf, gb),
                [sds((nsmp, D_MODEL), F32), sds(bufc.shape, F32), sds(hin.shape, F32)],
                scratch=[pltpu.VMEM((D_D, nstate), BF16), pltpu.VMEM((D_D, nstate), BF16)],
                specs={2: _layer_spec((D_MODEL, 3 * D_C), l), 4: _layer_spec((D_MODEL, D_MODEL), l),
                       10: grp_spec, 11: grp_spec, 13: _layer_spec((D_D, D_D), o)})
            cc_s.append(nbc)
            nh = nh.reshape(nsmp, S5_GROUPS, 2 * S5_STATE)
            re_s.append(nh[..., :S5_STATE])
            im_s.append(nh[..., S5_STATE:])
        final = l == DEPTH - 1
        g_ffn = row(norm_ffn[l])
        xp, fc = _ffn_prompt(xp, g_ffn, ffn_in_bf, ffn_down_bf, l, ffn_conv_w[l], gf, final)
        fc_p.append(fc)
        xs, nfc = _ffn_sample(xs, g_ffn, ffn_in_bf, ffn_down_bf, l, ffn_conv_w[l], state_ffn_conv[l], gf, final)
        fc_s.append(nfc)

    st = jnp.stack
    return (xp.reshape(bsz, seq, D_MODEL), xs.reshape(nsmp, 1, D_MODEL), st(v_rows), st(cb_p), st(cb_s),
            st(cc_p), st(cc_s), st(re_p), st(re_s), st(im_p), st(im_s), st(fc_p), st(fc_s))
```

```python
import functools

import numpy as np
import jax
import jax.numpy as jnp
from jax import lax
from jax.experimental import pallas as pl
from jax.experimental.pallas import tpu as pltpu

D_MODEL = 1024
DEPTH = 4
D_A = 512
D_B = 512
D_C = 512
D_D = 512
D_IN = 2048
A_HEADS = 4
A_HEAD_DIM = 128
CHUNK = 128
B_CONV_WIDTH = 31
S5_GROUP = 16
S5_GROUPS = 32
S5_STATE = 64
D_FF = 2816
EPS = 1e-6

S5_CHUNK = 16
S5_LEVELS = 7
S5_PREP_GB = 8
S5_GB = 2
PHO = 8
RC = 64
RS = 32
PH = 4
FF_CW = 256
FS_CW = 1408
SUB = 8
VMEM_LIMIT = 56 * 1024 * 1024

F32 = jnp.float32
BF16 = jnp.bfloat16


def _rms(x, g):
    return x * lax.rsqrt(jnp.mean(x * x, axis=-1, keepdims=True) + EPS) * g


def _ln(x, g, b):
    mu = jnp.mean(x, axis=-1, keepdims=True)
    xc = x - mu
    var = jnp.mean(xc * xc, axis=-1, keepdims=True)
    return xc * lax.rsqrt(var + EPS) * g + b


def _dot(a, b):
    return jnp.dot(a, b, preferred_element_type=F32)


def _dot_nt(a, b, precision=None):
    return lax.dot_general(a, b, (((1,), (1,)), ((), ())), precision=precision, preferred_element_type=F32)


def _const_spec(shape):
    nd = len(shape)
    return pl.BlockSpec(shape, lambda *_: (0,) * nd, pipeline_mode=pl.Buffered(1))


def _layer_spec(shape2d, layer, col_block=0):
    return pl.BlockSpec((None,) + tuple(shape2d), lambda *_: (layer, 0, col_block), pipeline_mode=pl.Buffered(1))


def _params(sem):
    return pltpu.CompilerParams(dimension_semantics=sem, vmem_limit_bytes=VMEM_LIMIT)


def _shift_rows(blk, first_row):
    row0 = lax.broadcasted_iota(jnp.int32, blk.shape, 0) == 0
    return jnp.where(row0, first_row, pltpu.roll(blk, 1, 0))


def _chunk_tile_spec(n):
    return pl.BlockSpec((1, S5_CHUNK, RC, n), lambda b, i: (b, 0, i, 0))


def _even_prompt_kernel(x_ref, g_ref, win_ref, wout_ref, alng_ref, alnb_ref, ws_ref, bs_ref,
                        bcw_ref, bcb_ref, blng_ref, blnb_ref, xo_ref, cb_ref,
                        z_ref, p_ref, gsh_ref, oa_ref, ob_ref, wb_ref, *, natural_in):
    i = pl.program_id(1)
    ni = pl.num_programs(1)
    nph, rc = xo_ref.shape[1], xo_ref.shape[2]
    rs = RS
    nsub, tms = rc // rs, nph * rs

    @pl.when(jnp.logical_and(pl.program_id(0) == 0, i == 0))
    def _():
        for k in range(B_CONV_WIDTH):
            wb_ref[k] = jnp.broadcast_to(bcw_ref[k:k + 1, :], (SUB, D_B))

    @pl.when(i == 0)
    def _():
        p_ref[0, :, 0:SUB, :] = jnp.zeros((nph, SUB, D_B), F32)

    def load(s):
        xt = jnp.swapaxes(x_ref[0, s * rs:(s + 1) * rs], 0, 1) if natural_in else x_ref[0, :, s * rs:(s + 1) * rs, :]
        return xt.reshape(tms, D_MODEL)

    xs = [load(s) for s in range(nsub)]
    for s in range(nsub):
        z_ref[s] = _dot(_rms(xs[s], g_ref[...]).astype(BF16), win_ref[...])

    q = CHUNK // nph
    r = lax.broadcasted_iota(jnp.int32, (CHUNK, CHUNK), 0)
    c = lax.broadcasted_iota(jnp.int32, (CHUNK, CHUNK), 1)
    pos = lambda k: (k & (q - 1)) * nph + (k >> (q.bit_length() - 1))
    keep = pos(r) >= pos(c)
    wm = [jnp.where(keep, ws_ref[h], 0.0).astype(BF16) for h in range(A_HEADS)]

    for s in range(nsub):
        if s > 0:
            p_ref[s, :, SUB - 2:SUB, :] = p_ref[s - 1, :, SUB + rs - 2:SUB + rs, :]
        for m in range(rs // q):
            rows = [ph * rs + q * m for ph in range(nph)]
            za = jax.nn.gelu(jnp.concatenate([z_ref[s, r0:r0 + q, 0:2 * D_A] for r0 in rows], axis=0))
            for h in range(A_HEADS):
                lo, hi = h * A_HEAD_DIM, (h + 1) * A_HEAD_DIM
                vh = _ln(za[:, D_A + lo:D_A + hi], alng_ref[:, lo:hi], alnb_ref[:, lo:hi])
                oa = za[:, lo:hi] * (_dot(wm[h], vh.astype(BF16)) + bs_ref[:, lo:hi])
                for ph, r0 in enumerate(rows):
                    oa_ref[s, r0:r0 + q, lo:hi] = oa[ph * q:(ph + 1) * q]

        for ph in range(nph):
            zb = z_ref[s, ph * rs:(ph + 1) * rs, 2 * D_A:]
            p_ref[s, ph, SUB:SUB + rs, :] = zb[:, :D_B] * jax.nn.sigmoid(zb[:, D_B:])
        for ph in range(nph):
            gsh_ref[s, 0, ph] = p_ref[s, ph, SUB - 1:SUB - 1 + rs, :]
            gsh_ref[s, 1, ph] = p_ref[s, ph, SUB - 2:SUB - 2 + rs, :]
        for ph in range(nph):
            acc = jnp.zeros((rs // SUB, SUB, D_B), F32)
            for j in range(B_CONV_WIDTH):
                src_ph = (ph - j) % nph
                back = (j - ph + nph - 1) // nph if j > ph else 0
                src = p_ref[s, src_ph, SUB:SUB + rs, :] if back == 0 else gsh_ref[s, back - 1, src_ph]
                acc = acc + wb_ref[B_CONV_WIDTH - 1 - j][None] * src.reshape(rs // SUB, SUB, D_B)
            y = _ln(acc.reshape(rs, D_B) + bcb_ref[...], blng_ref[...], blnb_ref[...])
            ob_ref[s, ph * rs:(ph + 1) * rs, :] = (y * jax.nn.sigmoid(y)).astype(BF16)

        out = (xs[s] + _dot(oa_ref[s].astype(BF16), wout_ref[0:D_A, :]) + _dot(ob_ref[s], wout_ref[D_A:, :]))
        xo_ref[0, :, s * rs:(s + 1) * rs, :] = out.reshape(nph, rs, D_MODEL)

    last = nsub - 1

    @pl.when(i == ni - 1)
    def _():
        n_out = B_CONV_WIDTH - 1
        for k in range(n_out):
            back = n_out - 1 - k
            ph, cl = (nph - 1 - back) % nph, rs - 1 - back // nph
            cb_ref[0, k:k + 1, :] = p_ref[last, ph, SUB + cl:SUB + cl + 1, :]

    p_ref[0, :, SUB - 2:SUB, :] = p_ref[last, :, SUB + rs - 2:SUB + rs, :]


def _even_prompt(x, g, w_in, w_out, layer, alng, alnb, ws, bs, bcw, bcb, blng, blnb, natural_in):
    bsz, nph, nchunk = x.shape[0], S5_CHUNK, x.shape[1] * x.shape[2] // S5_CHUNK
    row = lambda n: _const_spec((1, n))
    nsub, tms = RC // RS, nph * RS
    x_spec = pl.BlockSpec((1, RC, nph, D_MODEL), lambda b, i: (b, i, 0, 0)) if natural_in else \
        _chunk_tile_spec(D_MODEL)
    return pl.pallas_call(
        functools.partial(_even_prompt_kernel, natural_in=natural_in),
        grid=(bsz, nchunk // RC),
        in_specs=[x_spec, row(D_MODEL),
                  _layer_spec((D_MODEL, D_IN), layer), _layer_spec((D_MODEL, D_MODEL), layer),
                  row(D_A), row(D_A), _const_spec((A_HEADS, CHUNK, CHUNK)), _const_spec((CHUNK, D_A)),
                  _const_spec((B_CONV_WIDTH, D_B)), row(D_B), row(D_B), row(D_B)],
        out_specs=[_chunk_tile_spec(D_MODEL),
                   pl.BlockSpec((1, B_CONV_WIDTH - 1, D_B), lambda b, i: (b, 0, 0))],
        out_shape=[jax.ShapeDtypeStruct((bsz, nph, nchunk, D_MODEL), F32),
                   jax.ShapeDtypeStruct((bsz, B_CONV_WIDTH - 1, D_B), F32)],
        scratch_shapes=[pltpu.VMEM((nsub, tms, D_IN), F32), pltpu.VMEM((nsub, nph, RS + SUB, D_B), F32),
                        pltpu.VMEM((nsub, 2, nph, RS, D_B), F32), pltpu.VMEM((nsub, tms, D_A), F32),
                        pltpu.VMEM((nsub, tms, D_B), BF16), pltpu.VMEM((B_CONV_WIDTH, SUB, D_B), F32)],
        compiler_params=_params(("arbitrary", "arbitrary")),
        name="even_prompt",
    )(x, g, w_in, w_out, alng, alnb, ws, bs, bcw, bcb, blng, blnb)


def _ffn_prompt_kernel(x_ref, g_ref, wg_ref, wu_ref, wd_ref, cw_ref, gf_ref, xo_ref, fc_ref,
                       xn_ref, carry_ref, h_ref, *, final):
    i = pl.program_id(1)
    ni = pl.num_programs(1)
    nph, rc = x_ref.shape[1], x_ref.shape[2]
    rs = RS
    tms = nph * rs

    @pl.when(i == 0)
    def _():
        carry_ref[...] = jnp.zeros_like(carry_ref)

    lo2, lo1 = (nph - 2) * rs, (nph - 1) * rs
    for s in range(rc // rs):
        x = x_ref[0, :, s * rs:(s + 1) * rs, :].reshape(tms, D_MODEL)
        xn_ref[s] = _rms(x, g_ref[...]).astype(BF16)
        for c0 in range(0, D_FF, FF_CW):
            cs = slice(c0, c0 + FF_CW)
            zg = _dot(xn_ref[s], wg_ref[:, cs])
            zu = _dot(xn_ref[s], wu_ref[:, cs])
            s2 = _shift_rows(zg[lo2:lo1], carry_ref[0:1, cs])
            s1 = _shift_rows(zg[lo1:], carry_ref[1:2, cs])
            carry_ref[0:1, cs] = zg[lo1 - 1:lo1]
            carry_ref[1:2, cs] = zg[tms - 1:tms]
            z1 = jnp.concatenate([s1, zg[:lo1]], axis=0)
            z2 = jnp.concatenate([s2, s1, zg[:lo2]], axis=0)
            gc = cw_ref[0:1, cs] * z2 + cw_ref[1:2, cs] * z1 + cw_ref[2:3, cs] * zg
            h_ref[s, :, cs] = (gc * jax.nn.sigmoid(gc) * zu).astype(BF16)
        y = x + _dot(h_ref[s], wd_ref[...])
        if final:
            xo_ref[0, s * rs:(s + 1) * rs] = jnp.swapaxes(_rms(y, gf_ref[...]).reshape(nph, rs, D_MODEL), 0, 1)
        else:
            xo_ref[0, :, s * rs:(s + 1) * rs, :] = y.reshape(nph, rs, D_MODEL)

    @pl.when(i == ni - 1)
    def _():
        fc_ref[0] = carry_ref[...]


def _ffn_prompt(x, g, w_in, w_down, layer, cw, gf, final):
    bsz, nph, nchunk, _ = x.shape
    tm = nph * RC
    out_spec, out_shape = (pl.BlockSpec((1, RC, nph, D_MODEL), lambda b, i: (b, i, 0, 0)),
                           (bsz, nchunk, nph, D_MODEL)) if final else (_chunk_tile_spec(D_MODEL), x.shape)
    return pl.pallas_call(
        functools.partial(_ffn_prompt_kernel, final=final),
        grid=(bsz, nchunk // RC),
        in_specs=[_chunk_tile_spec(D_MODEL), _const_spec((1, D_MODEL)),
                  _layer_spec((D_MODEL, D_FF), layer, 0), _layer_spec((D_MODEL, D_FF), layer, 1),
                  _layer_spec((D_FF, D_MODEL), layer), _const_spec((3, D_FF)), _const_spec((1, D_MODEL))],
        out_specs=[out_spec, pl.BlockSpec((1, 2, D_FF), lambda b, i: (b, 0, 0))],
        out_shape=[jax.ShapeDtypeStruct(out_shape, F32), jax.ShapeDtypeStruct((bsz, 2, D_FF), F32)],
        scratch_shapes=[pltpu.VMEM((RC // RS, nph * RS, D_MODEL), BF16), pltpu.VMEM((2, D_FF), F32),
                        pltpu.VMEM((RC // RS, nph * RS, D_FF), BF16)],
        compiler_params=_params(("arbitrary", "arbitrary")),
        name="ffn_prompt",
    )(x, g, w_in, w_in, w_down, cw, gf)


def _odd_in_kernel(x_ref, g_ref, win_ref, wut_ref, ccw_ref, oc_ref, ut_ref, cc_ref, cin_ref, gb_ref):
    nph, nchunk = x_ref.shape[1], x_ref.shape[2]
    for s in range(nph // PH):
        rows = slice(s * PH * nchunk, (s + 1) * PH * nchunk)
        xn = _rms(x_ref[0, s * PH:(s + 1) * PH].reshape(PH * nchunk, D_MODEL), g_ref[...]).astype(BF16)
        z = _dot(xn, win_ref[...])
        cin_ref[rows, :] = z[:, 2 * D_C:] * z[:, :D_C]
        gb_ref[rows, :] = z[:, D_C:2 * D_C]
        ut = _dot_nt(wut_ref[...], xn)
        for k in range(PH):
            ut_ref[0, s * PH + k] = ut[:, k * nchunk:(k + 1) * nchunk].astype(BF16)

    zero = jnp.zeros((1, D_C), F32)
    blk = lambda ph: cin_ref[ph * nchunk:(ph + 1) * nchunk, :]
    s2 = _shift_rows(blk(nph - 2), zero)
    s1 = _shift_rows(blk(nph - 1), zero)
    for ph in range(nph):
        p1 = blk(ph - 1) if ph >= 1 else s1
        p2 = blk(ph - 2) if ph >= 2 else (s1 if ph == 1 else s2)
        conv = ccw_ref[0:1, :] * p2 + ccw_ref[1:2, :] * p1 + ccw_ref[2:3, :] * blk(ph)
        oc_ref[0, ph] = (gb_ref[ph * nchunk:(ph + 1) * nchunk, :] * conv).astype(BF16)
    last = nchunk - 1
    cc_ref[0, 0:1, :] = cin_ref[(nph - 2) * nchunk + last:(nph - 2) * nchunk + last + 1, :]
    cc_ref[0, 1:2, :] = cin_ref[(nph - 1) * nchunk + last:(nph - 1) * nchunk + last + 1, :]


def _odd_in(x, g, w_in, layer, wut, ccw):
    bsz, nph, nchunk, _ = x.shape
    seq = lambda *shape: pl.BlockSpec((1,) + shape, lambda b: (b,) + (0,) * len(shape))
    return pl.pallas_call(
        _odd_in_kernel,
        grid=(bsz,),
        in_specs=[seq(nph, nchunk, D_MODEL), _const_spec((1, D_MODEL)), _layer_spec((D_MODEL, 3 * D_C), layer),
                  _const_spec((D_D, D_MODEL)), _const_spec((3, D_C))],
        out_specs=[seq(nph, nchunk, D_C), seq(nph, D_D, nchunk), seq(2, D_C)],
        out_shape=[jax.ShapeDtypeStruct((bsz, nph, nchunk, D_C), BF16),
                   jax.ShapeDtypeStruct((bsz, nph, D_D, nchunk), BF16),
                   jax.ShapeDtypeStruct((bsz, 2, D_C), F32)],
        scratch_shapes=[pltpu.VMEM((nph * nchunk, D_C), F32), pltpu.VMEM((nph * nchunk, D_C), F32)],
        compiler_params=_params(("arbitrary",)),
        name="odd_in",
    )(x, g, w_in, wut, ccw)


def _s5_prompt_kernel(ut_ref, mt_ref, st_ref, ot_ref, a1_ref, a2_ref, yt_ref, hs_ref, h_ref):
    bsz, nph = ut_ref.shape[0], ut_ref.shape[1]
    nchunk = ut_ref.shape[3]
    n = nph * S5_GROUP
    swap = lambda v: jnp.concatenate([v[S5_STATE:], v[:S5_STATE]], axis=0)
    for gi in range(S5_GB):
        rows = slice(gi * S5_GROUP, (gi + 1) * S5_GROUP)
        ut = jnp.concatenate([ut_ref[b, :, rows, :].reshape(n, nchunk) for b in range(bsz)], axis=-1)
        x = _dot(st_ref[gi], ut)
        cidx = lax.broadcasted_iota(jnp.int32, x.shape, 1) & (nchunk - 1)
        a1, a2 = a1_ref[gi], a2_ref[gi]

        def shift(v, sh):
            rolled = jnp.concatenate([pltpu.roll(v[:, b * nchunk:(b + 1) * nchunk], sh, 1) for b in range(bsz)],
                                     axis=1)
            return jnp.where(cidx >= sh, rolled, 0.0)

        for k in range(S5_LEVELS):
            xs = shift(x, 1 << k)
            x = x + a1[:, k:k + 1] * xs + a2[:, k:k + 1] * swap(xs)
        h_ref[gi] = x.T
        hs_ref[gi] = h_ref[gi, pl.ds(nchunk - 1, bsz, stride=nchunk), :]
        hprev = shift(x, 1)
        yt = _dot(mt_ref[gi], ut) + _dot(ot_ref[gi], hprev.astype(BF16))
        for b in range(bsz):
            yt_ref[b, :, rows, :] = yt[:, b * nchunk:(b + 1) * nchunk].reshape(nph, S5_GROUP, nchunk)


def _s5_prompt(ut, mt, st, ot, a1t, a2t, odd):
    bsz, nph, _, nchunk = ut.shape
    n, lw = nph * S5_GROUP, 2 * S5_STATE
    steps = S5_GROUPS // S5_GB
    grp = lambda *shape: pl.BlockSpec((S5_GB,) + shape, lambda g: (g + odd * steps, 0, 0))
    act = pl.BlockSpec((bsz, nph, S5_GB * S5_GROUP, nchunk), lambda g: (0, 0, g, 0))
    return pl.pallas_call(
        _s5_prompt_kernel,
        grid=(steps,),
        in_specs=[act, grp(n, n), grp(lw, n), grp(n, lw), grp(lw, 8), grp(lw, 8)],
        out_specs=[act, pl.BlockSpec((S5_GB, bsz, lw), lambda g: (g, 0, 0))],
        out_shape=[jax.ShapeDtypeStruct(ut.shape, F32), jax.ShapeDtypeStruct((S5_GROUPS, bsz, lw), F32)],
        scratch_shapes=[pltpu.VMEM((S5_GB, bsz * nchunk, lw), F32)],
        compiler_params=_params(("arbitrary",)),
        name="s5_prompt",
    )(ut, mt, st, ot, a1t, a2t)


def _odd_out_kernel(x_ref, oc_ref, yt_ref, gw_ref, gb_ref, wout_ref, xo_ref):
    nph, nchunk = x_ref.shape[1], x_ref.shape[2]
    half = nph // 2
    tms = half * nchunk
    for s in range(2):
        ps = slice(s * half, (s + 1) * half)
        y = jnp.concatenate([yt_ref[0, k].T for k in range(s * half, (s + 1) * half)], axis=0)
        g = jax.nn.gelu(y)
        od = g * jax.nn.sigmoid(_dot(g.astype(BF16), gw_ref[...]) + gb_ref[...])
        out = (x_ref[0, ps].reshape(tms, D_MODEL) + _dot(oc_ref[0, ps].reshape(tms, D_C), wout_ref[0:D_C, :])
               + _dot(od.astype(BF16), wout_ref[D_C:, :]))
        xo_ref[0, ps] = out.reshape(half, nchunk, D_MODEL)


def _odd_out(x, oc, yt, glu_w, odd, gb, w_out, layer):
    bsz, nph, nchunk, _ = x.shape
    tok = lambda n: pl.BlockSpec((1, PHO, nchunk, n), lambda b, s: (b, s, 0, 0))
    return pl.pallas_call(
        _odd_out_kernel,
        grid=(bsz, nph // PHO),
        in_specs=[tok(D_MODEL), tok(D_C), pl.BlockSpec((1, PHO, D_D, nchunk), lambda b, s: (b, s, 0, 0)),
                  _layer_spec((D_D, D_D), odd), _const_spec((1, D_D)), _layer_spec((D_MODEL, D_MODEL), layer)],
        out_specs=tok(D_MODEL),
        out_shape=jax.ShapeDtypeStruct(x.shape, F32),
        compiler_params=_params(("arbitrary", "arbitrary")),
        name="odd_out",
    )(x, oc, yt, glu_w, gb, w_out)


def _even_sample_kernel(x_ref, g_ref, win_ref, wout_ref, alng_ref, alnb_ref, ws0_ref, bs0_ref,
                        bcw_ref, bcb_ref, blng_ref, blnb_ref, buf_ref, xo_ref, v_ref, nb_ref):
    x = x_ref[...]
    z = _dot(_rms(x, g_ref[...]).astype(BF16), win_ref[...])
    za = jax.nn.gelu(z[:, :2 * D_A])
    vs = []
    for h in range(A_HEADS):
        lo, hi = h * A_HEAD_DIM, (h + 1) * A_HEAD_DIM
        vs.append(_ln(za[:, D_A + lo:D_A + hi], alng_ref[:, lo:hi], alnb_ref[:, lo:hi]))
    v = jnp.concatenate(vs, axis=-1)
    v_ref[...] = v
    gate = ws0_ref[...] * v + bs0_ref[...]
    out_a = za[:, :D_A] * gate
    glu = z[:, 2 * D_A:2 * D_A + D_B] * jax.nn.sigmoid(z[:, 2 * D_A + D_B:])
    nk = B_CONV_WIDTH - 1
    buf = buf_ref[...]
    acc = bcb_ref[...] + bcw_ref[nk:nk + 1, :] * glu + jnp.sum(buf * bcw_ref[0:nk, :][None], axis=1)
    nb_ref[:, 0:nk - 1, :] = buf[:, 1:nk, :]
    nb_ref[:, nk - 1:nk, :] = glu[:, None, :]
    y = _ln(acc, blng_ref[...], blnb_ref[...])
    out_b = y * jax.nn.sigmoid(y)
    cat = jnp.concatenate([out_a, out_b], axis=-1).astype(BF16)
    xo_ref[...] = x + _dot(cat, wout_ref[...])


def _whole(shape):
    nd = len(shape)
    return pl.BlockSpec(shape, lambda *_: (0,) * nd)


def _call_whole(kernel_fn, name, args, out_shapes, scratch=(), specs=None):
    specs = specs or {}
    return pl.pallas_call(
        kernel_fn,
        grid=(1,),
        in_specs=[specs.get(i) or _const_spec(a.shape) for i, a in enumerate(args)],
        out_specs=[_whole(s.shape) for s in out_shapes],
        out_shape=out_shapes,
        scratch_shapes=list(scratch),
        compiler_params=_params(("arbitrary",)),
        name=name,
    )(*args)


def _odd_sample_kernel(x_ref, g_ref, wina_ref, wut_ref, wout_ref, ccw_ref, buf_ref, h_ref, p1_ref, p2_ref,
                       bbar_ref, ccn_ref, dsk_ref, gw_ref, gb_ref,
                       xo_ref, nb_ref, nh_ref, bd_ref, cd_ref):
    lw = 2 * S5_STATE
    bd_ref[...] = jnp.zeros_like(bd_ref)
    cd_ref[...] = jnp.zeros_like(cd_ref)
    for gi in range(S5_GROUPS):
        r0, l0 = gi * S5_GROUP, gi * lw
        bd_ref[r0:r0 + S5_GROUP, l0:l0 + lw] = bbar_ref[gi]
        cd_ref[r0:r0 + S5_GROUP, l0:l0 + lw] = ccn_ref[gi]
    x = x_ref[...]
    xn = _rms(x, g_ref[...]).astype(BF16)
    z = _dot(xn, wina_ref[...])
    u = _dot_nt(xn, wut_ref[...])
    cin = z[:, 2 * D_C:] * z[:, :D_C]
    buf = buf_ref[...]
    conv = jnp.sum(buf * ccw_ref[0:2, :][None], axis=1) + ccw_ref[2:3, :] * cin
    nb_ref[:, 0:1, :] = buf[:, 1:2, :]
    nb_ref[:, 1:2, :] = cin[:, None, :]
    out_c = z[:, D_C:2 * D_C] * conv
    h = h_ref[...]
    hswap = jnp.concatenate([pltpu.roll(h[:, gi * lw:(gi + 1) * lw], S5_STATE, 1) for gi in range(S5_GROUPS)],
                            axis=-1)
    nh = p1_ref[...] * h + p2_ref[...] * hswap + _dot(u.astype(BF16), bd_ref[...])
    nh_ref[...] = nh
    y = _dot_nt(nh.astype(BF16), cd_ref[...]) + dsk_ref[...] * u
    g = jax.nn.gelu(y)
    od = g * jax.nn.sigmoid(_dot(g.astype(BF16), gw_ref[...]) + gb_ref[...])
    cat = jnp.concatenate([out_c, od], axis=-1).astype(BF16)
    xo_ref[...] = x + _dot(cat, wout_ref[...])


def _ffn_sample_kernel(x_ref, g_ref, wg_ref, wu_ref, wd_ref, cw_ref, b_ref, gf_ref,
                       xo_ref, n_ref, xn_ref, acc_ref, *, final):
    j = pl.program_id(0)

    @pl.when(j == 0)
    def _():
        xn_ref[...] = _rms(x_ref[...], g_ref[...]).astype(BF16)
        acc_ref[...] = jnp.zeros_like(acc_ref)

    zg = _dot(xn_ref[...], wg_ref[...])
    zu = _dot(xn_ref[...], wu_ref[...])
    buf = b_ref[...]
    gc = jnp.sum(buf * cw_ref[0:2, :][None], axis=1) + cw_ref[2:3, :] * zg
    n_ref[:, 0:1, :] = buf[:, 1:2, :]
    n_ref[:, 1:2, :] = zg[:, None, :]
    acc_ref[...] += _dot((gc * jax.nn.sigmoid(gc) * zu).astype(BF16), wd_ref[...])

    @pl.when(j == pl.num_programs(0) - 1)
    def _():
        y = x_ref[...] + acc_ref[...]
        if final:
            y = _rms(y, gf_ref[...])
        xo_ref[...] = y


def _ffn_sample(x, g, w_in, w_down, layer, cw, buf, gf, final):
    n = x.shape[0]
    nc = D_FF // FS_CW
    state = pl.BlockSpec((n, 2, FS_CW), lambda j: (0, 0, j))
    return pl.pallas_call(
        functools.partial(_ffn_sample_kernel, final=final),
        grid=(nc,),
        in_specs=[_whole((n, D_MODEL)), _whole((1, D_MODEL)),
                  pl.BlockSpec((None, D_MODEL, FS_CW), lambda j: (layer, 0, j)),
                  pl.BlockSpec((None, D_MODEL, FS_CW), lambda j: (layer, 0, j + nc)),
                  pl.BlockSpec((None, FS_CW, D_MODEL), lambda j: (layer, j, 0)),
                  pl.BlockSpec((3, FS_CW), lambda j: (0, j)), state, _whole((1, D_MODEL))],
        out_specs=[_whole((n, D_MODEL)), state],
        out_shape=[jax.ShapeDtypeStruct((n, D_MODEL), F32), jax.ShapeDtypeStruct((n, 2, D_FF), F32)],
        scratch_shapes=[pltpu.VMEM((n, D_MODEL), BF16), pltpu.VMEM((n, D_MODEL), F32)],
        compiler_params=_params(("arbitrary",)),
        name="ffn_sample",
    )(x, g, w_in, w_in, w_down, cw, buf, gf)


_S5_POW = np.concatenate([np.arange(1 - S5_CHUNK, S5_CHUNK + 1), S5_CHUNK * 2 ** np.arange(8)]).astype(np.float32)
_S5_POW_ROWS = len(_S5_POW)


def _s5_prep_kernel(pw_ref, *refs):
    for gi in range(S5_PREP_GB):
        _s5_prep_group(pw_ref, *[r.at[pl.ds(gi, 1)] for r in refs])


def _s5_prep_group(pw_ref, lr_ref, li_ref, ldt_ref, bb_ref, bbs_ref, cc_ref, ccs_ref, dd_ref,
                   mt_ref, st_ref, ot_ref, a1_ref, a2_ref, bbar_ref, ccn_ref, p1_ref, p2_ref):
    lw, lc = 2 * S5_STATE, S5_CHUNK
    lane = lax.broadcasted_iota(jnp.int32, (1, lw), 1)
    sgn = jnp.where(lane < S5_STATE, -1.0, 1.0)
    lr, li = lr_ref[0], li_ref[0]
    dt = jnp.exp(ldt_ref[0])
    er, ei = lr * dt, li * dt
    jm = pw_ref[...]
    mag = jnp.exp(jm * er)
    cr, ci = mag * jnp.cos(jm * ei), mag * jnp.sin(jm * ei)

    def scale(j, x, xs):
        i = j + lc - 1
        return cr[i:i + 1] * x + (sgn * ci[i:i + 1]) * xs

    one = lc
    nr_, ni_ = cr[one:one + 1] - 1.0, ci[one:one + 1]
    den = lr * lr + li * li
    cfr = (nr_ * lr + ni_ * li) / den
    cfi = (ni_ * lr - nr_ * li) / den
    bb, bbs, cc, ccs = bb_ref[0], bbs_ref[0], cc_ref[0], ccs_ref[0]
    bbar = cfr * bb + (sgn * cfi) * bbs
    bbars = cfr * bbs - (sgn * cfi) * bb
    cat = lambda blocks: jnp.concatenate(blocks, axis=0)
    ymat = cat([scale(t, cc, ccs) for t in range(lc)])
    xmat = cat([scale(-t, bbar, bbars) * (-sgn) for t in range(lc)])
    mt = _dot_nt(ymat, xmat, precision=lax.Precision.HIGHEST)
    n = lc * S5_GROUP
    row = lax.broadcasted_iota(jnp.int32, (n, n), 0)
    col = lax.broadcasted_iota(jnp.int32, (n, n), 1)
    mt = jnp.where((row >> 4) >= (col >> 4), mt, 0.0)
    mt = mt + jnp.where(row == col, dd_ref[0], 0.0)
    mt_ref[0] = mt.astype(BF16)
    smat = cat([scale(lc - 1 - t, bbar, bbars) for t in range(lc)])
    st_ref[0] = smat.T.astype(BF16)
    ot_ref[0] = cat([scale(t + 1, cc, ccs) * (-sgn) for t in range(lc)]).astype(BF16)
    a1_ref[0] = cr[2 * lc:2 * lc + 8].T
    a2_ref[0] = (sgn * ci[2 * lc:2 * lc + 8]).T
    bbar_ref[0] = bbar.astype(BF16)
    ccn_ref[0] = (cc * (-sgn)).astype(BF16)
    p1_ref[0] = cr[one:one + 1]
    p2_ref[0] = sgn * ci[one:one + 1]


def _s5_prepare(lam_re, lam_im, log_dt, b_re, b_im, c_re, c_im, d_skip):
    flat = lambda a: a.reshape((-1,) + a.shape[2:])
    lam_re, lam_im, log_dt, b_re, b_im, c_re, c_im, d_skip = map(
        flat, (lam_re, lam_im, log_dt, b_re, b_im, c_re, c_im, d_skip))
    ng, lw, n = lam_re.shape[0], 2 * S5_STATE, S5_CHUNK * S5_GROUP
    two = lambda a: jnp.concatenate([a, a], axis=-1)[:, None, :]
    pack = lambda a, b: jnp.concatenate([a, b], axis=-1)
    bt_re, bt_im = b_re.transpose(0, 2, 1), b_im.transpose(0, 2, 1)
    args = (jnp.asarray(np.tile(_S5_POW[:, None], (1, lw))), two(lam_re), two(lam_im),
            jnp.broadcast_to(log_dt[:, None, None], (ng, 1, lw)),
            pack(bt_re, bt_im), pack(bt_im, bt_re), pack(c_re, c_im), pack(c_im, c_re),
            jnp.tile(d_skip, (1, S5_CHUNK))[:, None, :])
    blk = lambda *shape: pl.BlockSpec((S5_PREP_GB,) + shape, lambda s: (s, 0, 0))
    shapes = [((n, n), BF16), ((lw, n), BF16), ((n, lw), BF16), ((lw, 8), F32), ((lw, 8), F32),
              ((S5_GROUP, lw), BF16), ((S5_GROUP, lw), BF16), ((1, lw), F32), ((1, lw), F32)]
    outs = pl.pallas_call(
        _s5_prep_kernel,
        grid=(ng // S5_PREP_GB,),
        in_specs=[_const_spec((_S5_POW_ROWS, lw))] + [blk(*a.shape[1:]) for a in args[1:]],
        out_specs=[blk(*sh) for sh, _ in shapes],
        out_shape=[jax.ShapeDtypeStruct((ng,) + sh, dtp) for sh, dtp in shapes],
        compiler_params=_params(("arbitrary",)),
        name="s5_prep",
    )(*args)
    return dict(zip(("mt", "st", "ot", "a1t", "a2t", "bbar", "ccn", "p1", "p2"), outs))


def _cast_kernel(w_ref, o_ref, *, transpose):
    w = w_ref[...]
    o_ref[...] = (w.T if transpose else w).astype(BF16)


def _to_bf16_all(w, tk):
    nl, k, n = w.shape
    spec = pl.BlockSpec((None, tk, n), lambda l, i: (l, i, 0))
    return pl.pallas_call(
        functools.partial(_cast_kernel, transpose=False),
        grid=(nl, k // tk),
        in_specs=[spec],
        out_specs=spec,
        out_shape=jax.ShapeDtypeStruct(w.shape, BF16),
        compiler_params=_params(("arbitrary", "arbitrary")),
        name="to_bf16_all",
    )(w)


def _to_bf16(w, layer, col_block=0, ncols=None, transpose=False):
    _, k, n = w.shape
    ncols = ncols or n
    tk = 256 if k % 256 == 0 else k
    out_block, out_map, out_shape = ((ncols, tk), lambda i: (0, i), (ncols, k)) if transpose else \
        ((tk, ncols), lambda i: (i, 0), (k, ncols))
    return pl.pallas_call(
        functools.partial(_cast_kernel, transpose=transpose),
        grid=(k // tk,),
        in_specs=[pl.BlockSpec((None, tk, ncols), lambda i: (layer, i, col_block))],
        out_specs=pl.BlockSpec(out_block, out_map),
        out_shape=jax.ShapeDtypeStruct(out_shape, BF16),
        compiler_params=_params(("arbitrary",)),
        name="to_bf16",
    )(w)


def kernel(x_prompt, x_sample, state_conv_b, state_conv_c, state_ssm_re, state_ssm_im, state_ffn_conv, norm_mix, norm_ffn, norm_final, w_mix_in, w_mix_out, a_ln_g, a_ln_b, a_ws, a_bs, b_conv_w, b_conv_b, b_ln_g, b_ln_b, c_conv_w, s5_lam_re, s5_lam_im, s5_log_dt, s5_b_re, s5_b_im, s5_c_re, s5_c_im, s5_d, s5_glu_w, s5_glu_b, ffn_w_in, ffn_conv_w, ffn_w_down):
    bsz, seq, _ = x_prompt.shape
    nsmp = x_sample.shape[0]
    nph, nchunk = S5_CHUNK, seq // S5_CHUNK
    n_odd = s5_lam_re.shape[0]
    row = lambda a: a.reshape(1, -1)
    xp = x_prompt.reshape(bsz, nchunk, nph, D_MODEL)
    xs = x_sample.reshape(nsmp, D_MODEL)
    gf = row(norm_final)
    v_rows, cb_p, cb_s, cc_p, cc_s, re_p, re_s, im_p, im_s, fc_p, fc_s = ([] for _ in range(11))
    q = CHUNK // nph
    sds = jax.ShapeDtypeStruct

    w_in_bf = _to_bf16_all(w_mix_in, D_MODEL)
    w_out_bf = _to_bf16_all(w_mix_out, D_MODEL)
    ffn_in_bf = _to_bf16_all(ffn_w_in, D_MODEL // 2)
    ffn_down_bf = _to_bf16_all(ffn_w_down, D_FF // 2)
    glu_bf = _to_bf16_all(s5_glu_w, D_D)
    p = _s5_prepare(s5_lam_re, s5_lam_im, s5_log_dt, s5_b_re, s5_b_im, s5_c_re, s5_c_im, s5_d)
    p1_rows, p2_rows = p["p1"].reshape(n_odd, 1, -1), p["p2"].reshape(n_odd, 1, -1)

    for l in range(DEPTH):
        g_mix = row(norm_mix[l])
        if l % 2 == 0:
            e = l // 2
            alng, alnb = row(a_ln_g[e]), row(a_ln_b[e])
            ws_p = a_ws[e].reshape(A_HEADS, q, nph, q, nph).transpose(0, 2, 1, 4, 3).reshape(A_HEADS, CHUNK, CHUNK)
            bs_full = jnp.repeat(a_bs[e].T, A_HEAD_DIM, axis=1)
            bs_p = bs_full.reshape(q, nph, D_A).transpose(1, 0, 2).reshape(CHUNK, D_A)
            common = (row(b_conv_b[e]), row(b_ln_g[e]), row(b_ln_b[e]))
            xp, cb = _even_prompt(xp, g_mix, w_in_bf, w_out_bf, l, alng, alnb, ws_p, bs_p, b_conv_w[e], *common,
                                  natural_in=(l == 0))
            cb_p.append(cb)
            ws0 = row(jnp.repeat(a_ws[e][:, 0, 0], A_HEAD_DIM))
            bs0 = row(jnp.repeat(a_bs[e][:, 0], A_HEAD_DIM))
            buf = state_conv_b[e]
            xs, v, nb = _call_whole(
                _even_sample_kernel, "even_sample",
                (xs, g_mix, w_in_bf, w_out_bf, alng, alnb, ws0, bs0, b_conv_w[e], *common, buf),
                [sds((nsmp, D_MODEL), F32), sds((nsmp, D_A), F32), sds(buf.shape, F32)],
                specs={2: _layer_spec((D_MODEL, D_IN), l), 3: _layer_spec((D_MODEL, D_MODEL), l)})
            v_rows.append(v.reshape(nsmp, 1, D_A))
            cb_s.append(nb)
        else:
            o = l // 2
            wut = _to_bf16(w_mix_in, l, 3, D_D, transpose=True)
            gb = row(s5_glu_b[o])
            oc, ut, cc = _odd_in(xp, g_mix, w_in_bf, l, wut, c_conv_w[o])
            cc_p.append(cc)
            yt, hs = _s5_prompt(ut, p["mt"], p["st"], p["ot"], p["a1t"], p["a2t"], o)
            hs = hs.transpose(1, 0, 2)
            re_p.append(hs[..., :S5_STATE])
            im_p.append(hs[..., S5_STATE:])
            xp = _odd_out(xp, oc, yt, glu_bf, o, gb, w_out_bf, l)
            bufc = state_conv_c[o]
            hin = jnp.concatenate([state_ssm_re[o], state_ssm_im[o]], axis=-1).reshape(nsmp, -1)
            nstate = S5_GROUPS * 2 * S5_STATE
            grp_spec = pl.BlockSpec((S5_GROUPS, S5_GROUP, 2 * S5_STATE), lambda *_: (o, 0, 0),
                                    pipeline_mode=pl.Buffered(1))
            xs, nbc, nh = _call_whole(
                _odd_sample_kernel, "odd_sample",
                (xs, g_mix, w_in_bf, wut, w_out_bf, c_conv_w[o], bufc, hin, p1_rows[o], p2_rows[o],
                 p["bbar"], p["ccn"], row(s5_d[o]), glu_bf, gb),
                [sds((nsmp, D_MODEL), F32), sds(bufc.shape, F32), sds(hin.shape, F32)],
                scratch=[pltpu.VMEM((D_D, nstate), BF16), pltpu.VMEM((D_D, nstate), BF16)],
                specs={2: _layer_spec((D_MODEL, 3 * D_C), l), 4: _layer_spec((D_MODEL, D_MODEL), l),
                       10: grp_spec, 11: grp_spec, 13: _layer_spec((D_D, D_D), o)})
            cc_s.append(nbc)
            nh = nh.reshape(nsmp, S5_GROUPS, 2 * S5_STATE)
            re_s.append(nh[..., :S5_STATE])
            im_s.append(nh[..., S5_STATE:])
        final = l == DEPTH - 1
        g_ffn = row(norm_ffn[l])
        xp, fc = _ffn_prompt(xp, g_ffn, ffn_in_bf, ffn_down_bf, l, ffn_conv_w[l], gf, final)
        fc_p.append(fc)
        xs, nfc = _ffn_sample(xs, g_ffn, ffn_in_bf, ffn_down_bf, l, ffn_conv_w[l], state_ffn_conv[l], gf, final)
        fc_s.append(nfc)

    st = jnp.stack
    return (xp.reshape(bsz, seq, D_MODEL), xs.reshape(nsmp, 1, D_MODEL), st(v_rows), st(cb_p), st(cb_s),
            st(cc_p), st(cc_s), st(re_p), st(re_s), st(im_p), st(im_s), st(fc_p), st(fc_s))
```

```python
import functools

import numpy as np
import jax
import jax.numpy as jnp
from jax import lax
from jax.experimental import pallas as pl
from jax.experimental.pallas import tpu as pltpu

D_MODEL = 1024
DEPTH = 4
D_A = 512
D_B = 512
D_C = 512
D_D = 512
D_IN = 2048
A_HEADS = 4
A_HEAD_DIM = 128
CHUNK = 128
B_CONV_WIDTH = 31
S5_GROUP = 16
S5_GROUPS = 32
S5_STATE = 64
D_FF = 2816
EPS = 1e-6

S5_CHUNK = 16
S5_LEVELS = 7
S5_PREP_GB = 8
S5_GB = 2
PHO = 8
RC = 64
RS = 32
PH = 4
FF_CW = 256
FS_CW = 256
SUB = 8
VMEM_LIMIT = 56 * 1024 * 1024

F32 = jnp.float32
BF16 = jnp.bfloat16


def _rms(x, g):
    return x * lax.rsqrt(jnp.mean(x * x, axis=-1, keepdims=True) + EPS) * g


def _ln(x, g, b):
    mu = jnp.mean(x, axis=-1, keepdims=True)
    xc = x - mu
    var = jnp.mean(xc * xc, axis=-1, keepdims=True)
    return xc * lax.rsqrt(var + EPS) * g + b


def _dot(a, b):
    return jnp.dot(a, b, preferred_element_type=F32)


def _dot_nt(a, b, precision=None):
    return lax.dot_general(a, b, (((1,), (1,)), ((), ())), precision=precision, preferred_element_type=F32)


def _const_spec(shape):
    nd = len(shape)
    return pl.BlockSpec(shape, lambda *_: (0,) * nd, pipeline_mode=pl.Buffered(1))


def _layer_spec(shape2d, layer, col_block=0):
    return pl.BlockSpec((None,) + tuple(shape2d), lambda *_: (layer, 0, col_block), pipeline_mode=pl.Buffered(1))


def _lead_spec(shape, idx):
    nd = len(shape)
    return pl.BlockSpec((None,) + tuple(shape), lambda *_: (idx,) + (0,) * nd, pipeline_mode=pl.Buffered(1))


def _params(sem):
    return pltpu.CompilerParams(dimension_semantics=sem, vmem_limit_bytes=VMEM_LIMIT)


def _shift_rows(blk, first_row):
    row0 = lax.broadcasted_iota(jnp.int32, blk.shape, 0) == 0
    return jnp.where(row0, first_row, pltpu.roll(blk, 1, 0))


def _chunk_tile_spec(n):
    return pl.BlockSpec((1, S5_CHUNK, RC, n), lambda b, i: (b, 0, i, 0))


def _even_prompt_kernel(x_ref, g_ref, win_ref, wout_ref, alng_ref, alnb_ref, ws_ref, bs_ref,
                        bcw_ref, bcb_ref, blng_ref, blnb_ref, xo_ref, cb_ref,
                        z_ref, p_ref, gsh_ref, oa_ref, ob_ref, wb_ref, *, natural_in):
    i = pl.program_id(1)
    ni = pl.num_programs(1)
    nph, rc = xo_ref.shape[1], xo_ref.shape[2]
    rs = RS
    nsub, tms = rc // rs, nph * rs

    @pl.when(jnp.logical_and(pl.program_id(0) == 0, i == 0))
    def _():
        for k in range(B_CONV_WIDTH):
            wb_ref[k] = jnp.broadcast_to(bcw_ref[k:k + 1, :], (SUB, D_B))

    @pl.when(i == 0)
    def _():
        p_ref[0, :, 0:SUB, :] = jnp.zeros((nph, SUB, D_B), F32)

    def load(s):
        xt = jnp.swapaxes(x_ref[0, s * rs:(s + 1) * rs], 0, 1) if natural_in else x_ref[0, :, s * rs:(s + 1) * rs, :]
        return xt.reshape(tms, D_MODEL)

    xs = [load(s) for s in range(nsub)]
    for s in range(nsub):
        z_ref[s] = _dot(_rms(xs[s], g_ref[...]).astype(BF16), win_ref[...])

    q = CHUNK // nph
    r = lax.broadcasted_iota(jnp.int32, (CHUNK, CHUNK), 0)
    c = lax.broadcasted_iota(jnp.int32, (CHUNK, CHUNK), 1)
    pos = lambda k: (k & (q - 1)) * nph + (k >> (q.bit_length() - 1))
    keep = pos(r) >= pos(c)
    wm = [jnp.where(keep, ws_ref[h], 0.0).astype(BF16) for h in range(A_HEADS)]

    for s in range(nsub):
        if s > 0:
            p_ref[s, :, SUB - 2:SUB, :] = p_ref[s - 1, :, SUB + rs - 2:SUB + rs, :]
        for m in range(rs // q):
            rows = [ph * rs + q * m for ph in range(nph)]
            za = jax.nn.gelu(jnp.concatenate([z_ref[s, r0:r0 + q, 0:2 * D_A] for r0 in rows], axis=0))
            for h in range(A_HEADS):
                lo, hi = h * A_HEAD_DIM, (h + 1) * A_HEAD_DIM
                vh = _ln(za[:, D_A + lo:D_A + hi], alng_ref[:, lo:hi], alnb_ref[:, lo:hi])
                oa = za[:, lo:hi] * (_dot(wm[h], vh.astype(BF16)) + bs_ref[:, lo:hi])
                for ph, r0 in enumerate(rows):
                    oa_ref[s, r0:r0 + q, lo:hi] = oa[ph * q:(ph + 1) * q]

        for ph in range(nph):
            zb = z_ref[s, ph * rs:(ph + 1) * rs, 2 * D_A:]
            p_ref[s, ph, SUB:SUB + rs, :] = zb[:, :D_B] * jax.nn.sigmoid(zb[:, D_B:])
        for ph in range(nph):
            gsh_ref[s, 0, ph] = p_ref[s, ph, SUB - 1:SUB - 1 + rs, :]
            gsh_ref[s, 1, ph] = p_ref[s, ph, SUB - 2:SUB - 2 + rs, :]
        for ph in range(nph):
            acc = jnp.zeros((rs // SUB, SUB, D_B), F32)
            for j in range(B_CONV_WIDTH):
                src_ph = (ph - j) % nph
                back = (j - ph + nph - 1) // nph if j > ph else 0
                src = p_ref[s, src_ph, SUB:SUB + rs, :] if back == 0 else gsh_ref[s, back - 1, src_ph]
                acc = acc + wb_ref[B_CONV_WIDTH - 1 - j][None] * src.reshape(rs // SUB, SUB, D_B)
            y = _ln(acc.reshape(rs, D_B) + bcb_ref[...], blng_ref[...], blnb_ref[...])
            ob_ref[s, ph * rs:(ph + 1) * rs, :] = (y * jax.nn.sigmoid(y)).astype(BF16)

        out = (xs[s] + _dot(oa_ref[s].astype(BF16), wout_ref[0:D_A, :]) + _dot(ob_ref[s], wout_ref[D_A:, :]))
        xo_ref[0, :, s * rs:(s + 1) * rs, :] = out.reshape(nph, rs, D_MODEL)

    last = nsub - 1

    @pl.when(i == ni - 1)
    def _():
        n_out = B_CONV_WIDTH - 1
        for k in range(n_out):
            back = n_out - 1 - k
            ph, cl = (nph - 1 - back) % nph, rs - 1 - back // nph
            cb_ref[0, k:k + 1, :] = p_ref[last, ph, SUB + cl:SUB + cl + 1, :]

    p_ref[0, :, SUB - 2:SUB, :] = p_ref[last, :, SUB + rs - 2:SUB + rs, :]


def _even_prompt(x, g, w_in, w_out, layer, alng, alnb, ws, bs, bcw, bcb, blng, blnb, natural_in):
    bsz, nph, nchunk = x.shape[0], S5_CHUNK, x.shape[1] * x.shape[2] // S5_CHUNK
    row = lambda n: _const_spec((1, n))
    nsub, tms = RC // RS, nph * RS
    x_spec = pl.BlockSpec((1, RC, nph, D_MODEL), lambda b, i: (b, i, 0, 0)) if natural_in else \
        _chunk_tile_spec(D_MODEL)
    return pl.pallas_call(
        functools.partial(_even_prompt_kernel, natural_in=natural_in),
        grid=(bsz, nchunk // RC),
        in_specs=[x_spec, row(D_MODEL),
                  _layer_spec((D_MODEL, D_IN), layer), _layer_spec((D_MODEL, D_MODEL), layer),
                  row(D_A), row(D_A), _const_spec((A_HEADS, CHUNK, CHUNK)), _const_spec((CHUNK, D_A)),
                  _const_spec((B_CONV_WIDTH, D_B)), row(D_B), row(D_B), row(D_B)],
        out_specs=[_chunk_tile_spec(D_MODEL),
                   pl.BlockSpec((1, B_CONV_WIDTH - 1, D_B), lambda b, i: (b, 0, 0))],
        out_shape=[jax.ShapeDtypeStruct((bsz, nph, nchunk, D_MODEL), F32),
                   jax.ShapeDtypeStruct((bsz, B_CONV_WIDTH - 1, D_B), F32)],
        scratch_shapes=[pltpu.VMEM((nsub, tms, D_IN), F32), pltpu.VMEM((nsub, nph, RS + SUB, D_B), F32),
                        pltpu.VMEM((nsub, 2, nph, RS, D_B), F32), pltpu.VMEM((nsub, tms, D_A), F32),
                        pltpu.VMEM((nsub, tms, D_B), BF16), pltpu.VMEM((B_CONV_WIDTH, SUB, D_B), F32)],
        compiler_params=_params(("arbitrary", "arbitrary")),
        name="even_prompt",
    )(x, g, w_in, w_out, alng, alnb, ws, bs, bcw, bcb, blng, blnb)


def _ffn_prompt_kernel(x_ref, g_ref, wg_ref, wu_ref, wd_ref, cw_ref, gf_ref, xo_ref, fc_ref,
                       xn_ref, carry_ref, h_ref, *, final):
    i = pl.program_id(1)
    ni = pl.num_programs(1)
    nph, rc = x_ref.shape[1], x_ref.shape[2]
    rs = RS
    tms = nph * rs

    @pl.when(i == 0)
    def _():
        carry_ref[...] = jnp.zeros_like(carry_ref)

    lo2, lo1 = (nph - 2) * rs, (nph - 1) * rs
    for s in range(rc // rs):
        x = x_ref[0, :, s * rs:(s + 1) * rs, :].reshape(tms, D_MODEL)
        xn_ref[s] = _rms(x, g_ref[...]).astype(BF16)
        for c0 in range(0, D_FF, FF_CW):
            cs = slice(c0, c0 + FF_CW)
            zg = _dot(xn_ref[s], wg_ref[:, cs])
            zu = _dot(xn_ref[s], wu_ref[:, cs])
            s2 = _shift_rows(zg[lo2:lo1], carry_ref[0:1, cs])
            s1 = _shift_rows(zg[lo1:], carry_ref[1:2, cs])
            carry_ref[0:1, cs] = zg[lo1 - 1:lo1]
            carry_ref[1:2, cs] = zg[tms - 1:tms]
            z1 = jnp.concatenate([s1, zg[:lo1]], axis=0)
            z2 = jnp.concatenate([s2, s1, zg[:lo2]], axis=0)
            gc = cw_ref[0:1, cs] * z2 + cw_ref[1:2, cs] * z1 + cw_ref[2:3, cs] * zg
            h_ref[s, :, cs] = (gc * jax.nn.sigmoid(gc) * zu).astype(BF16)
        y = x + _dot(h_ref[s], wd_ref[...])
        if final:
            xo_ref[0, s * rs:(s + 1) * rs] = jnp.swapaxes(_rms(y, gf_ref[...]).reshape(nph, rs, D_MODEL), 0, 1)
        else:
            xo_ref[0, :, s * rs:(s + 1) * rs, :] = y.reshape(nph, rs, D_MODEL)

    @pl.when(i == ni - 1)
    def _():
        fc_ref[0] = carry_ref[...]


def _ffn_prompt(x, g, w_in, w_down, layer, cw, gf, final):
    bsz, nph, nchunk, _ = x.shape
    tm = nph * RC
    out_spec, out_shape = (pl.BlockSpec((1, RC, nph, D_MODEL), lambda b, i: (b, i, 0, 0)),
                           (bsz, nchunk, nph, D_MODEL)) if final else (_chunk_tile_spec(D_MODEL), x.shape)
    return pl.pallas_call(
        functools.partial(_ffn_prompt_kernel, final=final),
        grid=(bsz, nchunk // RC),
        in_specs=[_chunk_tile_spec(D_MODEL), _const_spec((1, D_MODEL)),
                  _layer_spec((D_MODEL, D_FF), layer, 0), _layer_spec((D_MODEL, D_FF), layer, 1),
                  _layer_spec((D_FF, D_MODEL), layer), _const_spec((3, D_FF)), _const_spec((1, D_MODEL))],
        out_specs=[out_spec, pl.BlockSpec((1, 2, D_FF), lambda b, i: (b, 0, 0))],
        out_shape=[jax.ShapeDtypeStruct(out_shape, F32), jax.ShapeDtypeStruct((bsz, 2, D_FF), F32)],
        scratch_shapes=[pltpu.VMEM((RC // RS, nph * RS, D_MODEL), BF16), pltpu.VMEM((2, D_FF), F32),
                        pltpu.VMEM((RC // RS, nph * RS, D_FF), BF16)],
        compiler_params=_params(("arbitrary", "arbitrary")),
        name="ffn_prompt",
    )(x, g, w_in, w_in, w_down, cw, gf)


def _odd_in_kernel(x_ref, g_ref, win_ref, wut_ref, ccw_ref, oc_ref, ut_ref, cc_ref, cin_ref, gb_ref):
    nph, nchunk = x_ref.shape[1], x_ref.shape[2]
    for s in range(nph // PH):
        rows = slice(s * PH * nchunk, (s + 1) * PH * nchunk)
        xn = _rms(x_ref[0, s * PH:(s + 1) * PH].reshape(PH * nchunk, D_MODEL), g_ref[...]).astype(BF16)
        z = _dot(xn, win_ref[...])
        cin_ref[rows, :] = z[:, 2 * D_C:] * z[:, :D_C]
        gb_ref[rows, :] = z[:, D_C:2 * D_C]
        ut = _dot_nt(wut_ref[...], xn)
        for k in range(PH):
            ut_ref[0, s * PH + k] = ut[:, k * nchunk:(k + 1) * nchunk].astype(BF16)

    zero = jnp.zeros((1, D_C), F32)
    blk = lambda ph: cin_ref[ph * nchunk:(ph + 1) * nchunk, :]
    s2 = _shift_rows(blk(nph - 2), zero)
    s1 = _shift_rows(blk(nph - 1), zero)
    for ph in range(nph):
        p1 = blk(ph - 1) if ph >= 1 else s1
        p2 = blk(ph - 2) if ph >= 2 else (s1 if ph == 1 else s2)
        conv = ccw_ref[0:1, :] * p2 + ccw_ref[1:2, :] * p1 + ccw_ref[2:3, :] * blk(ph)
        oc_ref[0, ph] = (gb_ref[ph * nchunk:(ph + 1) * nchunk, :] * conv).astype(BF16)
    last = nchunk - 1
    cc_ref[0, 0:1, :] = cin_ref[(nph - 2) * nchunk + last:(nph - 2) * nchunk + last + 1, :]
    cc_ref[0, 1:2, :] = cin_ref[(nph - 1) * nchunk + last:(nph - 1) * nchunk + last + 1, :]


def _odd_in(x, g, w_in, layer, wut, ccw):
    bsz, nph, nchunk, _ = x.shape
    seq = lambda *shape: pl.BlockSpec((1,) + shape, lambda b: (b,) + (0,) * len(shape))
    return pl.pallas_call(
        _odd_in_kernel,
        grid=(bsz,),
        in_specs=[seq(nph, nchunk, D_MODEL), _const_spec((1, D_MODEL)), _layer_spec((D_MODEL, 3 * D_C), layer),
                  _const_spec((D_D, D_MODEL)), _const_spec((3, D_C))],
        out_specs=[seq(nph, nchunk, D_C), seq(nph, D_D, nchunk), seq(2, D_C)],
        out_shape=[jax.ShapeDtypeStruct((bsz, nph, nchunk, D_C), BF16),
                   jax.ShapeDtypeStruct((bsz, nph, D_D, nchunk), BF16),
                   jax.ShapeDtypeStruct((bsz, 2, D_C), F32)],
        scratch_shapes=[pltpu.VMEM((nph * nchunk, D_C), F32), pltpu.VMEM((nph * nchunk, D_C), F32)],
        compiler_params=_params(("arbitrary",)),
        name="odd_in",
    )(x, g, w_in, wut, ccw)


def _s5_prompt_kernel(ut_ref, mt_ref, st_ref, ot_ref, a1_ref, a2_ref, yt_ref, hs_ref, h_ref):
    bsz, nph = ut_ref.shape[0], ut_ref.shape[1]
    nchunk = ut_ref.shape[3]
    n = nph * S5_GROUP
    swap = lambda v: jnp.concatenate([v[S5_STATE:], v[:S5_STATE]], axis=0)
    for gi in range(S5_GB):
        rows = slice(gi * S5_GROUP, (gi + 1) * S5_GROUP)
        ut = jnp.concatenate([ut_ref[b, :, rows, :].reshape(n, nchunk) for b in range(bsz)], axis=-1)
        x = _dot(st_ref[gi], ut)
        cidx = lax.broadcasted_iota(jnp.int32, x.shape, 1) & (nchunk - 1)
        a1, a2 = a1_ref[gi], a2_ref[gi]

        def shift(v, sh):
            rolled = jnp.concatenate([pltpu.roll(v[:, b * nchunk:(b + 1) * nchunk], sh, 1) for b in range(bsz)],
                                     axis=1)
            return jnp.where(cidx >= sh, rolled, 0.0)

        for k in range(S5_LEVELS):
            xs = shift(x, 1 << k)
            x = x + a1[:, k:k + 1] * xs + a2[:, k:k + 1] * swap(xs)
        h_ref[gi] = x.T
        hs_ref[gi] = h_ref[gi, pl.ds(nchunk - 1, bsz, stride=nchunk), :]
        hprev = shift(x, 1)
        yt = _dot(mt_ref[gi], ut) + _dot(ot_ref[gi], hprev.astype(BF16))
        for b in range(bsz):
            yt_ref[b, :, rows, :] = yt[:, b * nchunk:(b + 1) * nchunk].reshape(nph, S5_GROUP, nchunk)


def _s5_prompt(ut, mt, st, ot, a1t, a2t, odd):
    bsz, nph, _, nchunk = ut.shape
    n, lw = nph * S5_GROUP, 2 * S5_STATE
    steps = S5_GROUPS // S5_GB
    grp = lambda *shape: pl.BlockSpec((S5_GB,) + shape, lambda g: (g + odd * steps, 0, 0))
    act = pl.BlockSpec((bsz, nph, S5_GB * S5_GROUP, nchunk), lambda g: (0, 0, g, 0))
    return pl.pallas_call(
        _s5_prompt_kernel,
        grid=(steps,),
        in_specs=[act, grp(n, n), grp(lw, n), grp(n, lw), grp(lw, 8), grp(lw, 8)],
        out_specs=[act, pl.BlockSpec((S5_GB, bsz, lw), lambda g: (g, 0, 0))],
        out_shape=[jax.ShapeDtypeStruct(ut.shape, F32), jax.ShapeDtypeStruct((S5_GROUPS, bsz, lw), F32)],
        scratch_shapes=[pltpu.VMEM((S5_GB, bsz * nchunk, lw), F32)],
        compiler_params=_params(("arbitrary",)),
        name="s5_prompt",
    )(ut, mt, st, ot, a1t, a2t)


def _odd_out_kernel(x_ref, oc_ref, yt_ref, gw_ref, gb_ref, wout_ref, xo_ref):
    nph, nchunk = x_ref.shape[1], x_ref.shape[2]
    half = nph // 2
    tms = half * nchunk
    for s in range(2):
        ps = slice(s * half, (s + 1) * half)
        y = jnp.concatenate([yt_ref[0, k].T for k in range(s * half, (s + 1) * half)], axis=0)
        g = jax.nn.gelu(y)
        od = g * jax.nn.sigmoid(_dot(g.astype(BF16), gw_ref[...]) + gb_ref[...])
        out = (x_ref[0, ps].reshape(tms, D_MODEL) + _dot(oc_ref[0, ps].reshape(tms, D_C), wout_ref[0:D_C, :])
               + _dot(od.astype(BF16), wout_ref[D_C:, :]))
        xo_ref[0, ps] = out.reshape(half, nchunk, D_MODEL)


def _odd_out(x, oc, yt, glu_w, odd, gb, w_out, layer):
    bsz, nph, nchunk, _ = x.shape
    tok = lambda n: pl.BlockSpec((1, PHO, nchunk, n), lambda b, s: (b, s, 0, 0))
    return pl.pallas_call(
        _odd_out_kernel,
        grid=(bsz, nph // PHO),
        in_specs=[tok(D_MODEL), tok(D_C), pl.BlockSpec((1, PHO, D_D, nchunk), lambda b, s: (b, s, 0, 0)),
                  _layer_spec((D_D, D_D), odd), _const_spec((1, D_D)), _layer_spec((D_MODEL, D_MODEL), layer)],
        out_specs=tok(D_MODEL),
        out_shape=jax.ShapeDtypeStruct(x.shape, F32),
        compiler_params=_params(("arbitrary", "arbitrary")),
        name="odd_out",
    )(x, oc, yt, glu_w, gb, w_out)


def _even_sample_kernel(x_ref, g_ref, win_ref, wout_ref, alng_ref, alnb_ref, ws0_ref, bs0_ref,
                        bcw_ref, bcb_ref, blng_ref, blnb_ref, buf_ref, xo_ref, v_ref, nb_ref):
    x = x_ref[...]
    z = _dot(_rms(x, g_ref[...]).astype(BF16), win_ref[...])
    za = jax.nn.gelu(z[:, :2 * D_A])
    vs = []
    for h in range(A_HEADS):
        lo, hi = h * A_HEAD_DIM, (h + 1) * A_HEAD_DIM
        vs.append(_ln(za[:, D_A + lo:D_A + hi], alng_ref[:, lo:hi], alnb_ref[:, lo:hi]))
    v = jnp.concatenate(vs, axis=-1)
    v_ref[...] = v
    gate = ws0_ref[...] * v + bs0_ref[...]
    out_a = za[:, :D_A] * gate
    glu = z[:, 2 * D_A:2 * D_A + D_B] * jax.nn.sigmoid(z[:, 2 * D_A + D_B:])
    nk = B_CONV_WIDTH - 1
    buf = buf_ref[...]
    acc = bcb_ref[...] + bcw_ref[nk:nk + 1, :] * glu + jnp.sum(buf * bcw_ref[0:nk, :][None], axis=1)
    nb_ref[:, 0:nk - 1, :] = buf[:, 1:nk, :]
    nb_ref[:, nk - 1:nk, :] = glu[:, None, :]
    y = _ln(acc, blng_ref[...], blnb_ref[...])
    out_b = y * jax.nn.sigmoid(y)
    cat = jnp.concatenate([out_a, out_b], axis=-1).astype(BF16)
    xo_ref[...] = x + _dot(cat, wout_ref[...])


def _whole(shape):
    nd = len(shape)
    return pl.BlockSpec(shape, lambda *_: (0,) * nd)


def _call_whole(kernel_fn, name, args, out_shapes, scratch=(), specs=None):
    specs = specs or {}
    return pl.pallas_call(
        kernel_fn,
        grid=(1,),
        in_specs=[specs.get(i) or _const_spec(a.shape) for i, a in enumerate(args)],
        out_specs=[_whole(s.shape) for s in out_shapes],
        out_shape=out_shapes,
        scratch_shapes=list(scratch),
        compiler_params=_params(("arbitrary",)),
        name=name,
    )(*args)


def _odd_sample_kernel(x_ref, g_ref, wina_ref, wut_ref, wout_ref, ccw_ref, buf_ref, h_ref, p1_ref, p2_ref,
                       bbar_ref, ccn_ref, dsk_ref, gw_ref, gb_ref,
                       xo_ref, nb_ref, nh_ref, bd_ref, cd_ref):
    lw = 2 * S5_STATE
    bd_ref[...] = jnp.zeros_like(bd_ref)
    cd_ref[...] = jnp.zeros_like(cd_ref)
    for gi in range(S5_GROUPS):
        r0, l0 = gi * S5_GROUP, gi * lw
        bd_ref[r0:r0 + S5_GROUP, l0:l0 + lw] = bbar_ref[gi]
        cd_ref[r0:r0 + S5_GROUP, l0:l0 + lw] = ccn_ref[gi]
    x = x_ref[...]
    xn = _rms(x, g_ref[...]).astype(BF16)
    z = _dot(xn, wina_ref[...])
    u = _dot_nt(xn, wut_ref[...])
    cin = z[:, 2 * D_C:] * z[:, :D_C]
    buf = buf_ref[...]
    conv = jnp.sum(buf * ccw_ref[0:2, :][None], axis=1) + ccw_ref[2:3, :] * cin
    nb_ref[:, 0:1, :] = buf[:, 1:2, :]
    nb_ref[:, 1:2, :] = cin[:, None, :]
    out_c = z[:, D_C:2 * D_C] * conv
    h = h_ref[...]
    hswap = jnp.concatenate([pltpu.roll(h[:, gi * lw:(gi + 1) * lw], S5_STATE, 1) for gi in range(S5_GROUPS)],
                            axis=-1)
    nh = p1_ref[...] * h + p2_ref[...] * hswap + _dot(u.astype(BF16), bd_ref[...])
    nh_ref[...] = nh
    y = _dot_nt(nh.astype(BF16), cd_ref[...]) + dsk_ref[...] * u
    g = jax.nn.gelu(y)
    od = g * jax.nn.sigmoid(_dot(g.astype(BF16), gw_ref[...]) + gb_ref[...])
    cat = jnp.concatenate([out_c, od], axis=-1).astype(BF16)
    xo_ref[...] = x + _dot(cat, wout_ref[...])


def _ffn_sample_kernel(x_ref, g_ref, wg_ref, wu_ref, wd_ref, cw_ref, b_ref, gf_ref,
                       xo_ref, n_ref, xn_ref, acc_ref, *, final):
    j = pl.program_id(0)

    @pl.when(j == 0)
    def _():
        xn_ref[...] = _rms(x_ref[...], g_ref[...]).astype(BF16)
        acc_ref[...] = jnp.zeros_like(acc_ref)

    zg = _dot(xn_ref[...], wg_ref[...])
    zu = _dot(xn_ref[...], wu_ref[...])
    buf = b_ref[...]
    gc = jnp.sum(buf * cw_ref[0:2, :][None], axis=1) + cw_ref[2:3, :] * zg
    n_ref[:, 0:1, :] = buf[:, 1:2, :]
    n_ref[:, 1:2, :] = zg[:, None, :]
    acc_ref[...] += _dot((gc * jax.nn.sigmoid(gc) * zu).astype(BF16), wd_ref[...])

    @pl.when(j == pl.num_programs(0) - 1)
    def _():
        y = x_ref[...] + acc_ref[...]
        if final:
            y = _rms(y, gf_ref[...])
        xo_ref[...] = y


def _ffn_sample(x, g, w_in, w_down, layer, cw, buf, gf, final):
    n = x.shape[0]
    nc = D_FF // FS_CW
    state = pl.BlockSpec((n, 2, FS_CW), lambda j: (0, 0, j))
    return pl.pallas_call(
        functools.partial(_ffn_sample_kernel, final=final),
        grid=(nc,),
        in_specs=[_whole((n, D_MODEL)), _whole((1, D_MODEL)),
                  pl.BlockSpec((None, D_MODEL, FS_CW), lambda j: (layer, 0, j)),
                  pl.BlockSpec((None, D_MODEL, FS_CW), lambda j: (layer, 0, j + nc)),
                  pl.BlockSpec((None, FS_CW, D_MODEL), lambda j: (layer, j, 0)),
                  pl.BlockSpec((3, FS_CW), lambda j: (0, j)),
                  pl.BlockSpec((None, n, 2, FS_CW), lambda j: (layer, 0, 0, j)), _whole((1, D_MODEL))],
        out_specs=[_whole((n, D_MODEL)), state],
        out_shape=[jax.ShapeDtypeStruct((n, D_MODEL), F32), jax.ShapeDtypeStruct((n, 2, D_FF), F32)],
        scratch_shapes=[pltpu.VMEM((n, D_MODEL), BF16), pltpu.VMEM((n, D_MODEL), F32)],
        compiler_params=_params(("arbitrary",)),
        name="ffn_sample",
    )(x, g, w_in, w_in, w_down, cw, buf, gf)


_S5_POW = np.concatenate([np.arange(1 - S5_CHUNK, S5_CHUNK + 1), S5_CHUNK * 2 ** np.arange(8)]).astype(np.float32)
_S5_POW_ROWS = len(_S5_POW)


def _s5_prep_kernel(pw_ref, *refs):
    for gi in range(S5_PREP_GB):
        _s5_prep_group(pw_ref, *[r.at[pl.ds(gi, 1)] for r in refs])


def _s5_prep_group(pw_ref, lr_ref, li_ref, ldt_ref, bb_ref, bbs_ref, cc_ref, ccs_ref, dd_ref,
                   mt_ref, st_ref, ot_ref, a1_ref, a2_ref, bbar_ref, ccn_ref, p1_ref, p2_ref):
    lw, lc = 2 * S5_STATE, S5_CHUNK
    lane = lax.broadcasted_iota(jnp.int32, (1, lw), 1)
    sgn = jnp.where(lane < S5_STATE, -1.0, 1.0)
    lr, li = lr_ref[0], li_ref[0]
    dt = jnp.exp(ldt_ref[0])
    er, ei = lr * dt, li * dt
    jm = pw_ref[...]
    mag = jnp.exp(jm * er)
    cr, ci = mag * jnp.cos(jm * ei), mag * jnp.sin(jm * ei)

    def scale(j, x, xs):
        i = j + lc - 1
        return cr[i:i + 1] * x + (sgn * ci[i:i + 1]) * xs

    one = lc
    nr_, ni_ = cr[one:one + 1] - 1.0, ci[one:one + 1]
    den = lr * lr + li * li
    cfr = (nr_ * lr + ni_ * li) / den
    cfi = (ni_ * lr - nr_ * li) / den
    bb, bbs, cc, ccs = bb_ref[0], bbs_ref[0], cc_ref[0], ccs_ref[0]
    bbar = cfr * bb + (sgn * cfi) * bbs
    bbars = cfr * bbs - (sgn * cfi) * bb
    cat = lambda blocks: jnp.concatenate(blocks, axis=0)
    ymat = cat([scale(t, cc, ccs) for t in range(lc)])
    xmat = cat([scale(-t, bbar, bbars) * (-sgn) for t in range(lc)])
    mt = _dot_nt(ymat, xmat, precision=lax.Precision.HIGHEST)
    n = lc * S5_GROUP
    row = lax.broadcasted_iota(jnp.int32, (n, n), 0)
    col = lax.broadcasted_iota(jnp.int32, (n, n), 1)
    mt = jnp.where((row >> 4) >= (col >> 4), mt, 0.0)
    mt = mt + jnp.where(row == col, dd_ref[0], 0.0)
    mt_ref[0] = mt.astype(BF16)
    smat = cat([scale(lc - 1 - t, bbar, bbars) for t in range(lc)])
    st_ref[0] = smat.T.astype(BF16)
    ot_ref[0] = cat([scale(t + 1, cc, ccs) * (-sgn) for t in range(lc)]).astype(BF16)
    a1_ref[0] = cr[2 * lc:2 * lc + 8].T
    a2_ref[0] = (sgn * ci[2 * lc:2 * lc + 8]).T
    bbar_ref[0] = bbar.astype(BF16)
    ccn_ref[0] = (cc * (-sgn)).astype(BF16)
    p1_ref[0] = cr[one:one + 1]
    p2_ref[0] = sgn * ci[one:one + 1]


def _s5_prepare(lam_re, lam_im, log_dt, b_re, b_im, c_re, c_im, d_skip):
    flat = lambda a: a.reshape((-1,) + a.shape[2:])
    lam_re, lam_im, log_dt, b_re, b_im, c_re, c_im, d_skip = map(
        flat, (lam_re, lam_im, log_dt, b_re, b_im, c_re, c_im, d_skip))
    ng, lw, n = lam_re.shape[0], 2 * S5_STATE, S5_CHUNK * S5_GROUP
    two = lambda a: jnp.concatenate([a, a], axis=-1)[:, None, :]
    pack = lambda a, b: jnp.concatenate([a, b], axis=-1)
    bt_re, bt_im = b_re.transpose(0, 2, 1), b_im.transpose(0, 2, 1)
    args = (jnp.asarray(np.tile(_S5_POW[:, None], (1, lw))), two(lam_re), two(lam_im),
            jnp.broadcast_to(log_dt[:, None, None], (ng, 1, lw)),
            pack(bt_re, bt_im), pack(bt_im, bt_re), pack(c_re, c_im), pack(c_im, c_re),
            jnp.tile(d_skip, (1, S5_CHUNK))[:, None, :])
    blk = lambda *shape: pl.BlockSpec((S5_PREP_GB,) + shape, lambda s: (s, 0, 0))
    shapes = [((n, n), BF16), ((lw, n), BF16), ((n, lw), BF16), ((lw, 8), F32), ((lw, 8), F32),
              ((S5_GROUP, lw), BF16), ((S5_GROUP, lw), BF16), ((1, lw), F32), ((1, lw), F32)]
    outs = pl.pallas_call(
        _s5_prep_kernel,
        grid=(ng // S5_PREP_GB,),
        in_specs=[_const_spec((_S5_POW_ROWS, lw))] + [blk(*a.shape[1:]) for a in args[1:]],
        out_specs=[blk(*sh) for sh, _ in shapes],
        out_shape=[jax.ShapeDtypeStruct((ng,) + sh, dtp) for sh, dtp in shapes],
        compiler_params=_params(("arbitrary",)),
        name="s5_prep",
    )(*args)
    return dict(zip(("mt", "st", "ot", "a1t", "a2t", "bbar", "ccn", "p1", "p2"), outs))


def _cast_kernel(w_ref, o_ref, *, transpose):
    w = w_ref[...]
    o_ref[...] = (w.T if transpose else w).astype(BF16)


def _to_bf16_all(w, tk):
    nl, k, n = w.shape
    spec = pl.BlockSpec((None, tk, n), lambda l, i: (l, i, 0))
    return pl.pallas_call(
        functools.partial(_cast_kernel, transpose=False),
        grid=(nl, k // tk),
        in_specs=[spec],
        out_specs=spec,
        out_shape=jax.ShapeDtypeStruct(w.shape, BF16),
        compiler_params=_params(("arbitrary", "arbitrary")),
        name="to_bf16_all",
    )(w)


def _to_bf16(w, layer, col_block=0, ncols=None, transpose=False):
    _, k, n = w.shape
    ncols = ncols or n
    tk = 256 if k % 256 == 0 else k
    out_block, out_map, out_shape = ((ncols, tk), lambda i: (0, i), (ncols, k)) if transpose else \
        ((tk, ncols), lambda i: (i, 0), (k, ncols))
    return pl.pallas_call(
        functools.partial(_cast_kernel, transpose=transpose),
        grid=(k // tk,),
        in_specs=[pl.BlockSpec((None, tk, ncols), lambda i: (layer, i, col_block))],
        out_specs=pl.BlockSpec(out_block, out_map),
        out_shape=jax.ShapeDtypeStruct(out_shape, BF16),
        compiler_params=_params(("arbitrary",)),
        name="to_bf16",
    )(w)


def kernel(x_prompt, x_sample, state_conv_b, state_conv_c, state_ssm_re, state_ssm_im, state_ffn_conv, norm_mix, norm_ffn, norm_final, w_mix_in, w_mix_out, a_ln_g, a_ln_b, a_ws, a_bs, b_conv_w, b_conv_b, b_ln_g, b_ln_b, c_conv_w, s5_lam_re, s5_lam_im, s5_log_dt, s5_b_re, s5_b_im, s5_c_re, s5_c_im, s5_d, s5_glu_w, s5_glu_b, ffn_w_in, ffn_conv_w, ffn_w_down):
    bsz, seq, _ = x_prompt.shape
    nsmp = x_sample.shape[0]
    nph, nchunk = S5_CHUNK, seq // S5_CHUNK
    n_odd = s5_lam_re.shape[0]
    row = lambda a: a.reshape(1, -1)
    xp = x_prompt.reshape(bsz, nchunk, nph, D_MODEL)
    xs = x_sample.reshape(nsmp, D_MODEL)
    gf = row(norm_final)
    v_rows, cb_p, cb_s, cc_p, cc_s, re_p, re_s, im_p, im_s, fc_p, fc_s = ([] for _ in range(11))
    q = CHUNK // nph
    sds = jax.ShapeDtypeStruct

    w_in_bf = _to_bf16_all(w_mix_in, D_MODEL)
    w_out_bf = _to_bf16_all(w_mix_out, D_MODEL)
    ffn_in_bf = _to_bf16_all(ffn_w_in, D_MODEL // 2)
    ffn_down_bf = _to_bf16_all(ffn_w_down, D_FF // 2)
    glu_bf = _to_bf16_all(s5_glu_w, D_D)
    p = _s5_prepare(s5_lam_re, s5_lam_im, s5_log_dt, s5_b_re, s5_b_im, s5_c_re, s5_c_im, s5_d)
    p1_rows, p2_rows = p["p1"].reshape(n_odd, 1, -1), p["p2"].reshape(n_odd, 1, -1)

    for l in range(DEPTH):
        g_mix = row(norm_mix[l])
        if l % 2 == 0:
            e = l // 2
            alng, alnb = row(a_ln_g[e]), row(a_ln_b[e])
            ws_p = a_ws[e].reshape(A_HEADS, q, nph, q, nph).transpose(0, 2, 1, 4, 3).reshape(A_HEADS, CHUNK, CHUNK)
            bs_full = jnp.repeat(a_bs[e].T, A_HEAD_DIM, axis=1)
            bs_p = bs_full.reshape(q, nph, D_A).transpose(1, 0, 2).reshape(CHUNK, D_A)
            common = (row(b_conv_b[e]), row(b_ln_g[e]), row(b_ln_b[e]))
            xp, cb = _even_prompt(xp, g_mix, w_in_bf, w_out_bf, l, alng, alnb, ws_p, bs_p, b_conv_w[e], *common,
                                  natural_in=(l == 0))
            cb_p.append(cb)
            ws0 = row(jnp.repeat(a_ws[e][:, 0, 0], A_HEAD_DIM))
            bs0 = row(jnp.repeat(a_bs[e][:, 0], A_HEAD_DIM))
            xs, v, nb = _call_whole(
                _even_sample_kernel, "even_sample",
                (xs, g_mix, w_in_bf, w_out_bf, alng, alnb, ws0, bs0, b_conv_w[e], *common, state_conv_b),
                [sds((nsmp, D_MODEL), F32), sds((nsmp, D_A), F32), sds(state_conv_b.shape[1:], F32)],
                specs={2: _layer_spec((D_MODEL, D_IN), l), 3: _layer_spec((D_MODEL, D_MODEL), l),
                       12: _lead_spec(state_conv_b.shape[1:], e)})
            v_rows.append(v.reshape(nsmp, 1, D_A))
            cb_s.append(nb)
        else:
            o = l // 2
            wut = _to_bf16(w_mix_in, l, 3, D_D, transpose=True)
            gb = row(s5_glu_b[o])
            oc, ut, cc = _odd_in(xp, g_mix, w_in_bf, l, wut, c_conv_w[o])
            cc_p.append(cc)
            yt, hs = _s5_prompt(ut, p["mt"], p["st"], p["ot"], p["a1t"], p["a2t"], o)
            hs = hs.transpose(1, 0, 2)
            re_p.append(hs[..., :S5_STATE])
            im_p.append(hs[..., S5_STATE:])
            xp = _odd_out(xp, oc, yt, glu_bf, o, gb, w_out_bf, l)
            hin =jnp.concatenate([state_ssm_re[o], state_ssm_im[o]], axis=-1).reshape(nsmp, -1)
            nstate = S5_GROUPS * 2 * S5_STATE
            grp_spec = pl.BlockSpec((S5_GROUPS, S5_GROUP, 2 * S5_STATE), lambda *_: (o, 0, 0),
                                    pipeline_mode=pl.Buffered(1))
            xs, nbc, nh = _call_whole(
                _odd_sample_kernel, "odd_sample",
                (xs, g_mix, w_in_bf, wut, w_out_bf, c_conv_w[o], state_conv_c, hin, p1_rows[o], p2_rows[o],
                 p["bbar"], p["ccn"], row(s5_d[o]), glu_bf, gb),
                [sds((nsmp, D_MODEL), F32), sds(state_conv_c.shape[1:], F32), sds(hin.shape, F32)],
                scratch=[pltpu.VMEM((D_D, nstate), BF16), pltpu.VMEM((D_D, nstate), BF16)],
                specs={2: _layer_spec((D_MODEL, 3 * D_C), l), 4: _layer_spec((D_MODEL, D_MODEL), l),
                       6: _lead_spec(state_conv_c.shape[1:], o),
                       10: grp_spec, 11: grp_spec, 13: _layer_spec((D_D, D_D), o)})
            cc_s.append(nbc)
            nh = nh.reshape(nsmp, S5_GROUPS, 2 * S5_STATE)
            re_s.append(nh[..., :S5_STATE])
            im_s.append(nh[..., S5_STATE:])
        final = l == DEPTH - 1
        g_ffn = row(norm_ffn[l])
        xp, fc = _ffn_prompt(xp, g_ffn, ffn_in_bf, ffn_down_bf, l, ffn_conv_w[l], gf, final)
        fc_p.append(fc)
        xs, nfc = _ffn_sample(xs, g_ffn, ffn_in_bf, ffn_down_bf, l, ffn_conv_w[l], state_ffn_conv, gf, final)
        fc_s.append(nfc)

    st = jnp.stack
    return (xp.reshape(bsz, seq, D_MODEL), xs.reshape(nsmp, 1, D_MODEL), st(v_rows), st(cb_p), st(cb_s),
            st(cc_p), st(cc_s), st(re_p), st(re_s), st(im_p), st(im_s), st(fc_p), st(fc_s))
```

```python
import functools

import numpy as np
import jax
import jax.numpy as jnp
from jax import lax
from jax.experimental import pallas as pl
from jax.experimental.pallas import tpu as pltpu

D_MODEL = 1024
DEPTH = 4
D_A = 512
D_B = 512
D_C = 512
D_D = 512
D_IN = 2048
A_HEADS = 4
A_HEAD_DIM = 128
CHUNK = 128
B_CONV_WIDTH = 31
S5_GROUP = 16
S5_GROUPS = 32
S5_STATE = 64
D_FF = 2816
EPS = 1e-6

S5_CHUNK = 16
S5_LEVELS = 7
S5_PREP_GB = 8
S5_GB = 2
PHO = 8
RC = 64
RS = 32
PH = 4
FF_CW = 256
FS_CW = 256
SUB = 8
VMEM_LIMIT = 56 * 1024 * 1024

F32 = jnp.float32
BF16 = jnp.bfloat16


def _rms(x, g):
    return x * lax.rsqrt(jnp.mean(x * x, axis=-1, keepdims=True) + EPS) * g


def _ln(x, g, b):
    mu = jnp.mean(x, axis=-1, keepdims=True)
    xc = x - mu
    var = jnp.mean(xc * xc, axis=-1, keepdims=True)
    return xc * lax.rsqrt(var + EPS) * g + b


def _dot(a, b):
    return jnp.dot(a, b, preferred_element_type=F32)


def _dot_nt(a, b, precision=None):
    return lax.dot_general(a, b, (((1,), (1,)), ((), ())), precision=precision, preferred_element_type=F32)


def _const_spec(shape):
    nd = len(shape)
    return pl.BlockSpec(shape, lambda *_: (0,) * nd, pipeline_mode=pl.Buffered(1))


def _layer_spec(shape2d, layer, col_block=0):
    return pl.BlockSpec((None,) + tuple(shape2d), lambda *_: (layer, 0, col_block), pipeline_mode=pl.Buffered(1))


def _lead_spec(shape, idx):
    nd = len(shape)
    return pl.BlockSpec((None,) + tuple(shape), lambda *_: (idx,) + (0,) * nd, pipeline_mode=pl.Buffered(1))


def _params(sem):
    return pltpu.CompilerParams(dimension_semantics=sem, vmem_limit_bytes=VMEM_LIMIT)


def _shift_rows(blk, first_row):
    row0 = lax.broadcasted_iota(jnp.int32, blk.shape, 0) == 0
    return jnp.where(row0, first_row, pltpu.roll(blk, 1, 0))


def _chunk_tile_spec(n):
    return pl.BlockSpec((1, S5_CHUNK, RC, n), lambda b, i: (b, 0, i, 0))


def _even_prompt_kernel(x_ref, g_ref, win_ref, wout_ref, alng_ref, alnb_ref, ws_ref, bs_ref,
                        bcw_ref, bcb_ref, blng_ref, blnb_ref, xo_ref, cb_ref,
                        z_ref, p_ref, gsh_ref, oa_ref, ob_ref, wb_ref, *, natural_in):
    i = pl.program_id(1)
    ni = pl.num_programs(1)
    nph, rc = xo_ref.shape[1], xo_ref.shape[2]
    rs = RS
    nsub, tms = rc // rs, nph * rs

    @pl.when(jnp.logical_and(pl.program_id(0) == 0, i == 0))
    def _():
        for k in range(B_CONV_WIDTH):
            wb_ref[k] = jnp.broadcast_to(bcw_ref[k:k + 1, :], (SUB, D_B))

    @pl.when(i == 0)
    def _():
        p_ref[0, :, 0:SUB, :] = jnp.zeros((nph, SUB, D_B), F32)

    def load(s):
        xt = jnp.swapaxes(x_ref[0, s * rs:(s + 1) * rs], 0, 1) if natural_in else x_ref[0, :, s * rs:(s + 1) * rs, :]
        return xt.reshape(tms, D_MODEL)

    xs = [load(s) for s in range(nsub)]
    for s in range(nsub):
        z_ref[s] = _dot(_rms(xs[s], g_ref[...]).astype(BF16), win_ref[...])

    q = CHUNK // nph
    r = lax.broadcasted_iota(jnp.int32, (CHUNK, CHUNK), 0)
    c = lax.broadcasted_iota(jnp.int32, (CHUNK, CHUNK), 1)
    pos = lambda k: (k & (q - 1)) * nph + (k >> (q.bit_length() - 1))
    keep = pos(r) >= pos(c)
    wm = [jnp.where(keep, ws_ref[h], 0.0).astype(BF16) for h in range(A_HEADS)]

    for s in range(nsub):
        if s > 0:
            p_ref[s, :, SUB - 2:SUB, :] = p_ref[s - 1, :, SUB + rs - 2:SUB + rs, :]
        for m in range(rs // q):
            rows = [ph * rs + q * m for ph in range(nph)]
            za = jax.nn.gelu(jnp.concatenate([z_ref[s, r0:r0 + q, 0:2 * D_A] for r0 in rows], axis=0))
            for h in range(A_HEADS):
                lo, hi = h * A_HEAD_DIM, (h + 1) * A_HEAD_DIM
                vh = _ln(za[:, D_A + lo:D_A + hi], alng_ref[:, lo:hi], alnb_ref[:, lo:hi])
                oa = za[:, lo:hi] * (_dot(wm[h], vh.astype(BF16)) + bs_ref[:, lo:hi])
                for ph, r0 in enumerate(rows):
                    oa_ref[s, r0:r0 + q, lo:hi] = oa[ph * q:(ph + 1) * q]

        for ph in range(nph):
            zb = z_ref[s, ph * rs:(ph + 1) * rs, 2 * D_A:]
            p_ref[s, ph, SUB:SUB + rs, :] = zb[:, :D_B] * jax.nn.sigmoid(zb[:, D_B:])
        for ph in range(nph):
            gsh_ref[s, 0, ph] = p_ref[s, ph, SUB - 1:SUB - 1 + rs, :]
            gsh_ref[s, 1, ph] = p_ref[s, ph, SUB - 2:SUB - 2 + rs, :]
        for ph in range(nph):
            acc = jnp.zeros((rs // SUB, SUB, D_B), F32)
            for j in range(B_CONV_WIDTH):
                src_ph = (ph - j) % nph
                back = (j - ph + nph - 1) // nph if j > ph else 0
                src = p_ref[s, src_ph, SUB:SUB + rs, :] if back == 0 else gsh_ref[s, back - 1, src_ph]
                acc = acc + wb_ref[B_CONV_WIDTH - 1 - j][None] * src.reshape(rs // SUB, SUB, D_B)
            y = _ln(acc.reshape(rs, D_B) + bcb_ref[...], blng_ref[...], blnb_ref[...])
            ob_ref[s, ph * rs:(ph + 1) * rs, :] = (y * jax.nn.sigmoid(y)).astype(BF16)

        out = (xs[s] + _dot(oa_ref[s].astype(BF16), wout_ref[0:D_A, :]) + _dot(ob_ref[s], wout_ref[D_A:, :]))
        xo_ref[0, :, s * rs:(s + 1) * rs, :] = out.reshape(nph, rs, D_MODEL)

    last = nsub - 1

    @pl.when(i == ni - 1)
    def _():
        n_out = B_CONV_WIDTH - 1
        for k in range(n_out):
            back = n_out - 1 - k
            ph, cl = (nph - 1 - back) % nph, rs - 1 - back // nph
            cb_ref[0, k:k + 1, :] = p_ref[last, ph, SUB + cl:SUB + cl + 1, :]

    p_ref[0, :, SUB - 2:SUB, :] = p_ref[last, :, SUB + rs - 2:SUB + rs, :]


def _even_prompt(x, g, w_in, w_out, layer, alng, alnb, ws, bs, bcw, bcb, blng, blnb, natural_in):
    bsz, nph, nchunk = x.shape[0], S5_CHUNK, x.shape[1] * x.shape[2] // S5_CHUNK
    row = lambda n: _const_spec((1, n))
    nsub, tms = RC // RS, nph * RS
    x_spec = pl.BlockSpec((1, RC, nph, D_MODEL), lambda b, i: (b, i, 0, 0)) if natural_in else \
        _chunk_tile_spec(D_MODEL)
    return pl.pallas_call(
        functools.partial(_even_prompt_kernel, natural_in=natural_in),
        grid=(bsz, nchunk // RC),
        in_specs=[x_spec, row(D_MODEL),
                  _layer_spec((D_MODEL, D_IN), layer), _layer_spec((D_MODEL, D_MODEL), layer),
                  row(D_A), row(D_A), _const_spec((A_HEADS, CHUNK, CHUNK)), _const_spec((CHUNK, D_A)),
                  _const_spec((B_CONV_WIDTH, D_B)), row(D_B), row(D_B), row(D_B)],
        out_specs=[_chunk_tile_spec(D_MODEL),
                   pl.BlockSpec((1, B_CONV_WIDTH - 1, D_B), lambda b, i: (b, 0, 0))],
        out_shape=[jax.ShapeDtypeStruct((bsz, nph, nchunk, D_MODEL), F32),
                   jax.ShapeDtypeStruct((bsz, B_CONV_WIDTH - 1, D_B), F32)],
        scratch_shapes=[pltpu.VMEM((nsub, tms, D_IN), F32), pltpu.VMEM((nsub, nph, RS + SUB, D_B), F32),
                        pltpu.VMEM((nsub, 2, nph, RS, D_B), F32), pltpu.VMEM((nsub, tms, D_A), F32),
                        pltpu.VMEM((nsub, tms, D_B), BF16), pltpu.VMEM((B_CONV_WIDTH, SUB, D_B), F32)],
        compiler_params=_params(("arbitrary", "arbitrary")),
        name="even_prompt",
    )(x, g, w_in, w_out, alng, alnb, ws, bs, bcw, bcb, blng, blnb)


def _ffn_prompt_kernel(x_ref, g_ref, wg_ref, wu_ref, wd_ref, cw_ref, gf_ref, xo_ref, fc_ref,
                       xn_ref, carry_ref, h_ref, *, final):
    i = pl.program_id(1)
    ni = pl.num_programs(1)
    nph, rc = x_ref.shape[1], x_ref.shape[2]
    rs = RS
    tms = nph * rs

    @pl.when(i == 0)
    def _():
        carry_ref[...] = jnp.zeros_like(carry_ref)

    lo2, lo1 = (nph - 2) * rs, (nph - 1) * rs
    for s in range(rc // rs):
        x = x_ref[0, :, s * rs:(s + 1) * rs, :].reshape(tms, D_MODEL)
        xn_ref[s] = _rms(x, g_ref[...]).astype(BF16)
        for c0 in range(0, D_FF, FF_CW):
            cs = slice(c0, c0 + FF_CW)
            zg = _dot(xn_ref[s], wg_ref[:, cs])
            zu = _dot(xn_ref[s], wu_ref[:, cs])
            s2 = _shift_rows(zg[lo2:lo1], carry_ref[0:1, cs])
            s1 = _shift_rows(zg[lo1:], carry_ref[1:2, cs])
            carry_ref[0:1, cs] = zg[lo1 - 1:lo1]
            carry_ref[1:2, cs] = zg[tms - 1:tms]
            z1 = jnp.concatenate([s1, zg[:lo1]], axis=0)
            z2 = jnp.concatenate([s2, s1, zg[:lo2]], axis=0)
            gc = cw_ref[0:1, cs] * z2 + cw_ref[1:2, cs] * z1 + cw_ref[2:3, cs] * zg
            h_ref[s, :, cs] = (gc * jax.nn.sigmoid(gc) * zu).astype(BF16)
        y = x + _dot(h_ref[s], wd_ref[...])
        if final:
            xo_ref[0, s * rs:(s + 1) * rs] = jnp.swapaxes(_rms(y, gf_ref[...]).reshape(nph, rs, D_MODEL), 0, 1)
        else:
            xo_ref[0, :, s * rs:(s + 1) * rs, :] = y.reshape(nph, rs, D_MODEL)

    @pl.when(i == ni - 1)
    def _():
        fc_ref[0] = carry_ref[...]


def _ffn_prompt(x, g, w_in, w_down, layer, cw, gf, final):
    bsz, nph, nchunk, _ = x.shape
    tm = nph * RC
    out_spec, out_shape = (pl.BlockSpec((1, RC, nph, D_MODEL), lambda b, i: (b, i, 0, 0)),
                           (bsz, nchunk, nph, D_MODEL)) if final else (_chunk_tile_spec(D_MODEL), x.shape)
    return pl.pallas_call(
        functools.partial(_ffn_prompt_kernel, final=final),
        grid=(bsz, nchunk // RC),
        in_specs=[_chunk_tile_spec(D_MODEL), _const_spec((1, D_MODEL)),
                  _layer_spec((D_MODEL, D_FF), layer, 0), _layer_spec((D_MODEL, D_FF), layer, 1),
                  _layer_spec((D_FF, D_MODEL), layer), _const_spec((3, D_FF)), _const_spec((1, D_MODEL))],
        out_specs=[out_spec, pl.BlockSpec((1, 2, D_FF), lambda b, i: (b, 0, 0))],
        out_shape=[jax.ShapeDtypeStruct(out_shape, F32), jax.ShapeDtypeStruct((bsz, 2, D_FF), F32)],
        scratch_shapes=[pltpu.VMEM((RC // RS, nph * RS, D_MODEL), BF16), pltpu.VMEM((2, D_FF), F32),
                        pltpu.VMEM((RC // RS, nph * RS, D_FF), BF16)],
        compiler_params=_params(("arbitrary", "arbitrary")),
        name="ffn_prompt",
    )(x, g, w_in, w_in, w_down, cw, gf)


def _odd_in_kernel(x_ref, g_ref, win_ref, wut_ref, ccw_ref, oc_ref, ut_ref, cc_ref, cin_ref, gb_ref):
    nph, nchunk = x_ref.shape[1], x_ref.shape[2]
    for s in range(nph // PH):
        rows = slice(s * PH * nchunk, (s + 1) * PH * nchunk)
        xn = _rms(x_ref[0, s * PH:(s + 1) * PH].reshape(PH * nchunk, D_MODEL), g_ref[...]).astype(BF16)
        z = _dot(xn, win_ref[...])
        cin_ref[rows, :] = z[:, 2 * D_C:] * z[:, :D_C]
        gb_ref[rows, :] = z[:, D_C:2 * D_C]
        ut = _dot_nt(wut_ref[...], xn)
        for k in range(PH):
            ut_ref[0, s * PH + k] = ut[:, k * nchunk:(k + 1) * nchunk].astype(BF16)

    zero = jnp.zeros((1, D_C), F32)
    blk = lambda ph: cin_ref[ph * nchunk:(ph + 1) * nchunk, :]
    s2 = _shift_rows(blk(nph - 2), zero)
    s1 = _shift_rows(blk(nph - 1), zero)
    for ph in range(nph):
        p1 = blk(ph - 1) if ph >= 1 else s1
        p2 = blk(ph - 2) if ph >= 2 else (s1 if ph == 1 else s2)
        conv = ccw_ref[0:1, :] * p2 + ccw_ref[1:2, :] * p1 + ccw_ref[2:3, :] * blk(ph)
        oc_ref[0, ph] = (gb_ref[ph * nchunk:(ph + 1) * nchunk, :] * conv).astype(BF16)
    last = nchunk - 1
    cc_ref[0, 0:1, :] = cin_ref[(nph - 2) * nchunk + last:(nph - 2) * nchunk + last + 1, :]
    cc_ref[0, 1:2, :] = cin_ref[(nph - 1) * nchunk + last:(nph - 1) * nchunk + last + 1, :]


def _odd_in(x, g, w_in, layer, wut, ccw):
    bsz, nph, nchunk, _ = x.shape
    seq = lambda *shape: pl.BlockSpec((1,) + shape, lambda b: (b,) + (0,) * len(shape))
    return pl.pallas_call(
        _odd_in_kernel,
        grid=(bsz,),
        in_specs=[seq(nph, nchunk, D_MODEL), _const_spec((1, D_MODEL)), _layer_spec((D_MODEL, 3 * D_C), layer),
                  _const_spec((D_D, D_MODEL)), _const_spec((3, D_C))],
        out_specs=[seq(nph, nchunk, D_C), seq(nph, D_D, nchunk), seq(2, D_C)],
        out_shape=[jax.ShapeDtypeStruct((bsz, nph, nchunk, D_C), BF16),
                   jax.ShapeDtypeStruct((bsz, nph, D_D, nchunk), BF16),
                   jax.ShapeDtypeStruct((bsz, 2, D_C), F32)],
        scratch_shapes=[pltpu.VMEM((nph * nchunk, D_C), F32), pltpu.VMEM((nph * nchunk, D_C), F32)],
        compiler_params=_params(("arbitrary",)),
        name="odd_in",
    )(x, g, w_in, wut, ccw)


def _s5_prompt_kernel(ut_ref, mt_ref, st_ref, ot_ref, a1_ref, a2_ref, yt_ref, hs_ref, h_ref):
    bsz, nph = ut_ref.shape[0], ut_ref.shape[1]
    nchunk = ut_ref.shape[3]
    n = nph * S5_GROUP
    swap = lambda v: jnp.concatenate([v[S5_STATE:], v[:S5_STATE]], axis=0)
    for gi in range(S5_GB):
        rows = slice(gi * S5_GROUP, (gi + 1) * S5_GROUP)
        ut = jnp.concatenate([ut_ref[b, :, rows, :].reshape(n, nchunk) for b in range(bsz)], axis=-1)
        x = _dot(st_ref[gi], ut)
        cidx = lax.broadcasted_iota(jnp.int32, x.shape, 1) & (nchunk - 1)
        a1, a2 = a1_ref[gi], a2_ref[gi]

        def shift(v, sh):
            rolled = jnp.concatenate([pltpu.roll(v[:, b * nchunk:(b + 1) * nchunk], sh, 1) for b in range(bsz)],
                                     axis=1)
            return jnp.where(cidx >= sh, rolled, 0.0)

        for k in range(S5_LEVELS):
            xs = shift(x, 1 << k)
            x = x + a1[:, k:k + 1] * xs + a2[:, k:k + 1] * swap(xs)
        h_ref[gi] = x.T
        hs_ref[gi] = h_ref[gi, pl.ds(nchunk - 1, bsz, stride=nchunk), :]
        hprev = shift(x, 1)
        yt = _dot(mt_ref[gi], ut) + _dot(ot_ref[gi], hprev.astype(BF16))
        for b in range(bsz):
            yt_ref[b, :, rows, :] = yt[:, b * nchunk:(b + 1) * nchunk].reshape(nph, S5_GROUP, nchunk)


def _s5_prompt(ut, mt, st, ot, a1t, a2t, odd):
    bsz, nph, _, nchunk = ut.shape
    n, lw = nph * S5_GROUP, 2 * S5_STATE
    steps = S5_GROUPS // S5_GB
    grp = lambda *shape: pl.BlockSpec((S5_GB,) + shape, lambda g: (g + odd * steps, 0, 0))
    act = pl.BlockSpec((bsz, nph, S5_GB * S5_GROUP, nchunk), lambda g: (0, 0, g, 0))
    return pl.pallas_call(
        _s5_prompt_kernel,
        grid=(steps,),
        in_specs=[act, grp(n, n), grp(lw, n), grp(n, lw), grp(lw, 8), grp(lw, 8)],
        out_specs=[act, pl.BlockSpec((S5_GB, bsz, lw), lambda g: (g, 0, 0))],
        out_shape=[jax.ShapeDtypeStruct(ut.shape, F32), jax.ShapeDtypeStruct((S5_GROUPS, bsz, lw), F32)],
        scratch_shapes=[pltpu.VMEM((S5_GB, bsz * nchunk, lw), F32)],
        compiler_params=_params(("arbitrary",)),
        name="s5_prompt",
    )(ut, mt, st, ot, a1t, a2t)


def _odd_out_kernel(x_ref, oc_ref, yt_ref, gw_ref, gb_ref, wout_ref, xo_ref):
    nph, nchunk = x_ref.shape[1], x_ref.shape[2]
    half = nph // 2
    tms = half * nchunk
    for s in range(2):
        ps = slice(s * half, (s + 1) * half)
        y = jnp.concatenate([yt_ref[0, k].T for k in range(s * half, (s + 1) * half)], axis=0)
        g = jax.nn.gelu(y)
        od = g * jax.nn.sigmoid(_dot(g.astype(BF16), gw_ref[...]) + gb_ref[...])
        out = (x_ref[0, ps].reshape(tms, D_MODEL) + _dot(oc_ref[0, ps].reshape(tms, D_C), wout_ref[0:D_C, :])
               + _dot(od.astype(BF16), wout_ref[D_C:, :]))
        xo_ref[0, ps] = out.reshape(half, nchunk, D_MODEL)


def _odd_out(x, oc, yt, glu_w, odd, gb, w_out, layer):
    bsz, nph, nchunk, _ = x.shape
    tok = lambda n: pl.BlockSpec((1, PHO, nchunk, n), lambda b, s: (b, s, 0, 0))
    return pl.pallas_call(
        _odd_out_kernel,
        grid=(bsz, nph // PHO),
        in_specs=[tok(D_MODEL), tok(D_C), pl.BlockSpec((1, PHO, D_D, nchunk), lambda b, s: (b, s, 0, 0)),
                  _layer_spec((D_D, D_D), odd), _const_spec((1, D_D)), _layer_spec((D_MODEL, D_MODEL), layer)],
        out_specs=tok(D_MODEL),
        out_shape=jax.ShapeDtypeStruct(x.shape, F32),
        compiler_params=_params(("arbitrary", "arbitrary")),
        name="odd_out",
    )(x, oc, yt, glu_w, gb, w_out)


def _even_sample_kernel(x_ref, g_ref, win_ref, wout_ref, alng_ref, alnb_ref, ws0_ref, bs0_ref,
                        bcw_ref, bcb_ref, blng_ref, blnb_ref, buf_ref, xo_ref, v_ref, nb_ref):
    x = x_ref[...]
    z = _dot(_rms(x, g_ref[...]).astype(BF16), win_ref[...])
    za = jax.nn.gelu(z[:, :2 * D_A])
    vs = []
    for h in range(A_HEADS):
        lo, hi = h * A_HEAD_DIM, (h + 1) * A_HEAD_DIM
        vs.append(_ln(za[:, D_A + lo:D_A + hi], alng_ref[:, lo:hi], alnb_ref[:, lo:hi]))
    v = jnp.concatenate(vs, axis=-1)
    v_ref[...] = v
    gate = ws0_ref[...] * v + bs0_ref[...]
    out_a = za[:, :D_A] * gate
    glu = z[:, 2 * D_A:2 * D_A + D_B] * jax.nn.sigmoid(z[:, 2 * D_A + D_B:])
    nk = B_CONV_WIDTH - 1
    buf = buf_ref[...]
    acc = bcb_ref[...] + bcw_ref[nk:nk + 1, :] * glu + jnp.sum(buf * bcw_ref[0:nk, :][None], axis=1)
    nb_ref[:, 0:nk - 1, :] = buf[:, 1:nk, :]
    nb_ref[:, nk - 1:nk, :] = glu[:, None, :]
    y = _ln(acc, blng_ref[...], blnb_ref[...])
    out_b = y * jax.nn.sigmoid(y)
    cat = jnp.concatenate([out_a, out_b], axis=-1).astype(BF16)
    xo_ref[...] = x + _dot(cat, wout_ref[...])


def _whole(shape):
    nd = len(shape)
    return pl.BlockSpec(shape, lambda *_: (0,) * nd)


def _call_whole(kernel_fn, name, args, out_shapes, scratch=(), specs=None):
    specs = specs or {}
    return pl.pallas_call(
        kernel_fn,
        grid=(1,),
        in_specs=[specs.get(i) or _const_spec(a.shape) for i, a in enumerate(args)],
        out_specs=[_whole(s.shape) for s in out_shapes],
        out_shape=out_shapes,
        scratch_shapes=list(scratch),
        compiler_params=_params(("arbitrary",)),
        name=name,
    )(*args)


def _odd_sample_kernel(x_ref, g_ref, wina_ref, wut_ref, wout_ref, ccw_ref, buf_ref, h_ref, p1_ref, p2_ref,
                       bbar_ref, ccn_ref, dsk_ref, gw_ref, gb_ref,
                       xo_ref, nb_ref, nh_ref, bd_ref, cd_ref):
    lw = 2 * S5_STATE
    bd_ref[...] = jnp.zeros_like(bd_ref)
    cd_ref[...] = jnp.zeros_like(cd_ref)
    for gi in range(S5_GROUPS):
        r0, l0 = gi * S5_GROUP, gi * lw
        bd_ref[r0:r0 + S5_GROUP, l0:l0 + lw] = bbar_ref[gi]
        cd_ref[r0:r0 + S5_GROUP, l0:l0 + lw] = ccn_ref[gi]
    x = x_ref[...]
    xn = _rms(x, g_ref[...]).astype(BF16)
    z = _dot(xn, wina_ref[...])
    u = _dot_nt(xn, wut_ref[...])
    cin = z[:, 2 * D_C:] * z[:, :D_C]
    buf = buf_ref[...]
    conv = jnp.sum(buf * ccw_ref[0:2, :][None], axis=1) + ccw_ref[2:3, :] * cin
    nb_ref[:, 0:1, :] = buf[:, 1:2, :]
    nb_ref[:, 1:2, :] = cin[:, None, :]
    out_c = z[:, D_C:2 * D_C] * conv
    h = h_ref[...]
    hswap = jnp.concatenate([pltpu.roll(h[:, gi * lw:(gi + 1) * lw], S5_STATE, 1) for gi in range(S5_GROUPS)],
                            axis=-1)
    nh = p1_ref[...] * h + p2_ref[...] * hswap + _dot(u.astype(BF16), bd_ref[...])
    nh_ref[...] = nh
    y = _dot_nt(nh.astype(BF16), cd_ref[...]) + dsk_ref[...] * u
    g = jax.nn.gelu(y)
    od = g * jax.nn.sigmoid(_dot(g.astype(BF16), gw_ref[...]) + gb_ref[...])
    cat = jnp.concatenate([out_c, od], axis=-1).astype(BF16)
    xo_ref[...] = x + _dot(cat, wout_ref[...])


def _ffn_sample_kernel(x_ref, g_ref, wg_ref, wu_ref, wd_ref, cw_ref, b_ref, gf_ref,
                       xo_ref, n_ref, xn_ref, acc_ref, *, final):
    j = pl.program_id(0)

    @pl.when(j == 0)
    def _():
        xn_ref[...] = _rms(x_ref[...], g_ref[...]).astype(BF16)
        acc_ref[...] = jnp.zeros_like(acc_ref)

    zg = _dot(xn_ref[...], wg_ref[...])
    zu = _dot(xn_ref[...], wu_ref[...])
    buf = b_ref[...]
    gc = jnp.sum(buf * cw_ref[0:2, :][None], axis=1) + cw_ref[2:3, :] * zg
    n_ref[:, 0:1, :] = buf[:, 1:2, :]
    n_ref[:, 1:2, :] = zg[:, None, :]
    acc_ref[...] += _dot((gc * jax.nn.sigmoid(gc) * zu).astype(BF16), wd_ref[...])

    @pl.when(j == pl.num_programs(0) - 1)
    def _():
        y = x_ref[...] + acc_ref[...]
        if final:
            y = _rms(y, gf_ref[...])
        xo_ref[...] = y


def _ffn_sample(x, g, w_in, w_down, layer, cw, buf, gf, final):
    n = x.shape[0]
    nc = D_FF // FS_CW
    state = pl.BlockSpec((n, 2, FS_CW), lambda j: (0, 0, j))
    return pl.pallas_call(
        functools.partial(_ffn_sample_kernel, final=final),
        grid=(nc,),
        in_specs=[_whole((n, D_MODEL)), _whole((1, D_MODEL)),
                  pl.BlockSpec((None, D_MODEL, FS_CW), lambda j: (layer, 0, j)),
                  pl.BlockSpec((None, D_MODEL, FS_CW), lambda j: (layer, 0, j + nc)),
                  pl.BlockSpec((None, FS_CW, D_MODEL), lambda j: (layer, j, 0)),
                  pl.BlockSpec((3, FS_CW), lambda j: (0, j)),
                  pl.BlockSpec((None, n, 2, FS_CW), lambda j: (layer, 0, 0, j)), _whole((1, D_MODEL))],
        out_specs=[_whole((n, D_MODEL)), state],
        out_shape=[jax.ShapeDtypeStruct((n, D_MODEL), F32), jax.ShapeDtypeStruct((n, 2, D_FF), F32)],
        scratch_shapes=[pltpu.VMEM((n, D_MODEL), BF16), pltpu.VMEM((n, D_MODEL), F32)],
        compiler_params=_params(("arbitrary",)),
        name="ffn_sample",
    )(x, g, w_in, w_in, w_down, cw, buf, gf)


_S5_POW = np.concatenate([np.arange(S5_CHUNK + 1), S5_CHUNK * 2 ** np.arange(8)]).astype(np.float32)
_S5_POW_ROWS = len(_S5_POW)


def _s5_prep_kernel(pw_ref, *refs):
    for gi in range(S5_PREP_GB):
        _s5_prep_group(pw_ref, *[r.at[pl.ds(gi, 1)] for r in refs])


def _s5_prep_group(pw_ref, lr_ref, li_ref, ldt_ref, bb_ref, bbs_ref, cc_ref, ccs_ref, dd_ref,
                   mt_ref, st_ref, ot_ref, a1_ref, a2_ref, bbar_ref, ccn_ref, p1_ref, p2_ref):
    lw, lc = 2 * S5_STATE, S5_CHUNK
    lane = lax.broadcasted_iota(jnp.int32, (1, lw), 1)
    sgn = jnp.where(lane < S5_STATE, -1.0, 1.0)
    lr, li = lr_ref[0], li_ref[0]
    dt = jnp.exp(ldt_ref[0])
    er, ei = lr * dt, li * dt
    jm = pw_ref[...]
    mag = jnp.exp(jm * er)
    cr, ci = mag * jnp.cos(jm * ei), mag * jnp.sin(jm * ei)

    def scale(j, x, xs):
        return cr[j:j + 1] * x + (sgn * ci[j:j + 1]) * xs

    one = 1
    nr_, ni_ = cr[one:one + 1] - 1.0, ci[one:one + 1]
    den = lr * lr + li * li
    cfr = (nr_ * lr + ni_ * li) / den
    cfi = (ni_ * lr - nr_ * li) / den
    bb, bbs, cc, ccs = bb_ref[0], bbs_ref[0], cc_ref[0], ccs_ref[0]
    bbar = cfr * bb + (sgn * cfi) * bbs
    bbars = cfr * bbs - (sgn * cfi) * bb
    cat = lambda blocks: jnp.concatenate(blocks, axis=0)
    smat = cat([scale(lc - 1 - t, bbar, bbars) for t in range(lc)])
    st_ref[0] = smat.T.astype(BF16)
    krow = _dot_nt(cc, smat * (-sgn), precision=lax.Precision.HIGHEST)
    n = lc * S5_GROUP
    mt = cat([krow if t == lc - 1 else pltpu.roll(krow, (t + 1) * S5_GROUP, 1) for t in range(lc)])
    row = lax.broadcasted_iota(jnp.int32, (n, n), 0)
    col = lax.broadcasted_iota(jnp.int32, (n, n), 1)
    mt = jnp.where((row >> 4) >= (col >> 4), mt, 0.0)
    mt = mt + jnp.where(row == col, dd_ref[0], 0.0)
    mt_ref[0] = mt.astype(BF16)
    ot_ref[0] = cat([scale(t + 1, cc, ccs) * (-sgn) for t in range(lc)]).astype(BF16)
    a1_ref[0] = cr[lc + 1:lc + 9].T
    a2_ref[0] = (sgn * ci[lc + 1:lc + 9]).T
    bbar_ref[0] = bbar.astype(BF16)
    ccn_ref[0] = (cc * (-sgn)).astype(BF16)
    p1_ref[0] = cr[one:one + 1]
    p2_ref[0] = sgn * ci[one:one + 1]


def _s5_prepare(lam_re, lam_im, log_dt, b_re, b_im, c_re, c_im, d_skip):
    flat = lambda a: a.reshape((-1,) + a.shape[2:])
    lam_re, lam_im, log_dt, b_re, b_im, c_re, c_im, d_skip = map(
        flat, (lam_re, lam_im, log_dt, b_re, b_im, c_re, c_im, d_skip))
    ng, lw, n = lam_re.shape[0], 2 * S5_STATE, S5_CHUNK * S5_GROUP
    two = lambda a: jnp.concatenate([a, a], axis=-1)[:, None, :]
    pack = lambda a, b: jnp.concatenate([a, b], axis=-1)
    bt_re, bt_im = b_re.transpose(0, 2, 1), b_im.transpose(0, 2, 1)
    args = (jnp.asarray(np.tile(_S5_POW[:, None], (1, lw))), two(lam_re), two(lam_im),
            jnp.broadcast_to(log_dt[:, None, None], (ng, 1, lw)),
            pack(bt_re, bt_im), pack(bt_im, bt_re), pack(c_re, c_im), pack(c_im, c_re),
            jnp.tile(d_skip, (1, S5_CHUNK))[:, None, :])
    blk = lambda *shape: pl.BlockSpec((S5_PREP_GB,) + shape, lambda s: (s, 0, 0))
    shapes = [((n, n), BF16), ((lw, n), BF16), ((n, lw), BF16), ((lw, 8), F32), ((lw, 8), F32),
              ((S5_GROUP, lw), BF16), ((S5_GROUP, lw), BF16), ((1, lw), F32), ((1, lw), F32)]
    outs = pl.pallas_call(
        _s5_prep_kernel,
        grid=(ng // S5_PREP_GB,),
        in_specs=[_const_spec((_S5_POW_ROWS, lw))] + [blk(*a.shape[1:]) for a in args[1:]],
        out_specs=[blk(*sh) for sh, _ in shapes],
        out_shape=[jax.ShapeDtypeStruct((ng,) + sh, dtp) for sh, dtp in shapes],
        compiler_params=_params(("arbitrary",)),
        name="s5_prep",
    )(*args)
    return dict(zip(("mt", "st", "ot", "a1t", "a2t", "bbar", "ccn", "p1", "p2"), outs))


def _cast_kernel(w_ref, o_ref, *, transpose):
    w = w_ref[...]
    o_ref[...] = (w.T if transpose else w).astype(BF16)


def _to_bf16_all(w, tk):
    nl, k, n = w.shape
    spec = pl.BlockSpec((None, tk, n), lambda l, i: (l, i, 0))
    return pl.pallas_call(
        functools.partial(_cast_kernel, transpose=False),
        grid=(nl, k // tk),
        in_specs=[spec],
        out_specs=spec,
        out_shape=jax.ShapeDtypeStruct(w.shape, BF16),
        compiler_params=_params(("arbitrary", "arbitrary")),
        name="to_bf16_all",
    )(w)


def _to_bf16(w, layer, col_block=0, ncols=None, transpose=False):
    _, k, n = w.shape
    ncols = ncols or n
    tk = 256 if k % 256 == 0 else k
    out_block, out_map, out_shape = ((ncols, tk), lambda i: (0, i), (ncols, k)) if transpose else \
        ((tk, ncols), lambda i: (i, 0), (k, ncols))
    return pl.pallas_call(
        functools.partial(_cast_kernel, transpose=transpose),
        grid=(k // tk,),
        in_specs=[pl.BlockSpec((None, tk, ncols), lambda i: (layer, i, col_block))],
        out_specs=pl.BlockSpec(out_block, out_map),
        out_shape=jax.ShapeDtypeStruct(out_shape, BF16),
        compiler_params=_params(("arbitrary",)),
        name="to_bf16",
    )(w)


def kernel(x_prompt, x_sample, state_conv_b, state_conv_c, state_ssm_re, state_ssm_im, state_ffn_conv, norm_mix, norm_ffn, norm_final, w_mix_in, w_mix_out, a_ln_g, a_ln_b, a_ws, a_bs, b_conv_w, b_conv_b, b_ln_g, b_ln_b, c_conv_w, s5_lam_re, s5_lam_im, s5_log_dt, s5_b_re, s5_b_im, s5_c_re, s5_c_im, s5_d, s5_glu_w, s5_glu_b, ffn_w_in, ffn_conv_w, ffn_w_down):
    bsz, seq, _ = x_prompt.shape
    nsmp = x_sample.shape[0]
    nph, nchunk = S5_CHUNK, seq // S5_CHUNK
    n_odd = s5_lam_re.shape[0]
    row = lambda a: a.reshape(1, -1)
    xp = x_prompt.reshape(bsz, nchunk, nph, D_MODEL)
    xs = x_sample.reshape(nsmp, D_MODEL)
    gf = row(norm_final)
    v_rows, cb_p, cb_s, cc_p, cc_s, re_p, re_s, im_p, im_s, fc_p, fc_s = ([] for _ in range(11))
    q = CHUNK // nph
    sds = jax.ShapeDtypeStruct

    w_in_bf = _to_bf16_all(w_mix_in, D_MODEL)
    w_out_bf = _to_bf16_all(w_mix_out, D_MODEL)
    ffn_in_bf = _to_bf16_all(ffn_w_in, D_MODEL // 2)
    ffn_down_bf = _to_bf16_all(ffn_w_down, D_FF // 2)
    glu_bf = _to_bf16_all(s5_glu_w, D_D)
    p = _s5_prepare(s5_lam_re, s5_lam_im, s5_log_dt, s5_b_re, s5_b_im, s5_c_re, s5_c_im, s5_d)
    p1_rows, p2_rows = p["p1"].reshape(n_odd, 1, -1), p["p2"].reshape(n_odd, 1, -1)

    for l in range(DEPTH):
        g_mix = row(norm_mix[l])
        if l % 2 == 0:
            e = l // 2
            alng, alnb = row(a_ln_g[e]), row(a_ln_b[e])
            ws_p = a_ws[e].reshape(A_HEADS, q, nph, q, nph).transpose(0, 2, 1, 4, 3).reshape(A_HEADS, CHUNK, CHUNK)
            bs_full = jnp.repeat(a_bs[e].T, A_HEAD_DIM, axis=1)
            bs_p = bs_full.reshape(q, nph, D_A).transpose(1, 0, 2).reshape(CHUNK, D_A)
            common = (row(b_conv_b[e]), row(b_ln_g[e]), row(b_ln_b[e]))
            xp, cb = _even_prompt(xp, g_mix, w_in_bf, w_out_bf, l, alng, alnb, ws_p, bs_p, b_conv_w[e], *common,
                                  natural_in=(l == 0))
            cb_p.append(cb)
            ws0 = row(jnp.repeat(a_ws[e][:, 0, 0], A_HEAD_DIM))
            bs0 = row(jnp.repeat(a_bs[e][:, 0], A_HEAD_DIM))
            xs, v, nb = _call_whole(
                _even_sample_kernel, "even_sample",
                (xs, g_mix, w_in_bf, w_out_bf, alng, alnb, ws0, bs0, b_conv_w[e], *common, state_conv_b),
                [sds((nsmp, D_MODEL), F32), sds((nsmp, D_A), F32), sds(state_conv_b.shape[1:], F32)],
                specs={2: _layer_spec((D_MODEL, D_IN), l), 3: _layer_spec((D_MODEL, D_MODEL), l),
                       12: _lead_spec(state_conv_b.shape[1:], e)})
            v_rows.append(v.reshape(nsmp, 1, D_A))
            cb_s.append(nb)
        else:
            o = l // 2
            wut = _to_bf16(w_mix_in, l, 3, D_D, transpose=True)
            gb = row(s5_glu_b[o])
            oc, ut, cc = _odd_in(xp, g_mix, w_in_bf, l, wut, c_conv_w[o])
            cc_p.append(cc)
            yt, hs = _s5_prompt(ut, p["mt"], p["st"], p["ot"], p["a1t"], p["a2t"], o)
            hs = hs.transpose(1, 0, 2)
            re_p.append(hs[..., :S5_STATE])
            im_p.append(hs[..., S5_STATE:])
            xp = _odd_out(xp, oc, yt, glu_bf, o, gb, w_out_bf, l)
            hin =jnp.concatenate([state_ssm_re[o], state_ssm_im[o]], axis=-1).reshape(nsmp, -1)
            nstate = S5_GROUPS * 2 * S5_STATE
            grp_spec = pl.BlockSpec((S5_GROUPS, S5_GROUP, 2 * S5_STATE), lambda *_: (o, 0, 0),
                                    pipeline_mode=pl.Buffered(1))
            xs, nbc, nh = _call_whole(
                _odd_sample_kernel, "odd_sample",
                (xs, g_mix, w_in_bf, wut, w_out_bf, c_conv_w[o], state_conv_c, hin, p1_rows[o], p2_rows[o],
                 p["bbar"], p["ccn"], row(s5_d[o]), glu_bf, gb),
                [sds((nsmp, D_MODEL), F32), sds(state_conv_c.shape[1:], F32), sds(hin.shape, F32)],
                scratch=[pltpu.VMEM((D_D, nstate), BF16), pltpu.VMEM((D_D, nstate), BF16)],
                specs={2: _layer_spec((D_MODEL, 3 * D_C), l), 4: _layer_spec((D_MODEL, D_MODEL), l),
                       6: _lead_spec(state_conv_c.shape[1:], o),
                       10: grp_spec, 11: grp_spec, 13: _layer_spec((D_D, D_D), o)})
            cc_s.append(nbc)
            nh = nh.reshape(nsmp, S5_GROUPS, 2 * S5_STATE)
            re_s.append(nh[..., :S5_STATE])
            im_s.append(nh[..., S5_STATE:])
        final = l == DEPTH - 1
        g_ffn = row(norm_ffn[l])
        xp, fc = _ffn_prompt(xp, g_ffn, ffn_in_bf, ffn_down_bf, l, ffn_conv_w[l], gf, final)
        fc_p.append(fc)
        xs, nfc = _ffn_sample(xs, g_ffn, ffn_in_bf, ffn_down_bf, l, ffn_conv_w[l], state_ffn_conv, gf, final)
        fc_s.append(nfc)

    st = jnp.stack
    return (xp.reshape(bsz, seq, D_MODEL), xs.reshape(nsmp, 1, D_MODEL), st(v_rows), st(cb_p), st(cb_s),
            st(cc_p), st(cc_s), st(re_p), st(re_s), st(im_p), st(im_s), st(fc_p), st(fc_s))
```

```python
import functools

import numpy as np
import jax
import jax.numpy as jnp
from jax import lax
from jax.experimental import pallas as pl
from jax.experimental.pallas import tpu as pltpu

D_MODEL = 1024
DEPTH = 4
D_A = 512
D_B = 512
D_C = 512
D_D = 512
D_IN = 2048
A_HEADS = 4
A_HEAD_DIM = 128
CHUNK = 128
B_CONV_WIDTH = 31
S5_GROUP = 16
S5_GROUPS = 32
S5_STATE = 64
D_FF = 2816
EPS = 1e-6

S5_CHUNK = 16
S5_LEVELS = 7
S5_PREP_GB = 8
S5_GB = 2
PHO = 8
RC = 64
RS = 32
PH = 4
FF_CW = 256
FS_CW = 256
SUB = 8
VMEM_LIMIT = 56 * 1024 * 1024

F32 = jnp.float32
BF16 = jnp.bfloat16


def _rms(x, g):
    return x * lax.rsqrt(jnp.mean(x * x, axis=-1, keepdims=True) + EPS) * g


def _ln(x, g, b):
    mu = jnp.mean(x, axis=-1, keepdims=True)
    xc = x - mu
    var = jnp.mean(xc * xc, axis=-1, keepdims=True)
    return xc * lax.rsqrt(var + EPS) * g + b


def _dot(a, b):
    return jnp.dot(a, b, preferred_element_type=F32)


def _dot_nt(a, b, precision=None):
    return lax.dot_general(a, b, (((1,), (1,)), ((), ())), precision=precision, preferred_element_type=F32)


def _const_spec(shape):
    nd = len(shape)
    return pl.BlockSpec(shape, lambda *_: (0,) * nd, pipeline_mode=pl.Buffered(1))


def _layer_spec(shape2d, layer, col_block=0):
    return pl.BlockSpec((None,) + tuple(shape2d), lambda *_: (layer, 0, col_block), pipeline_mode=pl.Buffered(1))


def _lead_spec(shape, idx):
    nd = len(shape)
    return pl.BlockSpec((None,) + tuple(shape), lambda *_: (idx,) + (0,) * nd, pipeline_mode=pl.Buffered(1))


def _params(sem):
    return pltpu.CompilerParams(dimension_semantics=sem, vmem_limit_bytes=VMEM_LIMIT)


def _shift_rows(blk, first_row):
    row0 = lax.broadcasted_iota(jnp.int32, blk.shape, 0) == 0
    return jnp.where(row0, first_row, pltpu.roll(blk, 1, 0))


def _chunk_tile_spec(n):
    return pl.BlockSpec((1, S5_CHUNK, RC, n), lambda b, i: (b, 0, i, 0))


def _even_prompt_kernel(x_ref, g_ref, win_ref, wout_ref, alng_ref, alnb_ref, ws_ref, bs_ref,
                        bcw_ref, bcb_ref, blng_ref, blnb_ref, xo_ref, cb_ref,
                        z_ref, p_ref, gsh_ref, oa_ref, ob_ref, wb_ref, *, natural_in):
    i = pl.program_id(1)
    ni = pl.num_programs(1)
    nph, rc = xo_ref.shape[1], xo_ref.shape[2]
    rs = RS
    nsub, tms = rc // rs, nph * rs

    @pl.when(jnp.logical_and(pl.program_id(0) == 0, i == 0))
    def _():
        for k in range(B_CONV_WIDTH):
            wb_ref[k] = jnp.broadcast_to(bcw_ref[k:k + 1, :], (SUB, D_B))

    @pl.when(i == 0)
    def _():
        p_ref[0, :, 0:SUB, :] = jnp.zeros((nph, SUB, D_B), F32)

    def load(s):
        xt = jnp.swapaxes(x_ref[0, s * rs:(s + 1) * rs], 0, 1) if natural_in else x_ref[0, :, s * rs:(s + 1) * rs, :]
        return xt.reshape(tms, D_MODEL)

    xs = [load(s) for s in range(nsub)]
    for s in range(nsub):
        z_ref[s] = _dot(_rms(xs[s], g_ref[...]).astype(BF16), win_ref[...])

    q = CHUNK // nph
    r = lax.broadcasted_iota(jnp.int32, (CHUNK, CHUNK), 0)
    c = lax.broadcasted_iota(jnp.int32, (CHUNK, CHUNK), 1)
    pos = lambda k: (k & (q - 1)) * nph + (k >> (q.bit_length() - 1))
    keep = pos(r) >= pos(c)
    wm = [jnp.where(keep, ws_ref[h], 0.0).astype(BF16) for h in range(A_HEADS)]

    for s in range(nsub):
        if s > 0:
            p_ref[s, :, SUB - 2:SUB, :] = p_ref[s - 1, :, SUB + rs - 2:SUB + rs, :]
        for m in range(rs // q):
            rows = [ph * rs + q * m for ph in range(nph)]
            za = jax.nn.gelu(jnp.concatenate([z_ref[s, r0:r0 + q, 0:2 * D_A] for r0 in rows], axis=0))
            for h in range(A_HEADS):
                lo, hi = h * A_HEAD_DIM, (h + 1) * A_HEAD_DIM
                vh = _ln(za[:, D_A + lo:D_A + hi], alng_ref[:, lo:hi], alnb_ref[:, lo:hi])
                oa = za[:, lo:hi] * (_dot(wm[h], vh.astype(BF16)) + bs_ref[:, lo:hi])
                for ph, r0 in enumerate(rows):
                    oa_ref[s, r0:r0 + q, lo:hi] = oa[ph * q:(ph + 1) * q]

        for ph in range(nph):
            zb = z_ref[s, ph * rs:(ph + 1) * rs, 2 * D_A:]
            p_ref[s, ph, SUB:SUB + rs, :] = zb[:, :D_B] * jax.nn.sigmoid(zb[:, D_B:])
        for ph in range(nph):
            gsh_ref[s, 0, ph] = p_ref[s, ph, SUB - 1:SUB - 1 + rs, :]
            gsh_ref[s, 1, ph] = p_ref[s, ph, SUB - 2:SUB - 2 + rs, :]
        for ph in range(nph):
            acc = jnp.zeros((rs // SUB, SUB, D_B), F32)
            for j in range(B_CONV_WIDTH):
                src_ph = (ph - j) % nph
                back = (j - ph + nph - 1) // nph if j > ph else 0
                src = p_ref[s, src_ph, SUB:SUB + rs, :] if back == 0 else gsh_ref[s, back - 1, src_ph]
                acc = acc + wb_ref[B_CONV_WIDTH - 1 - j][None] * src.reshape(rs // SUB, SUB, D_B)
            y = _ln(acc.reshape(rs, D_B) + bcb_ref[...], blng_ref[...], blnb_ref[...])
            ob_ref[s, ph * rs:(ph + 1) * rs, :] = (y * jax.nn.sigmoid(y)).astype(BF16)

        out = (xs[s] + _dot(oa_ref[s].astype(BF16), wout_ref[0:D_A, :]) + _dot(ob_ref[s], wout_ref[D_A:, :]))
        xo_ref[0, :, s * rs:(s + 1) * rs, :] = out.reshape(nph, rs, D_MODEL)

    last = nsub - 1

    @pl.when(i == ni - 1)
    def _():
        n_out = B_CONV_WIDTH - 1
        for k in range(n_out):
            back = n_out - 1 - k
            ph, cl = (nph - 1 - back) % nph, rs - 1 - back // nph
            cb_ref[0, k:k + 1, :] = p_ref[last, ph, SUB + cl:SUB + cl + 1, :]

    p_ref[0, :, SUB - 2:SUB, :] = p_ref[last, :, SUB + rs - 2:SUB + rs, :]


def _even_prompt(x, g, w_in, w_out, layer, alng, alnb, ws, bs, bcw, bcb, blng, blnb, natural_in):
    bsz, nph, nchunk = x.shape[0], S5_CHUNK, x.shape[1] * x.shape[2] // S5_CHUNK
    row = lambda n: _const_spec((1, n))
    nsub, tms = RC // RS, nph * RS
    x_spec = pl.BlockSpec((1, RC, nph, D_MODEL), lambda b, i: (b, i, 0, 0)) if natural_in else \
        _chunk_tile_spec(D_MODEL)
    return pl.pallas_call(
        functools.partial(_even_prompt_kernel, natural_in=natural_in),
        grid=(bsz, nchunk // RC),
        in_specs=[x_spec, row(D_MODEL),
                  _layer_spec((D_MODEL, D_IN), layer), _layer_spec((D_MODEL, D_MODEL), layer),
                  row(D_A), row(D_A), _const_spec((A_HEADS, CHUNK, CHUNK)), _const_spec((CHUNK, D_A)),
                  _const_spec((B_CONV_WIDTH, D_B)), row(D_B), row(D_B), row(D_B)],
        out_specs=[_chunk_tile_spec(D_MODEL),
                   pl.BlockSpec((1, B_CONV_WIDTH - 1, D_B), lambda b, i: (b, 0, 0))],
        out_shape=[jax.ShapeDtypeStruct((bsz, nph, nchunk, D_MODEL), F32),
                   jax.ShapeDtypeStruct((bsz, B_CONV_WIDTH - 1, D_B), F32)],
        scratch_shapes=[pltpu.VMEM((nsub, tms, D_IN), F32), pltpu.VMEM((nsub, nph, RS + SUB, D_B), F32),
                        pltpu.VMEM((nsub, 2, nph, RS, D_B), F32), pltpu.VMEM((nsub, tms, D_A), F32),
                        pltpu.VMEM((nsub, tms, D_B), BF16), pltpu.VMEM((B_CONV_WIDTH, SUB, D_B), F32)],
        compiler_params=_params(("arbitrary", "arbitrary")),
        name="even_prompt",
    )(x, g, w_in, w_out, alng, alnb, ws, bs, bcw, bcb, blng, blnb)


def _ffn_prompt_kernel(x_ref, g_ref, wg_ref, wu_ref, wd_ref, cw_ref, gf_ref, xo_ref, fc_ref,
                       xn_ref, carry_ref, h_ref, *, final):
    i = pl.program_id(1)
    ni = pl.num_programs(1)
    nph, rc = x_ref.shape[1], x_ref.shape[2]
    rs = RS
    tms = nph * rs

    @pl.when(i == 0)
    def _():
        carry_ref[...] = jnp.zeros_like(carry_ref)

    lo2, lo1 = (nph - 2) * rs, (nph - 1) * rs
    for s in range(rc // rs):
        x = x_ref[0, :, s * rs:(s + 1) * rs, :].reshape(tms, D_MODEL)
        xn_ref[s] = _rms(x, g_ref[...]).astype(BF16)
        for c0 in range(0, D_FF, FF_CW):
            cs = slice(c0, c0 + FF_CW)
            zg = _dot(xn_ref[s], wg_ref[:, cs])
            zu = _dot(xn_ref[s], wu_ref[:, cs])
            s2 = _shift_rows(zg[lo2:lo1], carry_ref[0:1, cs])
            s1 = _shift_rows(zg[lo1:], carry_ref[1:2, cs])
            carry_ref[0:1, cs] = zg[lo1 - 1:lo1]
            carry_ref[1:2, cs] = zg[tms - 1:tms]
            z1 = jnp.concatenate([s1, zg[:lo1]], axis=0)
            z2 = jnp.concatenate([s2, s1, zg[:lo2]], axis=0)
            gc = cw_ref[0:1, cs] * z2 + cw_ref[1:2, cs] * z1 + cw_ref[2:3, cs] * zg
            h_ref[s, :, cs] = (gc * jax.nn.sigmoid(gc) * zu).astype(BF16)
        y = x + _dot(h_ref[s], wd_ref[...])
        if final:
            xo_ref[0, s * rs:(s + 1) * rs] = jnp.swapaxes(_rms(y, gf_ref[...]).reshape(nph, rs, D_MODEL), 0, 1)
        else:
            xo_ref[0, :, s * rs:(s + 1) * rs, :] = y.reshape(nph, rs, D_MODEL)

    @pl.when(i == ni - 1)
    def _():
        fc_ref[0] = carry_ref[...]


def _ffn_prompt(x, g, w_in, w_down, layer, cw, gf, final):
    bsz, nph, nchunk, _ = x.shape
    tm = nph * RC
    out_spec, out_shape = (pl.BlockSpec((1, RC, nph, D_MODEL), lambda b, i: (b, i, 0, 0)),
                           (bsz, nchunk, nph, D_MODEL)) if final else (_chunk_tile_spec(D_MODEL), x.shape)
    return pl.pallas_call(
        functools.partial(_ffn_prompt_kernel, final=final),
        grid=(bsz, nchunk // RC),
        in_specs=[_chunk_tile_spec(D_MODEL), _const_spec((1, D_MODEL)),
                  _layer_spec((D_MODEL, D_FF), layer, 0), _layer_spec((D_MODEL, D_FF), layer, 1),
                  _layer_spec((D_FF, D_MODEL), layer), _const_spec((3, D_FF)), _const_spec((1, D_MODEL))],
        out_specs=[out_spec, pl.BlockSpec((1, 2, D_FF), lambda b, i: (b, 0, 0))],
        out_shape=[jax.ShapeDtypeStruct(out_shape, F32), jax.ShapeDtypeStruct((bsz, 2, D_FF), F32)],
        scratch_shapes=[pltpu.VMEM((RC // RS, nph * RS, D_MODEL), BF16), pltpu.VMEM((2, D_FF), F32),
                        pltpu.VMEM((RC // RS, nph * RS, D_FF), BF16)],
        compiler_params=_params(("arbitrary", "arbitrary")),
        name="ffn_prompt",
    )(x, g, w_in, w_in, w_down, cw, gf)


def _odd_in_kernel(x_ref, g_ref, win_ref, wut_ref, ccw_ref, oc_ref, ut_ref, cc_ref, cin_ref, gb_ref):
    nph, nchunk = x_ref.shape[1], x_ref.shape[2]
    for s in range(nph // PH):
        rows = slice(s * PH * nchunk, (s + 1) * PH * nchunk)
        xn = _rms(x_ref[0, s * PH:(s + 1) * PH].reshape(PH * nchunk, D_MODEL), g_ref[...]).astype(BF16)
        z = _dot(xn, win_ref[...])
        cin_ref[rows, :] = z[:, 2 * D_C:] * z[:, :D_C]
        gb_ref[rows, :] = z[:, D_C:2 * D_C]
        ut = _dot_nt(wut_ref[...], xn)
        for k in range(PH):
            ut_ref[0, s * PH + k] = ut[:, k * nchunk:(k + 1) * nchunk].astype(BF16)

    zero = jnp.zeros((1, D_C), F32)
    blk = lambda ph: cin_ref[ph * nchunk:(ph + 1) * nchunk, :]
    s2 = _shift_rows(blk(nph - 2), zero)
    s1 = _shift_rows(blk(nph - 1), zero)
    for ph in range(nph):
        p1 = blk(ph - 1) if ph >= 1 else s1
        p2 = blk(ph - 2) if ph >= 2 else (s1 if ph == 1 else s2)
        conv = ccw_ref[0:1, :] * p2 + ccw_ref[1:2, :] * p1 + ccw_ref[2:3, :] * blk(ph)
        oc_ref[0, ph] = (gb_ref[ph * nchunk:(ph + 1) * nchunk, :] * conv).astype(BF16)
    last = nchunk - 1
    cc_ref[0, 0:1, :] = cin_ref[(nph - 2) * nchunk + last:(nph - 2) * nchunk + last + 1, :]
    cc_ref[0, 1:2, :] = cin_ref[(nph - 1) * nchunk + last:(nph - 1) * nchunk + last + 1, :]


def _odd_in(x, g, w_in, layer, wut, ccw):
    bsz, nph, nchunk, _ = x.shape
    seq = lambda *shape: pl.BlockSpec((1,) + shape, lambda b: (b,) + (0,) * len(shape))
    return pl.pallas_call(
        _odd_in_kernel,
        grid=(bsz,),
        in_specs=[seq(nph, nchunk, D_MODEL), _const_spec((1, D_MODEL)), _layer_spec((D_MODEL, 3 * D_C), layer),
                  _const_spec((D_D, D_MODEL)), _const_spec((3, D_C))],
        out_specs=[seq(nph, nchunk, D_C), seq(nph, D_D, nchunk), seq(2, D_C)],
        out_shape=[jax.ShapeDtypeStruct((bsz, nph, nchunk, D_C), BF16),
                   jax.ShapeDtypeStruct((bsz, nph, D_D, nchunk), BF16),
                   jax.ShapeDtypeStruct((bsz, 2, D_C), F32)],
        scratch_shapes=[pltpu.VMEM((nph * nchunk, D_C), F32), pltpu.VMEM((nph * nchunk, D_C), F32)],
        compiler_params=_params(("arbitrary",)),
        name="odd_in",
    )(x, g, w_in, wut, ccw)


def _s5_prompt_kernel(ut_ref, mt_ref, st_ref, ot_ref, a1_ref, a2_ref, yt_ref, hs_ref, h_ref):
    bsz, nph = ut_ref.shape[0], ut_ref.shape[1]
    nchunk = ut_ref.shape[3]
    n = nph * S5_GROUP
    swap = lambda v: jnp.concatenate([v[S5_STATE:], v[:S5_STATE]], axis=0)
    for gi in range(S5_GB):
        rows = slice(gi * S5_GROUP, (gi + 1) * S5_GROUP)
        ut = jnp.concatenate([ut_ref[b, :, rows, :].reshape(n, nchunk) for b in range(bsz)], axis=-1)
        x = _dot(st_ref[gi], ut)
        cidx = lax.broadcasted_iota(jnp.int32, x.shape, 1) & (nchunk - 1)
        a1, a2 = a1_ref[gi], a2_ref[gi]

        def shift(v, sh):
            rolled = jnp.concatenate([pltpu.roll(v[:, b * nchunk:(b + 1) * nchunk], sh, 1) for b in range(bsz)],
                                     axis=1)
            return jnp.where(cidx >= sh, rolled, 0.0)

        for k in range(S5_LEVELS):
            xs = shift(x, 1 << k)
            x = x + a1[:, k:k + 1] * xs + a2[:, k:k + 1] * swap(xs)
        h_ref[gi] = x.T
        hs_ref[gi] = h_ref[gi, pl.ds(nchunk - 1, bsz, stride=nchunk), :]
        hprev = shift(x, 1)
        yt = _dot(mt_ref[gi], ut) + _dot(ot_ref[gi], hprev.astype(BF16))
        for b in range(bsz):
            yt_ref[b, :, rows, :] = yt[:, b * nchunk:(b + 1) * nchunk].reshape(nph, S5_GROUP, nchunk)


def _s5_prompt(ut, mt, st, ot, a1t, a2t, odd):
    bsz, nph, _, nchunk = ut.shape
    n, lw = nph * S5_GROUP, 2 * S5_STATE
    steps = S5_GROUPS // S5_GB
    grp = lambda *shape: pl.BlockSpec((S5_GB,) + shape, lambda g: (g + odd * steps, 0, 0))
    act = pl.BlockSpec((bsz, nph, S5_GB * S5_GROUP, nchunk), lambda g: (0, 0, g, 0))
    return pl.pallas_call(
        _s5_prompt_kernel,
        grid=(steps,),
        in_specs=[act, grp(n, n), grp(lw, n), grp(n, lw), grp(lw, 8), grp(lw, 8)],
        out_specs=[act, pl.BlockSpec((S5_GB, bsz, lw), lambda g: (g, 0, 0))],
        out_shape=[jax.ShapeDtypeStruct(ut.shape, F32), jax.ShapeDtypeStruct((S5_GROUPS, bsz, lw), F32)],
        scratch_shapes=[pltpu.VMEM((S5_GB, bsz * nchunk, lw), F32)],
        compiler_params=_params(("arbitrary",)),
        name="s5_prompt",
    )(ut, mt, st, ot, a1t, a2t)


def _odd_out_kernel(x_ref, oc_ref, yt_ref, gw_ref, gb_ref, wout_ref, xo_ref):
    nph, nchunk = x_ref.shape[1], x_ref.shape[2]
    half = nph // 2
    tms = half * nchunk
    for s in range(2):
        ps = slice(s * half, (s + 1) * half)
        y = jnp.concatenate([yt_ref[0, k].T for k in range(s * half, (s + 1) * half)], axis=0)
        g = jax.nn.gelu(y)
        od = g * jax.nn.sigmoid(_dot(g.astype(BF16), gw_ref[...]) + gb_ref[...])
        out = (x_ref[0, ps].reshape(tms, D_MODEL) + _dot(oc_ref[0, ps].reshape(tms, D_C), wout_ref[0:D_C, :])
               + _dot(od.astype(BF16), wout_ref[D_C:, :]))
        xo_ref[0, ps] = out.reshape(half, nchunk, D_MODEL)


def _odd_out(x, oc, yt, glu_w, odd, gb, w_out, layer):
    bsz, nph, nchunk, _ = x.shape
    tok = lambda n, **kw: pl.BlockSpec((1, PHO, nchunk, n), lambda b, s: (b, s, 0, 0), **kw)
    deep = dict(pipeline_mode=pl.Buffered(3))
    in_specs = [tok(D_MODEL, **deep), tok(D_C, **deep),
                pl.BlockSpec((1, PHO, D_D, nchunk), lambda b, s: (b, s, 0, 0), **deep),
                pl.BlockSpec((None, D_D, D_D), lambda b, s: (odd, 0, 0)),
                pl.BlockSpec((1, D_D), lambda b, s: (0, 0)),
                pl.BlockSpec((None, D_MODEL, D_MODEL), lambda b, s: (layer, 0, 0))]

    def outer(*hbm_refs):
        pltpu.emit_pipeline(_odd_out_kernel, grid=(bsz, nph // PHO), in_specs=in_specs,
                            out_specs=[tok(D_MODEL)])(*hbm_refs)

    hbm = pl.BlockSpec(memory_space=pl.ANY)
    return pl.pallas_call(
        outer,
        in_specs=[hbm] * 6,
        out_specs=hbm,
        out_shape=jax.ShapeDtypeStruct(x.shape, F32),
        compiler_params=pltpu.CompilerParams(vmem_limit_bytes=VMEM_LIMIT),
        name="odd_out",
    )(x, oc, yt, glu_w, gb, w_out)


def _even_sample_kernel(x_ref, g_ref, win_ref, wout_ref, alng_ref, alnb_ref, ws0_ref, bs0_ref,
                        bcw_ref, bcb_ref, blng_ref, blnb_ref, buf_ref, xo_ref, v_ref, nb_ref):
    x = x_ref[...]
    z = _dot(_rms(x, g_ref[...]).astype(BF16), win_ref[...])
    za = jax.nn.gelu(z[:, :2 * D_A])
    vs = []
    for h in range(A_HEADS):
        lo, hi = h * A_HEAD_DIM, (h + 1) * A_HEAD_DIM
        vs.append(_ln(za[:, D_A + lo:D_A + hi], alng_ref[:, lo:hi], alnb_ref[:, lo:hi]))
    v = jnp.concatenate(vs, axis=-1)
    v_ref[...] = v
    gate = ws0_ref[...] * v + bs0_ref[...]
    out_a = za[:, :D_A] * gate
    glu = z[:, 2 * D_A:2 * D_A + D_B] * jax.nn.sigmoid(z[:, 2 * D_A + D_B:])
    nk = B_CONV_WIDTH - 1
    buf = buf_ref[...]
    acc = bcb_ref[...] + bcw_ref[nk:nk + 1, :] * glu + jnp.sum(buf * bcw_ref[0:nk, :][None], axis=1)
    nb_ref[:, 0:nk - 1, :] = buf[:, 1:nk, :]
    nb_ref[:, nk - 1:nk, :] = glu[:, None, :]
    y = _ln(acc, blng_ref[...], blnb_ref[...])
    out_b = y * jax.nn.sigmoid(y)
    cat = jnp.concatenate([out_a, out_b], axis=-1).astype(BF16)
    xo_ref[...] = x + _dot(cat, wout_ref[...])


def _whole(shape):
    nd = len(shape)
    return pl.BlockSpec(shape, lambda *_: (0,) * nd)


def _call_whole(kernel_fn, name, args, out_shapes, scratch=(), specs=None):
    specs = specs or {}
    return pl.pallas_call(
        kernel_fn,
        grid=(1,),
        in_specs=[specs.get(i) or _const_spec(a.shape) for i, a in enumerate(args)],
        out_specs=[_whole(s.shape) for s in out_shapes],
        out_shape=out_shapes,
        scratch_shapes=list(scratch),
        compiler_params=_params(("arbitrary",)),
        name=name,
    )(*args)


def _odd_sample_kernel(x_ref, g_ref, wina_ref, wut_ref, wout_ref, ccw_ref, buf_ref, h_ref, p1_ref, p2_ref,
                       bbar_ref, ccn_ref, dsk_ref, gw_ref, gb_ref,
                       xo_ref, nb_ref, nh_ref, bd_ref, cd_ref):
    lw = 2 * S5_STATE
    bd_ref[...] = jnp.zeros_like(bd_ref)
    cd_ref[...] = jnp.zeros_like(cd_ref)
    for gi in range(S5_GROUPS):
        r0, l0 = gi * S5_GROUP, gi * lw
        bd_ref[r0:r0 + S5_GROUP, l0:l0 + lw] = bbar_ref[gi]
        cd_ref[r0:r0 + S5_GROUP, l0:l0 + lw] = ccn_ref[gi]
    x = x_ref[...]
    xn = _rms(x, g_ref[...]).astype(BF16)
    z = _dot(xn, wina_ref[...])
    u = _dot_nt(xn, wut_ref[...])
    cin = z[:, 2 * D_C:] * z[:, :D_C]
    buf = buf_ref[...]
    conv = jnp.sum(buf * ccw_ref[0:2, :][None], axis=1) + ccw_ref[2:3, :] * cin
    nb_ref[:, 0:1, :] = buf[:, 1:2, :]
    nb_ref[:, 1:2, :] = cin[:, None, :]
    out_c = z[:, D_C:2 * D_C] * conv
    h = h_ref[...]
    hswap = jnp.concatenate([pltpu.roll(h[:, gi * lw:(gi + 1) * lw], S5_STATE, 1) for gi in range(S5_GROUPS)],
                            axis=-1)
    nh = p1_ref[...] * h + p2_ref[...] * hswap + _dot(u.astype(BF16), bd_ref[...])
    nh_ref[...] = nh
    y = _dot_nt(nh.astype(BF16), cd_ref[...]) + dsk_ref[...] * u
    g = jax.nn.gelu(y)
    od = g * jax.nn.sigmoid(_dot(g.astype(BF16), gw_ref[...]) + gb_ref[...])
    cat = jnp.concatenate([out_c, od], axis=-1).astype(BF16)
    xo_ref[...] = x + _dot(cat, wout_ref[...])


def _ffn_sample_kernel(x_ref, g_ref, wg_ref, wu_ref, wd_ref, cw_ref, b_ref, gf_ref,
                       xo_ref, n_ref, xn_ref, acc_ref, *, final):
    j = pl.program_id(0)

    @pl.when(j == 0)
    def _():
        xn_ref[...] = _rms(x_ref[...], g_ref[...]).astype(BF16)
        acc_ref[...] = jnp.zeros_like(acc_ref)

    zg = _dot(xn_ref[...], wg_ref[...])
    zu = _dot(xn_ref[...], wu_ref[...])
    buf = b_ref[...]
    gc = jnp.sum(buf * cw_ref[0:2, :][None], axis=1) + cw_ref[2:3, :] * zg
    n_ref[:, 0:1, :] = buf[:, 1:2, :]
    n_ref[:, 1:2, :] = zg[:, None, :]
    acc_ref[...] += _dot((gc * jax.nn.sigmoid(gc) * zu).astype(BF16), wd_ref[...])

    @pl.when(j == pl.num_programs(0) - 1)
    def _():
        y = x_ref[...] + acc_ref[...]
        if final:
            y = _rms(y, gf_ref[...])
        xo_ref[...] = y


def _ffn_sample(x, g, w_in, w_down, layer, cw, buf, gf, final):
    n = x.shape[0]
    nc = D_FF // FS_CW
    state = pl.BlockSpec((n, 2, FS_CW), lambda j: (0, 0, j))
    return pl.pallas_call(
        functools.partial(_ffn_sample_kernel, final=final),
        grid=(nc,),
        in_specs=[_whole((n, D_MODEL)), _whole((1, D_MODEL)),
                  pl.BlockSpec((None, D_MODEL, FS_CW), lambda j: (layer, 0, j)),
                  pl.BlockSpec((None, D_MODEL, FS_CW), lambda j: (layer, 0, j + nc)),
                  pl.BlockSpec((None, FS_CW, D_MODEL), lambda j: (layer, j, 0)),
                  pl.BlockSpec((3, FS_CW), lambda j: (0, j)),
                  pl.BlockSpec((None, n, 2, FS_CW), lambda j: (layer, 0, 0, j)), _whole((1, D_MODEL))],
        out_specs=[_whole((n, D_MODEL)), state],
        out_shape=[jax.ShapeDtypeStruct((n, D_MODEL), F32), jax.ShapeDtypeStruct((n, 2, D_FF), F32)],
        scratch_shapes=[pltpu.VMEM((n, D_MODEL), BF16), pltpu.VMEM((n, D_MODEL), F32)],
        compiler_params=_params(("arbitrary",)),
        name="ffn_sample",
    )(x, g, w_in, w_in, w_down, cw, buf, gf)


_S5_POW = np.concatenate([np.arange(S5_CHUNK + 1), S5_CHUNK * 2 ** np.arange(8)]).astype(np.float32)
_S5_POW_ROWS = len(_S5_POW)


def _s5_prep_kernel(pw_ref, *refs):
    for gi in range(S5_PREP_GB):
        _s5_prep_group(pw_ref, *[r.at[pl.ds(gi, 1)] for r in refs])


def _s5_prep_group(pw_ref, lr_ref, li_ref, ldt_ref, bb_ref, bbs_ref, cc_ref, ccs_ref, dd_ref,
                   mt_ref, st_ref, ot_ref, a1_ref, a2_ref, bbar_ref, ccn_ref, p1_ref, p2_ref):
    lw, lc = 2 * S5_STATE, S5_CHUNK
    lane = lax.broadcasted_iota(jnp.int32, (1, lw), 1)
    sgn = jnp.where(lane < S5_STATE, -1.0, 1.0)
    lr, li = lr_ref[0], li_ref[0]
    dt = jnp.exp(ldt_ref[0])
    er, ei = lr * dt, li * dt
    jm = pw_ref[...]
    mag = jnp.exp(jm * er)
    cr, ci = mag * jnp.cos(jm * ei), mag * jnp.sin(jm * ei)

    def scale(j, x, xs):
        return cr[j:j + 1] * x + (sgn * ci[j:j + 1]) * xs

    one = 1
    nr_, ni_ = cr[one:one + 1] - 1.0, ci[one:one + 1]
    den = lr * lr + li * li
    cfr = (nr_ * lr + ni_ * li) / den
    cfi = (ni_ * lr - nr_ * li) / den
    bb, bbs, cc, ccs = bb_ref[0], bbs_ref[0], cc_ref[0], ccs_ref[0]
    bbar = cfr * bb + (sgn * cfi) * bbs
    bbars = cfr * bbs - (sgn * cfi) * bb
    cat = lambda blocks: jnp.concatenate(blocks, axis=0)
    smat = cat([scale(lc - 1 - t, bbar, bbars) for t in range(lc)])
    st_ref[0] = smat.T.astype(BF16)
    krow = _dot_nt(cc, smat * (-sgn), precision=lax.Precision.HIGHEST)
    n = lc * S5_GROUP
    mt = cat([krow if t == lc - 1 else pltpu.roll(krow, (t + 1) * S5_GROUP, 1) for t in range(lc)])
    row = lax.broadcasted_iota(jnp.int32, (n, n), 0)
    col = lax.broadcasted_iota(jnp.int32, (n, n), 1)
    mt = jnp.where((row >> 4) >= (col >> 4), mt, 0.0)
    mt = mt + jnp.where(row == col, dd_ref[0], 0.0)
    mt_ref[0] = mt.astype(BF16)
    ot_ref[0] = cat([scale(t + 1, cc, ccs) * (-sgn) for t in range(lc)]).astype(BF16)
    a1_ref[0] = cr[lc + 1:lc + 9].T
    a2_ref[0] = (sgn * ci[lc + 1:lc + 9]).T
    bbar_ref[0] = bbar.astype(BF16)
    ccn_ref[0] = (cc * (-sgn)).astype(BF16)
    p1_ref[0] = cr[one:one + 1]
    p2_ref[0] = sgn * ci[one:one + 1]


def _s5_prepare(lam_re, lam_im, log_dt, b_re, b_im, c_re, c_im, d_skip):
    flat = lambda a: a.reshape((-1,) + a.shape[2:])
    lam_re, lam_im, log_dt, b_re, b_im, c_re, c_im, d_skip = map(
        flat, (lam_re, lam_im, log_dt, b_re, b_im, c_re, c_im, d_skip))
    ng, lw, n = lam_re.shape[0], 2 * S5_STATE, S5_CHUNK * S5_GROUP
    two = lambda a: jnp.concatenate([a, a], axis=-1)[:, None, :]
    pack = lambda a, b: jnp.concatenate([a, b], axis=-1)
    bt_re, bt_im = b_re.transpose(0, 2, 1), b_im.transpose(0, 2, 1)
    args = (jnp.asarray(np.tile(_S5_POW[:, None], (1, lw))), two(lam_re), two(lam_im),
            jnp.broadcast_to(log_dt[:, None, None], (ng, 1, lw)),
            pack(bt_re, bt_im), pack(bt_im, bt_re), pack(c_re, c_im), pack(c_im, c_re),
            jnp.tile(d_skip, (1, S5_CHUNK))[:, None, :])
    blk = lambda *shape: pl.BlockSpec((S5_PREP_GB,) + shape, lambda s: (s, 0, 0))
    shapes = [((n, n), BF16), ((lw, n), BF16), ((n, lw), BF16), ((lw, 8), F32), ((lw, 8), F32),
              ((S5_GROUP, lw), BF16), ((S5_GROUP, lw), BF16), ((1, lw), F32), ((1, lw), F32)]
    outs = pl.pallas_call(
        _s5_prep_kernel,
        grid=(ng // S5_PREP_GB,),
        in_specs=[_const_spec((_S5_POW_ROWS, lw))] + [blk(*a.shape[1:]) for a in args[1:]],
        out_specs=[blk(*sh) for sh, _ in shapes],
        out_shape=[jax.ShapeDtypeStruct((ng,) + sh, dtp) for sh, dtp in shapes],
        compiler_params=_params(("arbitrary",)),
        name="s5_prep",
    )(*args)
    return dict(zip(("mt", "st", "ot", "a1t", "a2t", "bbar", "ccn", "p1", "p2"), outs))


def _cast_kernel(w_ref, o_ref, *, transpose):
    w = w_ref[...]
    o_ref[...] = (w.T if transpose else w).astype(BF16)


def _to_bf16_all(w, tk):
    nl, k, n = w.shape
    spec = pl.BlockSpec((None, tk, n), lambda l, i: (l, i, 0))
    return pl.pallas_call(
        functools.partial(_cast_kernel, transpose=False),
        grid=(nl, k // tk),
        in_specs=[spec],
        out_specs=spec,
        out_shape=jax.ShapeDtypeStruct(w.shape, BF16),
        compiler_params=_params(("arbitrary", "arbitrary")),
        name="to_bf16_all",
    )(w)


def _to_bf16(w, layer, col_block=0, ncols=None, transpose=False):
    _, k, n = w.shape
    ncols = ncols or n
    tk = 256 if k % 256 == 0 else k
    out_block, out_map, out_shape = ((ncols, tk), lambda i: (0, i), (ncols, k)) if transpose else \
        ((tk, ncols), lambda i: (i, 0), (k, ncols))
    return pl.pallas_call(
        functools.partial(_cast_kernel, transpose=transpose),
        grid=(k // tk,),
        in_specs=[pl.BlockSpec((None, tk, ncols), lambda i: (layer, i, col_block))],
        out_specs=pl.BlockSpec(out_block, out_map),
        out_shape=jax.ShapeDtypeStruct(out_shape, BF16),
        compiler_params=_params(("arbitrary",)),
        name="to_bf16",
    )(w)


def kernel(x_prompt, x_sample, state_conv_b, state_conv_c, state_ssm_re, state_ssm_im, state_ffn_conv, norm_mix, norm_ffn, norm_final, w_mix_in, w_mix_out, a_ln_g, a_ln_b, a_ws, a_bs, b_conv_w, b_conv_b, b_ln_g, b_ln_b, c_conv_w, s5_lam_re, s5_lam_im, s5_log_dt, s5_b_re, s5_b_im, s5_c_re, s5_c_im, s5_d, s5_glu_w, s5_glu_b, ffn_w_in, ffn_conv_w, ffn_w_down):
    bsz, seq, _ = x_prompt.shape
    nsmp = x_sample.shape[0]
    nph, nchunk = S5_CHUNK, seq // S5_CHUNK
    n_odd = s5_lam_re.shape[0]
    row = lambda a: a.reshape(1, -1)
    xp = x_prompt.reshape(bsz, nchunk, nph, D_MODEL)
    xs = x_sample.reshape(nsmp, D_MODEL)
    gf = row(norm_final)
    v_rows, cb_p, cb_s, cc_p, cc_s, re_p, re_s, im_p, im_s, fc_p, fc_s = ([] for _ in range(11))
    q = CHUNK // nph
    sds = jax.ShapeDtypeStruct

    w_in_bf = _to_bf16_all(w_mix_in, D_MODEL)
    w_out_bf = _to_bf16_all(w_mix_out, D_MODEL)
    ffn_in_bf = _to_bf16_all(ffn_w_in, D_MODEL // 2)
    ffn_down_bf = _to_bf16_all(ffn_w_down, D_FF // 2)
    glu_bf = _to_bf16_all(s5_glu_w, D_D)
    p = _s5_prepare(s5_lam_re, s5_lam_im, s5_log_dt, s5_b_re, s5_b_im, s5_c_re, s5_c_im, s5_d)
    p1_rows, p2_rows = p["p1"].reshape(n_odd, 1, -1), p["p2"].reshape(n_odd, 1, -1)

    for l in range(DEPTH):
        g_mix = row(norm_mix[l])
        if l % 2 == 0:
            e = l // 2
            alng, alnb = row(a_ln_g[e]), row(a_ln_b[e])
            ws_p = a_ws[e].reshape(A_HEADS, q, nph, q, nph).transpose(0, 2, 1, 4, 3).reshape(A_HEADS, CHUNK, CHUNK)
            bs_full = jnp.repeat(a_bs[e].T, A_HEAD_DIM, axis=1)
            bs_p = bs_full.reshape(q, nph, D_A).transpose(1, 0, 2).reshape(CHUNK, D_A)
            common = (row(b_conv_b[e]), row(b_ln_g[e]), row(b_ln_b[e]))
            xp, cb = _even_prompt(xp, g_mix, w_in_bf, w_out_bf, l, alng, alnb, ws_p, bs_p, b_conv_w[e], *common,
                                  natural_in=(l == 0))
            cb_p.append(cb)
            ws0 = row(jnp.repeat(a_ws[e][:, 0, 0], A_HEAD_DIM))
            bs0 = row(jnp.repeat(a_bs[e][:, 0], A_HEAD_DIM))
            xs, v, nb = _call_whole(
                _even_sample_kernel, "even_sample",
                (xs, g_mix, w_in_bf, w_out_bf, alng, alnb, ws0, bs0, b_conv_w[e], *common, state_conv_b),
                [sds((nsmp, D_MODEL), F32), sds((nsmp, D_A), F32), sds(state_conv_b.shape[1:], F32)],
                specs={2: _layer_spec((D_MODEL, D_IN), l), 3: _layer_spec((D_MODEL, D_MODEL), l),
                       12: _lead_spec(state_conv_b.shape[1:], e)})
            v_rows.append(v.reshape(nsmp, 1, D_A))
            cb_s.append(nb)
        else:
            o = l // 2
            wut = _to_bf16(w_mix_in, l, 3, D_D, transpose=True)
            gb = row(s5_glu_b[o])
            oc, ut, cc = _odd_in(xp, g_mix, w_in_bf, l, wut, c_conv_w[o])
            cc_p.append(cc)
            yt, hs = _s5_prompt(ut, p["mt"], p["st"], p["ot"], p["a1t"], p["a2t"], o)
            hs = hs.transpose(1, 0, 2)
            re_p.append(hs[..., :S5_STATE])
            im_p.append(hs[..., S5_STATE:])
            xp = _odd_out(xp, oc, yt, glu_bf, o, gb, w_out_bf, l)
            hin =jnp.concatenate([state_ssm_re[o], state_ssm_im[o]], axis=-1).reshape(nsmp, -1)
            nstate = S5_GROUPS * 2 * S5_STATE
            grp_spec = pl.BlockSpec((S5_GROUPS, S5_GROUP, 2 * S5_STATE), lambda *_: (o, 0, 0),
                                    pipeline_mode=pl.Buffered(1))
            xs, nbc, nh = _call_whole(
                _odd_sample_kernel, "odd_sample",
                (xs, g_mix, w_in_bf, wut, w_out_bf, c_conv_w[o], state_conv_c, hin, p1_rows[o], p2_rows[o],
                 p["bbar"], p["ccn"], row(s5_d[o]), glu_bf, gb),
                [sds((nsmp, D_MODEL), F32), sds(state_conv_c.shape[1:], F32), sds(hin.shape, F32)],
                scratch=[pltpu.VMEM((D_D, nstate), BF16), pltpu.VMEM((D_D, nstate), BF16)],
                specs={2: _layer_spec((D_MODEL, 3 * D_C), l), 4: _layer_spec((D_MODEL, D_MODEL), l),
                       6: _lead_spec(state_conv_c.shape[1:], o),
                       10: grp_spec, 11: grp_spec, 13: _layer_spec((D_D, D_D), o)})
            cc_s.append(nbc)
            nh = nh.reshape(nsmp, S5_GROUPS, 2 * S5_STATE)
            re_s.append(nh[..., :S5_STATE])
            im_s.append(nh[..., S5_STATE:])
        final = l == DEPTH - 1
        g_ffn = row(norm_ffn[l])
        xp, fc = _ffn_prompt(xp, g_ffn, ffn_in_bf, ffn_down_bf, l, ffn_conv_w[l], gf, final)
        fc_p.append(fc)
        xs, nfc = _ffn_sample(xs, g_ffn, ffn_in_bf, ffn_down_bf, l, ffn_conv_w[l], state_ffn_conv, gf, final)
        fc_s.append(nfc)

    st = jnp.stack
    return (xp.reshape(bsz, seq, D_MODEL), xs.reshape(nsmp, 1, D_MODEL), st(v_rows), st(cb_p), st(cb_s),
            st(cc_p), st(cc_s), st(re_p), st(re_s), st(im_p), st(im_s), st(fc_p), st(fc_s))
```

```python
import functools

import numpy as np
import jax
import jax.numpy as jnp
from jax import lax
from jax.experimental import pallas as pl
from jax.experimental.pallas import tpu as pltpu

D_MODEL = 1024
DEPTH = 4
D_A = 512
D_B = 512
D_C = 512
D_D = 512
D_IN = 2048
A_HEADS = 4
A_HEAD_DIM = 128
CHUNK = 128
B_CONV_WIDTH = 31
S5_GROUP = 16
S5_GROUPS = 32
S5_STATE = 64
D_FF = 2816
EPS = 1e-6

S5_CHUNK = 16
S5_LEVELS = 7
S5_PREP_GB = 8
S5_GB = 2
PHO = 8
RC = 64
RS = 32
PH = 4
FF_CW = 256
FS_CW = 256
SUB = 8
VMEM_LIMIT = 56 * 1024 * 1024

F32 = jnp.float32
BF16 = jnp.bfloat16


def _rms(x, g):
    return x * lax.rsqrt(jnp.mean(x * x, axis=-1, keepdims=True) + EPS) * g


def _ln(x, g, b):
    mu = jnp.mean(x, axis=-1, keepdims=True)
    xc = x - mu
    var = jnp.mean(xc * xc, axis=-1, keepdims=True)
    return xc * lax.rsqrt(var + EPS) * g + b


_GELU_C0 = float(np.sqrt(2.0 / np.pi))
_GELU_C1 = 0.044715 * _GELU_C0


def _gelu(x):
    return x * (0.5 + 0.5 * jnp.tanh(x * (_GELU_C0 + _GELU_C1 * (x * x))))


def _dot(a, b):
    return jnp.dot(a, b, preferred_element_type=F32)


def _dot_nt(a, b, precision=None):
    return lax.dot_general(a, b, (((1,), (1,)), ((), ())), precision=precision, preferred_element_type=F32)


def _const_spec(shape):
    nd = len(shape)
    return pl.BlockSpec(shape, lambda *_: (0,) * nd, pipeline_mode=pl.Buffered(1))


def _layer_spec(shape2d, layer, col_block=0):
    return pl.BlockSpec((None,) + tuple(shape2d), lambda *_: (layer, 0, col_block), pipeline_mode=pl.Buffered(1))


def _lead_spec(shape, idx):
    nd = len(shape)
    return pl.BlockSpec((None,) + tuple(shape), lambda *_: (idx,) + (0,) * nd, pipeline_mode=pl.Buffered(1))


def _params(sem):
    return pltpu.CompilerParams(dimension_semantics=sem, vmem_limit_bytes=VMEM_LIMIT)


def _shift_rows(blk, first_row):
    row0 = lax.broadcasted_iota(jnp.int32, blk.shape, 0) == 0
    return jnp.where(row0, first_row, pltpu.roll(blk, 1, 0))


def _chunk_tile_spec(n):
    return pl.BlockSpec((1, S5_CHUNK, RC, n), lambda b, i: (b, 0, i, 0))


def _even_prompt_kernel(x_ref, g_ref, win_ref, wout_ref, alng_ref, alnb_ref, ws_ref, bs_ref,
                        bcw_ref, bcb_ref, blng_ref, blnb_ref, xo_ref, cb_ref,
                        z_ref, p_ref, gsh_ref, oa_ref, ob_ref, wb_ref, *, natural_in):
    i = pl.program_id(1)
    ni = pl.num_programs(1)
    nph, rc = xo_ref.shape[1], xo_ref.shape[2]
    rs = RS
    nsub, tms = rc // rs, nph * rs

    @pl.when(jnp.logical_and(pl.program_id(0) == 0, i == 0))
    def _():
        for k in range(B_CONV_WIDTH):
            wb_ref[k] = jnp.broadcast_to(bcw_ref[k:k + 1, :], (SUB, D_B))

    @pl.when(i == 0)
    def _():
        p_ref[0, :, 0:SUB, :] = jnp.zeros((nph, SUB, D_B), F32)

    def load(s):
        xt = jnp.swapaxes(x_ref[0, s * rs:(s + 1) * rs], 0, 1) if natural_in else x_ref[0, :, s * rs:(s + 1) * rs, :]
        return xt.reshape(tms, D_MODEL)

    xs = [load(s) for s in range(nsub)]
    for s in range(nsub):
        z_ref[s] = _dot(_rms(xs[s], g_ref[...]).astype(BF16), win_ref[...])

    q = CHUNK // nph
    r = lax.broadcasted_iota(jnp.int32, (CHUNK, CHUNK), 0)
    c = lax.broadcasted_iota(jnp.int32, (CHUNK, CHUNK), 1)
    pos = lambda k: (k & (q - 1)) * nph + (k >> (q.bit_length() - 1))
    keep = pos(r) >= pos(c)
    wm = [jnp.where(keep, ws_ref[h], 0.0).astype(BF16) for h in range(A_HEADS)]

    for s in range(nsub):
        if s > 0:
            p_ref[s, :, SUB - 2:SUB, :] = p_ref[s - 1, :, SUB + rs - 2:SUB + rs, :]
        for m in range(rs // q):
            rows = [ph * rs + q * m for ph in range(nph)]
            za = _gelu(jnp.concatenate([z_ref[s, r0:r0 + q, 0:2 * D_A] for r0 in rows], axis=0))
            for h in range(A_HEADS):
                lo, hi = h * A_HEAD_DIM, (h + 1) * A_HEAD_DIM
                vh = _ln(za[:, D_A + lo:D_A + hi], alng_ref[:, lo:hi], alnb_ref[:, lo:hi])
                oa = za[:, lo:hi] * (_dot(wm[h], vh.astype(BF16)) + bs_ref[:, lo:hi])
                for ph, r0 in enumerate(rows):
                    oa_ref[s, r0:r0 + q, lo:hi] = oa[ph * q:(ph + 1) * q]

        for ph in range(nph):
            zb = z_ref[s, ph * rs:(ph + 1) * rs, 2 * D_A:]
            p_ref[s, ph, SUB:SUB + rs, :] = zb[:, :D_B] * jax.nn.sigmoid(zb[:, D_B:])
        for ph in range(nph):
            gsh_ref[s, 0, ph] = p_ref[s, ph, SUB - 1:SUB - 1 + rs, :]
            gsh_ref[s, 1, ph] = p_ref[s, ph, SUB - 2:SUB - 2 + rs, :]
        for ph in range(nph):
            acc = jnp.zeros((rs // SUB, SUB, D_B), F32)
            for j in range(B_CONV_WIDTH):
                src_ph = (ph - j) % nph
                back = (j - ph + nph - 1) // nph if j > ph else 0
                src = p_ref[s, src_ph, SUB:SUB + rs, :] if back == 0 else gsh_ref[s, back - 1, src_ph]
                acc = acc + wb_ref[B_CONV_WIDTH - 1 - j][None] * src.reshape(rs // SUB, SUB, D_B)
            y = _ln(acc.reshape(rs, D_B) + bcb_ref[...], blng_ref[...], blnb_ref[...])
            ob_ref[s, ph * rs:(ph + 1) * rs, :] = (y * jax.nn.sigmoid(y)).astype(BF16)

        out = (xs[s] + _dot(oa_ref[s].astype(BF16), wout_ref[0:D_A, :]) + _dot(ob_ref[s], wout_ref[D_A:, :]))
        xo_ref[0, :, s * rs:(s + 1) * rs, :] = out.reshape(nph, rs, D_MODEL)

    last = nsub - 1

    @pl.when(i == ni - 1)
    def _():
        n_out = B_CONV_WIDTH - 1
        for k in range(n_out):
            back = n_out - 1 - k
            ph, cl = (nph - 1 - back) % nph, rs - 1 - back // nph
            cb_ref[0, k:k + 1, :] = p_ref[last, ph, SUB + cl:SUB + cl + 1, :]

    p_ref[0, :, SUB - 2:SUB, :] = p_ref[last, :, SUB + rs - 2:SUB + rs, :]


def _even_prompt(x, g, w_in, w_out, layer, alng, alnb, ws, bs, bcw, bcb, blng, blnb, natural_in):
    bsz, nph, nchunk = x.shape[0], S5_CHUNK, x.shape[1] * x.shape[2] // S5_CHUNK
    row = lambda n: _const_spec((1, n))
    nsub, tms = RC // RS, nph * RS
    x_spec = pl.BlockSpec((1, RC, nph, D_MODEL), lambda b, i: (b, i, 0, 0)) if natural_in else \
        _chunk_tile_spec(D_MODEL)
    return pl.pallas_call(
        functools.partial(_even_prompt_kernel, natural_in=natural_in),
        grid=(bsz, nchunk // RC),
        in_specs=[x_spec, row(D_MODEL),
                  _layer_spec((D_MODEL, D_IN), layer), _layer_spec((D_MODEL, D_MODEL), layer),
                  row(D_A), row(D_A), _const_spec((A_HEADS, CHUNK, CHUNK)), _const_spec((CHUNK, D_A)),
                  _const_spec((B_CONV_WIDTH, D_B)), row(D_B), row(D_B), row(D_B)],
        out_specs=[_chunk_tile_spec(D_MODEL),
                   pl.BlockSpec((1, B_CONV_WIDTH - 1, D_B), lambda b, i: (b, 0, 0))],
        out_shape=[jax.ShapeDtypeStruct((bsz, nph, nchunk, D_MODEL), F32),
                   jax.ShapeDtypeStruct((bsz, B_CONV_WIDTH - 1, D_B), F32)],
        scratch_shapes=[pltpu.VMEM((nsub, tms, D_IN), F32), pltpu.VMEM((nsub, nph, RS + SUB, D_B), F32),
                        pltpu.VMEM((nsub, 2, nph, RS, D_B), F32), pltpu.VMEM((nsub, tms, D_A), F32),
                        pltpu.VMEM((nsub, tms, D_B), BF16), pltpu.VMEM((B_CONV_WIDTH, SUB, D_B), F32)],
        compiler_params=_params(("arbitrary", "arbitrary")),
        name="even_prompt",
    )(x, g, w_in, w_out, alng, alnb, ws, bs, bcw, bcb, blng, blnb)


def _ffn_prompt_kernel(x_ref, g_ref, wg_ref, wu_ref, wd_ref, cw_ref, gf_ref, xo_ref, fc_ref,
                       xn_ref, carry_ref, h_ref, *, final):
    i = pl.program_id(1)
    ni = pl.num_programs(1)
    nph, rc = x_ref.shape[1], x_ref.shape[2]
    rs = RS
    tms = nph * rs

    @pl.when(i == 0)
    def _():
        carry_ref[...] = jnp.zeros_like(carry_ref)

    lo2, lo1 = (nph - 2) * rs, (nph - 1) * rs
    for s in range(rc // rs):
        x = x_ref[0, :, s * rs:(s + 1) * rs, :].reshape(tms, D_MODEL)
        xn_ref[s] = _rms(x, g_ref[...]).astype(BF16)
        for c0 in range(0, D_FF, FF_CW):
            cs = slice(c0, c0 + FF_CW)
            zg = _dot(xn_ref[s], wg_ref[:, cs])
            zu = _dot(xn_ref[s], wu_ref[:, cs])
            s2 = _shift_rows(zg[lo2:lo1], carry_ref[0:1, cs])
            s1 = _shift_rows(zg[lo1:], carry_ref[1:2, cs])
            carry_ref[0:1, cs] = zg[lo1 - 1:lo1]
            carry_ref[1:2, cs] = zg[tms - 1:tms]
            z1 = jnp.concatenate([s1, zg[:lo1]], axis=0)
            z2 = jnp.concatenate([s2, s1, zg[:lo2]], axis=0)
            gc = cw_ref[0:1, cs] * z2 + cw_ref[1:2, cs] * z1 + cw_ref[2:3, cs] * zg
            h_ref[s, :, cs] = (gc * jax.nn.sigmoid(gc) * zu).astype(BF16)
        y = x + _dot(h_ref[s], wd_ref[...])
        if final:
            xo_ref[0, s * rs:(s + 1) * rs] = jnp.swapaxes(_rms(y, gf_ref[...]).reshape(nph, rs, D_MODEL), 0, 1)
        else:
            xo_ref[0, :, s * rs:(s + 1) * rs, :] = y.reshape(nph, rs, D_MODEL)

    @pl.when(i == ni - 1)
    def _():
        fc_ref[0] = carry_ref[...]


def _ffn_prompt(x, g, w_in, w_down, layer, cw, gf, final):
    bsz, nph, nchunk, _ = x.shape
    tm = nph * RC
    out_spec, out_shape = (pl.BlockSpec((1, RC, nph, D_MODEL), lambda b, i: (b, i, 0, 0)),
                           (bsz, nchunk, nph, D_MODEL)) if final else (_chunk_tile_spec(D_MODEL), x.shape)
    return pl.pallas_call(
        functools.partial(_ffn_prompt_kernel, final=final),
        grid=(bsz, nchunk // RC),
        in_specs=[_chunk_tile_spec(D_MODEL), _const_spec((1, D_MODEL)),
                  _layer_spec((D_MODEL, D_FF), layer, 0), _layer_spec((D_MODEL, D_FF), layer, 1),
                  _layer_spec((D_FF, D_MODEL), layer), _const_spec((3, D_FF)), _const_spec((1, D_MODEL))],
        out_specs=[out_spec, pl.BlockSpec((1, 2, D_FF), lambda b, i: (b, 0, 0))],
        out_shape=[jax.ShapeDtypeStruct(out_shape, F32), jax.ShapeDtypeStruct((bsz, 2, D_FF), F32)],
        scratch_shapes=[pltpu.VMEM((RC // RS, nph * RS, D_MODEL), BF16), pltpu.VMEM((2, D_FF), F32),
                        pltpu.VMEM((RC // RS, nph * RS, D_FF), BF16)],
        compiler_params=_params(("arbitrary", "arbitrary")),
        name="ffn_prompt",
    )(x, g, w_in, w_in, w_down, cw, gf)


def _odd_in_kernel(x_ref, g_ref, win_ref, wut_ref, ccw_ref, oc_ref, ut_ref, cc_ref, cin_ref, gb_ref):
    nph, nchunk = x_ref.shape[1], x_ref.shape[2]
    for s in range(nph // PH):
        rows = slice(s * PH * nchunk, (s + 1) * PH * nchunk)
        xn = _rms(x_ref[0, s * PH:(s + 1) * PH].reshape(PH * nchunk, D_MODEL), g_ref[...]).astype(BF16)
        z = _dot(xn, win_ref[...])
        cin_ref[rows, :] = z[:, 2 * D_C:] * z[:, :D_C]
        gb_ref[rows, :] = z[:, D_C:2 * D_C]
        ut = _dot_nt(wut_ref[...], xn)
        for k in range(PH):
            ut_ref[0, s * PH + k] = ut[:, k * nchunk:(k + 1) * nchunk].astype(BF16)

    zero = jnp.zeros((1, D_C), F32)
    blk = lambda ph: cin_ref[ph * nchunk:(ph + 1) * nchunk, :]
    s2 = _shift_rows(blk(nph - 2), zero)
    s1 = _shift_rows(blk(nph - 1), zero)
    for ph in range(nph):
        p1 = blk(ph - 1) if ph >= 1 else s1
        p2 = blk(ph - 2) if ph >= 2 else (s1 if ph == 1 else s2)
        conv = ccw_ref[0:1, :] * p2 + ccw_ref[1:2, :] * p1 + ccw_ref[2:3, :] * blk(ph)
        oc_ref[0, ph] = (gb_ref[ph * nchunk:(ph + 1) * nchunk, :] * conv).astype(BF16)
    last = nchunk - 1
    cc_ref[0, 0:1, :] = cin_ref[(nph - 2) * nchunk + last:(nph - 2) * nchunk + last + 1, :]
    cc_ref[0, 1:2, :] = cin_ref[(nph - 1) * nchunk + last:(nph - 1) * nchunk + last + 1, :]


def _odd_in(x, g, w_in, layer, wut, ccw):
    bsz, nph, nchunk, _ = x.shape
    seq = lambda *shape: pl.BlockSpec((1,) + shape, lambda b: (b,) + (0,) * len(shape))
    return pl.pallas_call(
        _odd_in_kernel,
        grid=(bsz,),
        in_specs=[seq(nph, nchunk, D_MODEL), _const_spec((1, D_MODEL)), _layer_spec((D_MODEL, 3 * D_C), layer),
                  _const_spec((D_D, D_MODEL)), _const_spec((3, D_C))],
        out_specs=[seq(nph, nchunk, D_C), seq(nph, D_D, nchunk), seq(2, D_C)],
        out_shape=[jax.ShapeDtypeStruct((bsz, nph, nchunk, D_C), BF16),
                   jax.ShapeDtypeStruct((bsz, nph, D_D, nchunk), BF16),
                   jax.ShapeDtypeStruct((bsz, 2, D_C), F32)],
        scratch_shapes=[pltpu.VMEM((nph * nchunk, D_C), F32), pltpu.VMEM((nph * nchunk, D_C), F32)],
        compiler_params=_params(("arbitrary",)),
        name="odd_in",
    )(x, g, w_in, wut, ccw)


def _s5_prompt_kernel(ut_ref, mt_ref, st_ref, ot_ref, a1_ref, a2_ref, yt_ref, hs_ref, h_ref):
    bsz, nph = ut_ref.shape[0], ut_ref.shape[1]
    nchunk = ut_ref.shape[3]
    n = nph * S5_GROUP
    swap = lambda v: jnp.concatenate([v[S5_STATE:], v[:S5_STATE]], axis=0)
    for gi in range(S5_GB):
        rows = slice(gi * S5_GROUP, (gi + 1) * S5_GROUP)
        ut = jnp.concatenate([ut_ref[b, :, rows, :].reshape(n, nchunk) for b in range(bsz)], axis=-1)
        x = _dot(st_ref[gi], ut)
        cidx = lax.broadcasted_iota(jnp.int32, x.shape, 1) & (nchunk - 1)
        a1, a2 = a1_ref[gi], a2_ref[gi]

        def shift(v, sh):
            rolled = jnp.concatenate([pltpu.roll(v[:, b * nchunk:(b + 1) * nchunk], sh, 1) for b in range(bsz)],
                                     axis=1)
            return jnp.where(cidx >= sh, rolled, 0.0)

        for k in range(S5_LEVELS):
            xs = shift(x, 1 << k)
            x = x + a1[:, k:k + 1] * xs + a2[:, k:k + 1] * swap(xs)
        h_ref[gi] = x.T
        hs_ref[gi] = h_ref[gi, pl.ds(nchunk - 1, bsz, stride=nchunk), :]
        hprev = shift(x, 1)
        yt = _dot(mt_ref[gi], ut) + _dot(ot_ref[gi], hprev.astype(BF16))
        for b in range(bsz):
            yt_ref[b, :, rows, :] = yt[:, b * nchunk:(b + 1) * nchunk].reshape(nph, S5_GROUP, nchunk)


def _s5_prompt(ut, mt, st, ot, a1t, a2t, odd):
    bsz, nph, _, nchunk = ut.shape
    n, lw = nph * S5_GROUP, 2 * S5_STATE
    steps = S5_GROUPS // S5_GB
    grp = lambda *shape: pl.BlockSpec((S5_GB,) + shape, lambda g: (g + odd * steps, 0, 0))
    act = pl.BlockSpec((bsz, nph, S5_GB * S5_GROUP, nchunk), lambda g: (0, 0, g, 0))
    return pl.pallas_call(
        _s5_prompt_kernel,
        grid=(steps,),
        in_specs=[act, grp(n, n), grp(lw, n), grp(n, lw), grp(lw, 8), grp(lw, 8)],
        out_specs=[act, pl.BlockSpec((S5_GB, bsz, lw), lambda g: (g, 0, 0))],
        out_shape=[jax.ShapeDtypeStruct(ut.shape, F32), jax.ShapeDtypeStruct((S5_GROUPS, bsz, lw), F32)],
        scratch_shapes=[pltpu.VMEM((S5_GB, bsz * nchunk, lw), F32)],
        compiler_params=_params(("arbitrary",)),
        name="s5_prompt",
    )(ut, mt, st, ot, a1t, a2t)


def _odd_out_kernel(x_ref, oc_ref, yt_ref, gw_ref, gb_ref, wout_ref, xo_ref):
    nph, nchunk = x_ref.shape[1], x_ref.shape[2]
    half = nph // 2
    tms = half * nchunk
    for s in range(2):
        ps = slice(s * half, (s + 1) * half)
        y = jnp.concatenate([yt_ref[0, k].T for k in range(s * half, (s + 1) * half)], axis=0)
        g = _gelu(y)
        od = g * jax.nn.sigmoid(_dot(g.astype(BF16), gw_ref[...]) + gb_ref[...])
        out = (x_ref[0, ps].reshape(tms, D_MODEL) + _dot(oc_ref[0, ps].reshape(tms, D_C), wout_ref[0:D_C, :])
               + _dot(od.astype(BF16), wout_ref[D_C:, :]))
        xo_ref[0, ps] = out.reshape(half, nchunk, D_MODEL)


def _odd_out(x, oc, yt, glu_w, odd, gb, w_out, layer):
    bsz, nph, nchunk, _ = x.shape
    tok = lambda n, **kw: pl.BlockSpec((1, PHO, nchunk, n), lambda b, s: (b, s, 0, 0), **kw)
    deep = dict(pipeline_mode=pl.Buffered(3))
    in_specs = [tok(D_MODEL, **deep), tok(D_C, **deep),
                pl.BlockSpec((1, PHO, D_D, nchunk), lambda b, s: (b, s, 0, 0), **deep),
                pl.BlockSpec((None, D_D, D_D), lambda b, s: (odd, 0, 0)),
                pl.BlockSpec((1, D_D), lambda b, s: (0, 0)),
                pl.BlockSpec((None, D_MODEL, D_MODEL), lambda b, s: (layer, 0, 0))]

    def outer(*hbm_refs):
        pltpu.emit_pipeline(_odd_out_kernel, grid=(bsz, nph // PHO), in_specs=in_specs,
                            out_specs=[tok(D_MODEL)])(*hbm_refs)

    hbm = pl.BlockSpec(memory_space=pl.ANY)
    return pl.pallas_call(
        outer,
        in_specs=[hbm] * 6,
        out_specs=hbm,
        out_shape=jax.ShapeDtypeStruct(x.shape, F32),
        compiler_params=pltpu.CompilerParams(vmem_limit_bytes=VMEM_LIMIT),
        name="odd_out",
    )(x, oc, yt, glu_w, gb, w_out)


def _even_sample_kernel(x_ref, g_ref, win_ref, wout_ref, alng_ref, alnb_ref, ws0_ref, bs0_ref,
                        bcw_ref, bcb_ref, blng_ref, blnb_ref, buf_ref, xo_ref, v_ref, nb_ref):
    x = x_ref[...]
    z = _dot(_rms(x, g_ref[...]).astype(BF16), win_ref[...])
    za = _gelu(z[:, :2 * D_A])
    vs = []
    for h in range(A_HEADS):
        lo, hi = h * A_HEAD_DIM, (h + 1) * A_HEAD_DIM
        vs.append(_ln(za[:, D_A + lo:D_A + hi], alng_ref[:, lo:hi], alnb_ref[:, lo:hi]))
    v = jnp.concatenate(vs, axis=-1)
    v_ref[...] = v
    gate = ws0_ref[...] * v + bs0_ref[...]
    out_a = za[:, :D_A] * gate
    glu = z[:, 2 * D_A:2 * D_A + D_B] * jax.nn.sigmoid(z[:, 2 * D_A + D_B:])
    nk = B_CONV_WIDTH - 1
    buf = buf_ref[...]
    acc = bcb_ref[...] + bcw_ref[nk:nk + 1, :] * glu + jnp.sum(buf * bcw_ref[0:nk, :][None], axis=1)
    nb_ref[:, 0:nk - 1, :] = buf[:, 1:nk, :]
    nb_ref[:, nk - 1:nk, :] = glu[:, None, :]
    y = _ln(acc, blng_ref[...], blnb_ref[...])
    out_b = y * jax.nn.sigmoid(y)
    cat = jnp.concatenate([out_a, out_b], axis=-1).astype(BF16)
    xo_ref[...] = x + _dot(cat, wout_ref[...])


def _whole(shape):
    nd = len(shape)
    return pl.BlockSpec(shape, lambda *_: (0,) * nd)


def _call_whole(kernel_fn, name, args, out_shapes, scratch=(), specs=None):
    specs = specs or {}
    return pl.pallas_call(
        kernel_fn,
        grid=(1,),
        in_specs=[specs.get(i) or _const_spec(a.shape) for i, a in enumerate(args)],
        out_specs=[_whole(s.shape) for s in out_shapes],
        out_shape=out_shapes,
        scratch_shapes=list(scratch),
        compiler_params=_params(("arbitrary",)),
        name=name,
    )(*args)


def _odd_sample_kernel(x_ref, g_ref, wina_ref, wut_ref, wout_ref, ccw_ref, buf_ref, h_ref, p1_ref, p2_ref,
                       bbar_ref, ccn_ref, dsk_ref, gw_ref, gb_ref,
                       xo_ref, nb_ref, nh_ref, bd_ref, cd_ref):
    lw = 2 * S5_STATE
    bd_ref[...] = jnp.zeros_like(bd_ref)
    cd_ref[...] = jnp.zeros_like(cd_ref)
    for gi in range(S5_GROUPS):
        r0, l0 = gi * S5_GROUP, gi * lw
        bd_ref[r0:r0 + S5_GROUP, l0:l0 + lw] = bbar_ref[gi]
        cd_ref[r0:r0 + S5_GROUP, l0:l0 + lw] = ccn_ref[gi]
    x = x_ref[...]
    xn = _rms(x, g_ref[...]).astype(BF16)
    z = _dot(xn, wina_ref[...])
    u = _dot_nt(xn, wut_ref[...])
    cin = z[:, 2 * D_C:] * z[:, :D_C]
    buf = buf_ref[...]
    conv = jnp.sum(buf * ccw_ref[0:2, :][None], axis=1) + ccw_ref[2:3, :] * cin
    nb_ref[:, 0:1, :] = buf[:, 1:2, :]
    nb_ref[:, 1:2, :] = cin[:, None, :]
    out_c = z[:, D_C:2 * D_C] * conv
    h = h_ref[...]
    hswap = jnp.concatenate([pltpu.roll(h[:, gi * lw:(gi + 1) * lw], S5_STATE, 1) for gi in range(S5_GROUPS)],
                            axis=-1)
    nh = p1_ref[...] * h + p2_ref[...] * hswap + _dot(u.astype(BF16), bd_ref[...])
    nh_ref[...] = nh
    y = _dot_nt(nh.astype(BF16), cd_ref[...]) + dsk_ref[...] * u
    g = _gelu(y)
    od = g * jax.nn.sigmoid(_dot(g.astype(BF16), gw_ref[...]) + gb_ref[...])
    cat = jnp.concatenate([out_c, od], axis=-1).astype(BF16)
    xo_ref[...] = x + _dot(cat, wout_ref[...])


def _ffn_sample_kernel(x_ref, g_ref, wg_ref, wu_ref, wd_ref, cw_ref, b_ref, gf_ref,
                       xo_ref, n_ref, xn_ref, acc_ref, *, final):
    j = pl.program_id(0)

    @pl.when(j == 0)
    def _():
        xn_ref[...] = _rms(x_ref[...], g_ref[...]).astype(BF16)
        acc_ref[...] = jnp.zeros_like(acc_ref)

    zg = _dot(xn_ref[...], wg_ref[...])
    zu = _dot(xn_ref[...], wu_ref[...])
    buf = b_ref[...]
    gc = jnp.sum(buf * cw_ref[0:2, :][None], axis=1) + cw_ref[2:3, :] * zg
    n_ref[:, 0:1, :] = buf[:, 1:2, :]
    n_ref[:, 1:2, :] = zg[:, None, :]
    acc_ref[...] += _dot((gc * jax.nn.sigmoid(gc) * zu).astype(BF16), wd_ref[...])

    @pl.when(j == pl.num_programs(0) - 1)
    def _():
        y = x_ref[...] + acc_ref[...]
        if final:
            y = _rms(y, gf_ref[...])
        xo_ref[...] = y


def _ffn_sample(x, g, w_in, w_down, layer, cw, buf, gf, final):
    n = x.shape[0]
    nc = D_FF // FS_CW
    state = pl.BlockSpec((n, 2, FS_CW), lambda j: (0, 0, j))
    return pl.pallas_call(
        functools.partial(_ffn_sample_kernel, final=final),
        grid=(nc,),
        in_specs=[_whole((n, D_MODEL)), _whole((1, D_MODEL)),
                  pl.BlockSpec((None, D_MODEL, FS_CW), lambda j: (layer, 0, j)),
                  pl.BlockSpec((None, D_MODEL, FS_CW), lambda j: (layer, 0, j + nc)),
                  pl.BlockSpec((None, FS_CW, D_MODEL), lambda j: (layer, j, 0)),
                  pl.BlockSpec((3, FS_CW), lambda j: (0, j)),
                  pl.BlockSpec((None, n, 2, FS_CW), lambda j: (layer, 0, 0, j)), _whole((1, D_MODEL))],
        out_specs=[_whole((n, D_MODEL)), state],
        out_shape=[jax.ShapeDtypeStruct((n, D_MODEL), F32), jax.ShapeDtypeStruct((n, 2, D_FF), F32)],
        scratch_shapes=[pltpu.VMEM((n, D_MODEL), BF16), pltpu.VMEM((n, D_MODEL), F32)],
        compiler_params=_params(("arbitrary",)),
        name="ffn_sample",
    )(x, g, w_in, w_in, w_down, cw, buf, gf)


_S5_POW = np.concatenate([np.arange(S5_CHUNK + 1), S5_CHUNK * 2 ** np.arange(8)]).astype(np.float32)
_S5_POW_ROWS = len(_S5_POW)


def _s5_prep_kernel(pw_ref, *refs):
    for gi in range(S5_PREP_GB):
        _s5_prep_group(pw_ref, *[r.at[pl.ds(gi, 1)] for r in refs])


def _s5_prep_group(pw_ref, lr_ref, li_ref, ldt_ref, bb_ref, bbs_ref, cc_ref, ccs_ref, dd_ref,
                   mt_ref, st_ref, ot_ref, a1_ref, a2_ref, bbar_ref, ccn_ref, p1_ref, p2_ref):
    lw, lc = 2 * S5_STATE, S5_CHUNK
    lane = lax.broadcasted_iota(jnp.int32, (1, lw), 1)
    sgn = jnp.where(lane < S5_STATE, -1.0, 1.0)
    lr, li = lr_ref[0], li_ref[0]
    dt = jnp.exp(ldt_ref[0])
    er, ei = lr * dt, li * dt
    jm = pw_ref[...]
    mag = jnp.exp(jm * er)
    cr, ci = mag * jnp.cos(jm * ei), mag * jnp.sin(jm * ei)

    def scale(j, x, xs):
        return cr[j:j + 1] * x + (sgn * ci[j:j + 1]) * xs

    one = 1
    nr_, ni_ = cr[one:one + 1] - 1.0, ci[one:one + 1]
    den = lr * lr + li * li
    cfr = (nr_ * lr + ni_ * li) / den
    cfi = (ni_ * lr - nr_ * li) / den
    bb, bbs, cc, ccs = bb_ref[0], bbs_ref[0], cc_ref[0], ccs_ref[0]
    bbar = cfr * bb + (sgn * cfi) * bbs
    bbars = cfr * bbs - (sgn * cfi) * bb
    cat = lambda blocks: jnp.concatenate(blocks, axis=0)
    smat = cat([scale(lc - 1 - t, bbar, bbars) for t in range(lc)])
    st_ref[0] = smat.T.astype(BF16)
    krow = _dot_nt(cc, smat * (-sgn), precision=lax.Precision.HIGHEST)
    n = lc * S5_GROUP
    mt = cat([krow if t == lc - 1 else pltpu.roll(krow, (t + 1) * S5_GROUP, 1) for t in range(lc)])
    row = lax.broadcasted_iota(jnp.int32, (n, n), 0)
    col = lax.broadcasted_iota(jnp.int32, (n, n), 1)
    mt = jnp.where((row >> 4) >= (col >> 4), mt, 0.0)
    mt = mt + jnp.where(row == col, dd_ref[0], 0.0)
    mt_ref[0] = mt.astype(BF16)
    ot_ref[0] = cat([scale(t + 1, cc, ccs) * (-sgn) for t in range(lc)]).astype(BF16)
    a1_ref[0] = cr[lc + 1:lc + 9].T
    a2_ref[0] = (sgn * ci[lc + 1:lc + 9]).T
    bbar_ref[0] = bbar.astype(BF16)
    ccn_ref[0] = (cc * (-sgn)).astype(BF16)
    p1_ref[0] = cr[one:one + 1]
    p2_ref[0] = sgn * ci[one:one + 1]


def _s5_prepare(lam_re, lam_im, log_dt, b_re, b_im, c_re, c_im, d_skip):
    flat = lambda a: a.reshape((-1,) + a.shape[2:])
    lam_re, lam_im, log_dt, b_re, b_im, c_re, c_im, d_skip = map(
        flat, (lam_re, lam_im, log_dt, b_re, b_im, c_re, c_im, d_skip))
    ng, lw, n = lam_re.shape[0], 2 * S5_STATE, S5_CHUNK * S5_GROUP
    two = lambda a: jnp.concatenate([a, a], axis=-1)[:, None, :]
    pack = lambda a, b: jnp.concatenate([a, b], axis=-1)
    bt_re, bt_im = b_re.transpose(0, 2, 1), b_im.transpose(0, 2, 1)
    args = (jnp.asarray(np.tile(_S5_POW[:, None], (1, lw))), two(lam_re), two(lam_im),
            jnp.broadcast_to(log_dt[:, None, None], (ng, 1, lw)),
            pack(bt_re, bt_im), pack(bt_im, bt_re), pack(c_re, c_im), pack(c_im, c_re),
            jnp.tile(d_skip, (1, S5_CHUNK))[:, None, :])
    blk = lambda *shape: pl.BlockSpec((S5_PREP_GB,) + shape, lambda s: (s, 0, 0))
    shapes = [((n, n), BF16), ((lw, n), BF16), ((n, lw), BF16), ((lw, 8), F32), ((lw, 8), F32),
              ((S5_GROUP, lw), BF16), ((S5_GROUP, lw), BF16), ((1, lw), F32), ((1, lw), F32)]
    outs = pl.pallas_call(
        _s5_prep_kernel,
        grid=(ng // S5_PREP_GB,),
        in_specs=[_const_spec((_S5_POW_ROWS, lw))] + [blk(*a.shape[1:]) for a in args[1:]],
        out_specs=[blk(*sh) for sh, _ in shapes],
        out_shape=[jax.ShapeDtypeStruct((ng,) + sh, dtp) for sh, dtp in shapes],
        compiler_params=_params(("arbitrary",)),
        name="s5_prep",
    )(*args)
    return dict(zip(("mt", "st", "ot", "a1t", "a2t", "bbar", "ccn", "p1", "p2"), outs))


def _cast_kernel(w_ref, o_ref, *, transpose):
    w = w_ref[...]
    o_ref[...] = (w.T if transpose else w).astype(BF16)


def _to_bf16_all(w, tk):
    nl, k, n = w.shape
    spec = pl.BlockSpec((None, tk, n), lambda l, i: (l, i, 0))
    return pl.pallas_call(
        functools.partial(_cast_kernel, transpose=False),
        grid=(nl, k // tk),
        in_specs=[spec],
        out_specs=spec,
        out_shape=jax.ShapeDtypeStruct(w.shape, BF16),
        compiler_params=_params(("arbitrary", "arbitrary")),
        name="to_bf16_all",
    )(w)


def _to_bf16(w, layer, col_block=0, ncols=None, transpose=False):
    _, k, n = w.shape
    ncols = ncols or n
    tk = 256 if k % 256 == 0 else k
    out_block, out_map, out_shape = ((ncols, tk), lambda i: (0, i), (ncols, k)) if transpose else \
        ((tk, ncols), lambda i: (i, 0), (k, ncols))
    return pl.pallas_call(
        functools.partial(_cast_kernel, transpose=transpose),
        grid=(k // tk,),
        in_specs=[pl.BlockSpec((None, tk, ncols), lambda i: (layer, i, col_block))],
        out_specs=pl.BlockSpec(out_block, out_map),
        out_shape=jax.ShapeDtypeStruct(out_shape, BF16),
        compiler_params=_params(("arbitrary",)),
        name="to_bf16",
    )(w)


def kernel(x_prompt, x_sample, state_conv_b, state_conv_c, state_ssm_re, state_ssm_im, state_ffn_conv, norm_mix, norm_ffn, norm_final, w_mix_in, w_mix_out, a_ln_g, a_ln_b, a_ws, a_bs, b_conv_w, b_conv_b, b_ln_g, b_ln_b, c_conv_w, s5_lam_re, s5_lam_im, s5_log_dt, s5_b_re, s5_b_im, s5_c_re, s5_c_im, s5_d, s5_glu_w, s5_glu_b, ffn_w_in, ffn_conv_w, ffn_w_down):
    bsz, seq, _ = x_prompt.shape
    nsmp = x_sample.shape[0]
    nph, nchunk = S5_CHUNK, seq // S5_CHUNK
    n_odd = s5_lam_re.shape[0]
    row = lambda a: a.reshape(1, -1)
    xp = x_prompt.reshape(bsz, nchunk, nph, D_MODEL)
    xs = x_sample.reshape(nsmp, D_MODEL)
    gf = row(norm_final)
    v_rows, cb_p, cb_s, cc_p, cc_s, re_p, re_s, im_p, im_s, fc_p, fc_s = ([] for _ in range(11))
    q = CHUNK // nph
    sds = jax.ShapeDtypeStruct

    w_in_bf = _to_bf16_all(w_mix_in, D_MODEL)
    w_out_bf = _to_bf16_all(w_mix_out, D_MODEL)
    ffn_in_bf = _to_bf16_all(ffn_w_in, D_MODEL // 2)
    ffn_down_bf = _to_bf16_all(ffn_w_down, D_FF // 2)
    glu_bf = _to_bf16_all(s5_glu_w, D_D)
    p = _s5_prepare(s5_lam_re, s5_lam_im, s5_log_dt, s5_b_re, s5_b_im, s5_c_re, s5_c_im, s5_d)
    p1_rows, p2_rows = p["p1"].reshape(n_odd, 1, -1), p["p2"].reshape(n_odd, 1, -1)

    for l in range(DEPTH):
        g_mix = row(norm_mix[l])
        if l % 2 == 0:
            e = l // 2
            alng, alnb = row(a_ln_g[e]), row(a_ln_b[e])
            ws_p = a_ws[e].reshape(A_HEADS, q, nph, q, nph).transpose(0, 2, 1, 4, 3).reshape(A_HEADS, CHUNK, CHUNK)
            bs_full = jnp.repeat(a_bs[e].T, A_HEAD_DIM, axis=1)
            bs_p = bs_full.reshape(q, nph, D_A).transpose(1, 0, 2).reshape(CHUNK, D_A)
            common = (row(b_conv_b[e]), row(b_ln_g[e]), row(b_ln_b[e]))
            xp, cb = _even_prompt(xp, g_mix, w_in_bf, w_out_bf, l, alng, alnb, ws_p, bs_p, b_conv_w[e], *common,
                                  natural_in=(l == 0))
            cb_p.append(cb)
            ws0 = row(jnp.repeat(a_ws[e][:, 0, 0], A_HEAD_DIM))
            bs0 = row(jnp.repeat(a_bs[e][:, 0], A_HEAD_DIM))
            xs, v, nb = _call_whole(
                _even_sample_kernel, "even_sample",
                (xs, g_mix, w_in_bf, w_out_bf, alng, alnb, ws0, bs0, b_conv_w[e], *common, state_conv_b),
                [sds((nsmp, D_MODEL), F32), sds((nsmp, D_A), F32), sds(state_conv_b.shape[1:], F32)],
                specs={2: _layer_spec((D_MODEL, D_IN), l), 3: _layer_spec((D_MODEL, D_MODEL), l),
                       12: _lead_spec(state_conv_b.shape[1:], e)})
            v_rows.append(v.reshape(nsmp, 1, D_A))
            cb_s.append(nb)
        else:
            o = l // 2
            wut = _to_bf16(w_mix_in, l, 3, D_D, transpose=True)
            gb = row(s5_glu_b[o])
            oc, ut, cc = _odd_in(xp, g_mix, w_in_bf, l, wut, c_conv_w[o])
            cc_p.append(cc)
            yt, hs = _s5_prompt(ut, p["mt"], p["st"], p["ot"], p["a1t"], p["a2t"], o)
            hs = hs.transpose(1, 0, 2)
            re_p.append(hs[..., :S5_STATE])
            im_p.append(hs[..., S5_STATE:])
            xp = _odd_out(xp, oc, yt, glu_bf, o, gb, w_out_bf, l)
            hin =jnp.concatenate([state_ssm_re[o], state_ssm_im[o]], axis=-1).reshape(nsmp, -1)
            nstate = S5_GROUPS * 2 * S5_STATE
            grp_spec = pl.BlockSpec((S5_GROUPS, S5_GROUP, 2 * S5_STATE), lambda *_: (o, 0, 0),
                                    pipeline_mode=pl.Buffered(1))
            xs, nbc, nh = _call_whole(
                _odd_sample_kernel, "odd_sample",
                (xs, g_mix, w_in_bf, wut, w_out_bf, c_conv_w[o], state_conv_c, hin, p1_rows[o], p2_rows[o],
                 p["bbar"], p["ccn"], row(s5_d[o]), glu_bf, gb),
                [sds((nsmp, D_MODEL), F32), sds(state_conv_c.shape[1:], F32), sds(hin.shape, F32)],
                scratch=[pltpu.VMEM((D_D, nstate), BF16), pltpu.VMEM((D_D, nstate), BF16)],
                specs={2: _layer_spec((D_MODEL, 3 * D_C), l), 4: _layer_spec((D_MODEL, D_MODEL), l),
                       6: _lead_spec(state_conv_c.shape[1:], o),
                       10: grp_spec, 11: grp_spec, 13: _layer_spec((D_D, D_D), o)})
            cc_s.append(nbc)
            nh = nh.reshape(nsmp, S5_GROUPS, 2 * S5_STATE)
            re_s.append(nh[..., :S5_STATE])
            im_s.append(nh[..., S5_STATE:])
        final = l == DEPTH - 1
        g_ffn = row(norm_ffn[l])
        xp, fc = _ffn_prompt(xp, g_ffn, ffn_in_bf, ffn_down_bf, l, ffn_conv_w[l], gf, final)
        fc_p.append(fc)
        xs, nfc = _ffn_sample(xs, g_ffn, ffn_in_bf, ffn_down_bf, l, ffn_conv_w[l], state_ffn_conv, gf, final)
        fc_s.append(nfc)

    st = jnp.stack
    return (xp.reshape(bsz, seq, D_MODEL), xs.reshape(nsmp, 1, D_MODEL), st(v_rows), st(cb_p), st(cb_s),
            st(cc_p), st(cc_s), st(re_p), st(re_s), st(im_p), st(im_s), st(fc_p), st(fc_s))
```
